```python
import jax, jax.numpy as jnp
from jax import lax
import numpy as np

D_MODEL = 1024
BATCH = 8
SEQ = 2048
DEPTH = 4

GRID_W = 64
CTX_LEN = 256
N_MIXERS = 4
HEAD_DIM = 64
ROPE_THETA = 10000.0
NA_HEADS = 16
NA_WIN_H = 8
NA_WIN_W = 16
GQA_HEADS = 16
GQA_KV_HEADS = 4
SWA_HEADS = 16
SWA_KV_HEADS = 2
SWA_WINDOW = 128
Q_BLOCK = 128
LRU_WIDTH = D_MODEL
LRU_BLOCKS = 16
LRU_BLOCK_DIM = LRU_WIDTH // LRU_BLOCKS
CONV_WIDTH = 4
LRU_C = 8.0
FFN_HIDDEN = ((8 * D_MODEL + 2) // 3 + 255) // 256 * 256
N_MOD = 6
EPS = 1e-6
MASK_VALUE = -1e30
N_NA_LAYERS = (DEPTH + 3) // N_MIXERS
N_GQA_LAYERS = (DEPTH + 2) // N_MIXERS
N_SWA_LAYERS = (DEPTH + 1) // N_MIXERS
N_LRU_LAYERS = DEPTH // N_MIXERS

kernel_name = 'hybrid_dit_interleaved_na_gqa_swa_rglru'


def rms_norm(x, g):
    xf = x.astype(jnp.float32)
    y = xf * lax.rsqrt(jnp.mean(xf * xf, axis=-1, keepdims=True) + EPS)
    return (y * g.astype(jnp.float32)).astype(x.dtype)


def ada_norm(x, g, shift, scale):
    return rms_norm(x, g) * (1.0 + scale) + shift


def axial_angles(n_tokens):
    t = jnp.arange(n_tokens, dtype=jnp.int32)
    row = (t // GRID_W).astype(jnp.float32)
    col = (t % GRID_W).astype(jnp.float32)
    half = HEAD_DIM // 2
    inv_freq = 1.0 / (ROPE_THETA ** (jnp.arange(0, half, 2, dtype=jnp.float32) / half))
    return row[:, None] * inv_freq, col[:, None] * inv_freq


def _rotate(x, ang):
    cos = jnp.cos(ang)[None, :, None, :].astype(x.dtype)
    sin = jnp.sin(ang)[None, :, None, :].astype(x.dtype)
    x1, x2 = jnp.split(x, 2, axis=-1)
    return jnp.concatenate([x1 * cos - x2 * sin, x1 * sin + x2 * cos], axis=-1)


def axial_rope(x, ang_row, ang_col):
    xr, xc = jnp.split(x, 2, axis=-1)
    return jnp.concatenate([_rotate(xr, ang_row), _rotate(xc, ang_col)], axis=-1)


def project_qkv(h, w_qkv, n_q, n_kv):
    b, t, _ = h.shape
    qkv = h @ w_qkv
    q, k, v = jnp.split(qkv, [n_q * HEAD_DIM, (n_q + n_kv) * HEAD_DIM], axis=-1)
    return (q.reshape(b, t, n_q, HEAD_DIM), k.reshape(b, t, n_kv, HEAD_DIM), v.reshape(b, t, n_kv, HEAD_DIM))


def attn_probs(s, sink=None):
    if sink is None:
        return jax.nn.softmax(s, axis=-1)
    sk = sink.astype(jnp.float32)[None, :, :, None, None]
    m = jnp.maximum(jnp.max(s, axis=-1, keepdims=True), sk)
    e = jnp.exp(s - m)
    return e / (jnp.sum(e, axis=-1, keepdims=True) + jnp.exp(sk - m))


def context_attention(q, k, v, sink=None):
    b, t, n_q, dh = q.shape
    n_kv = k.shape[2]
    qg = q.reshape(b, t, n_kv, n_q // n_kv, dh)
    s = jnp.einsum('bqhgd,bshd->bhgqs', qg, k).astype(jnp.float32) * dh ** -0.5
    p = attn_probs(s, sink).astype(v.dtype)
    o = jnp.einsum('bhgqs,bshd->bqhgd', p, v)
    return o.reshape(b, t, n_q * dh)


def neighbourhood_attention_mixer(h_lat, h_ctx, w_qkv, rpb, w_o, with_ctx_out):
    b, s_len, _ = h_lat.shape
    rows = s_len // GRID_W
    kh = min(NA_WIN_H, rows)
    scale = HEAD_DIM ** -0.5
    q, k, v = project_qkv(h_lat, w_qkv, NA_HEADS, NA_HEADS)
    qc, kc, vc = project_qkv(h_ctx, w_qkv, NA_HEADS, NA_HEADS)
    k_grid = k.reshape(b, rows, GRID_W, NA_HEADS, HEAD_DIM)
    v_grid = v.reshape(b, rows, GRID_W, NA_HEADS, HEAD_DIM)
    q_rows = jnp.moveaxis(q.reshape(b, rows, GRID_W, NA_HEADS, HEAD_DIM), 1, 0)
    qcol = jnp.arange(GRID_W, dtype=jnp.int32)
    kcol = jnp.arange(GRID_W, dtype=jnp.int32)
    col_start = jnp.clip(qcol - NA_WIN_W // 2, 0, GRID_W - NA_WIN_W)
    col_in = (kcol[None, :] >= col_start[:, None]) & (kcol[None, :] < col_start[:, None] + NA_WIN_W)
    dcol = jnp.clip(kcol[None, :] - qcol[:, None] + NA_WIN_W - 1, 0, 2 * NA_WIN_W - 2)
    n_nb = kh * GRID_W

    def row_block(args):
        r, q_r = args
        r0 = jnp.clip(r - kh // 2, 0, rows - kh)
        k_b = lax.dynamic_slice_in_dim(k_grid, r0, kh, axis=1)
        v_b = lax.dynamic_slice_in_dim(v_grid, r0, kh, axis=1).reshape(b, n_nb, NA_HEADS, HEAD_DIM)
        drow = r0 + jnp.arange(kh, dtype=jnp.int32) - r + NA_WIN_H - 1
        bias = rpb[:, drow[None, :, None], dcol[:, None, :]]
        s_nb = jnp.einsum('bqhd,biwhd->bhqiw', q_r, k_b).astype(jnp.float32) * scale + bias.astype(jnp.float32)[None]
        s_nb = jnp.where(col_in[None, None, :, None, :], s_nb, MASK_VALUE).reshape(b, NA_HEADS, GRID_W, n_nb)
        s_cx = jnp.einsum('bqhd,bshd->bhqs', q_r, kc).astype(jnp.float32) * scale
        p = jax.nn.softmax(jnp.concatenate([s_nb, s_cx], axis=-1), axis=-1).astype(v.dtype)
        return (jnp.einsum('bhqn,bnhd->bqhd', p[..., :n_nb], v_b)
                + jnp.einsum('bhqs,bshd->bqhd', p[..., n_nb:], vc))

    o = lax.map(row_block, (jnp.arange(rows, dtype=jnp.int32), q_rows))
    y_lat = jnp.moveaxis(o, 0, 1).reshape(b, s_len, NA_HEADS * HEAD_DIM) @ w_o
    y_ctx = context_attention(qc, kc, vc) @ w_o if with_ctx_out else None
    return y_lat, y_ctx


def qknorm_gqa_mixer(h_lat, h_ctx, w_qkv, q_gain, k_gain, w_o, ang_row, ang_col, with_ctx_out):
    b, s_len, _ = h_lat.shape
    g = GQA_HEADS // GQA_KV_HEADS
    scale = HEAD_DIM ** -0.5
    q, k, v = project_qkv(h_lat, w_qkv, GQA_HEADS, GQA_KV_HEADS)
    qc, kc, vc = project_qkv(h_ctx, w_qkv, GQA_HEADS, GQA_KV_HEADS)
    q = axial_rope(rms_norm(q, q_gain), ang_row, ang_col)
    k = axial_rope(rms_norm(k, k_gain), ang_row, ang_col)
    qc = rms_norm(qc, q_gain)
    kc = rms_norm(kc, k_gain)
    k_all = jnp.concatenate([kc, k], axis=1)
    v_all = jnp.concatenate([vc, v], axis=1)
    nb = s_len // Q_BLOCK
    q_blocks = jnp.moveaxis(q.reshape(b, nb, Q_BLOCK, GQA_KV_HEADS, g, HEAD_DIM), 1, 0)

    def block(q_b):
        s = jnp.einsum('bqhgd,bshd->bhgqs', q_b, k_all).astype(jnp.float32) * scale
        p = jax.nn.softmax(s, axis=-1).astype(v_all.dtype)
        return jnp.einsum('bhgqs,bshd->bqhgd', p, v_all)

    o = lax.map(block, q_blocks)
    y_lat = jnp.moveaxis(o, 0, 1).reshape(b, s_len, GQA_HEADS * HEAD_DIM) @ w_o
    y_ctx = context_attention(qc, kc, vc) @ w_o if with_ctx_out else None
    return y_lat, y_ctx


def sliding_window_mixer(h_lat, h_ctx, w_qkv, sinks, w_o, ang_row, ang_col, with_ctx_out):
    b, s_len, _ = h_lat.shape
    g = SWA_HEADS // SWA_KV_HEADS
    scale = HEAD_DIM ** -0.5
    q, k, v = project_qkv(h_lat, w_qkv, SWA_HEADS, SWA_KV_HEADS)
    qc, kc, vc = project_qkv(h_ctx, w_qkv, SWA_HEADS, SWA_KV_HEADS)
    q = axial_rope(q, ang_row, ang_col)
    k = axial_rope(k, ang_row, ang_col)
    sink = sinks.reshape(SWA_KV_HEADS, g)
    band = Q_BLOCK + 2 * SWA_WINDOW
    pad = ((0, 0), (SWA_WINDOW, SWA_WINDOW), (0, 0), (0, 0))
    k_pad = jnp.pad(k, pad)
    v_pad = jnp.pad(v, pad)
    nb = s_len // Q_BLOCK
    q_blocks = jnp.moveaxis(q.reshape(b, nb, Q_BLOCK, SWA_KV_HEADS, g, HEAD_DIM), 1, 0)

    def block(args):
        j, q_b = args
        start = j * Q_BLOCK
        k_b = lax.dynamic_slice_in_dim(k_pad, start, band, axis=1)
        v_b = lax.dynamic_slice_in_dim(v_pad, start, band, axis=1)
        qpos = start + jnp.arange(Q_BLOCK, dtype=jnp.int32)
        kpos = start - SWA_WINDOW + jnp.arange(band, dtype=jnp.int32)
        valid = ((jnp.abs(qpos[:, None] - kpos[None, :]) <= SWA_WINDOW)
                 & (kpos[None, :] >= 0) & (kpos[None, :] < s_len))
        s_loc = jnp.einsum('bqhgd,bshd->bhgqs', q_b, k_b).astype(jnp.float32) * scale
        s_loc = jnp.where(valid, s_loc, MASK_VALUE)
        s_cx = jnp.einsum('bqhgd,bshd->bhgqs', q_b, kc).astype(jnp.float32) * scale
        p = attn_probs(jnp.concatenate([s_loc, s_cx], axis=-1), sink).astype(v.dtype)
        return (jnp.einsum('bhgqs,bshd->bqhgd', p[..., :band], v_b)
                + jnp.einsum('bhgqs,bshd->bqhgd', p[..., band:], vc))

    o = lax.map(block, (jnp.arange(nb, dtype=jnp.int32), q_blocks))
    y_lat = jnp.moveaxis(o, 0, 1).reshape(b, s_len, SWA_HEADS * HEAD_DIM) @ w_o
    y_ctx = context_attention(qc, kc, vc, sink) @ w_o if with_ctx_out else None
    return y_lat, y_ctx


def centred_depthwise_conv(x, w, bias):
    t = x.shape[1]
    left = CONV_WIDTH // 2
    xp = jnp.pad(x, ((0, 0), (left, CONV_WIDTH - 1 - left), (0, 0)))
    return sum(xp[:, kk:kk + t] * w[kk] for kk in range(CONV_WIDTH)) + bias


def rglru_coeffs(x, w_a, b_a, w_x, b_x, lam):
    b, t, _ = x.shape
    xf = x.astype(jnp.float32)
    xb = xf.reshape(b, t, LRU_BLOCKS, LRU_BLOCK_DIM)
    r = jax.nn.sigmoid(jnp.einsum('btnd,nde->btne', xb, w_a.astype(jnp.float32)).reshape(b, t, LRU_WIDTH) + b_a.astype(jnp.float32))
    ig = jax.nn.sigmoid(jnp.einsum('btnd,nde->btne', xb, w_x.astype(jnp.float32)).reshape(b, t, LRU_WIDTH) + b_x.astype(jnp.float32))
    log_a = -LRU_C * r * jax.nn.softplus(-lam.astype(jnp.float32))
    return jnp.exp(log_a), jnp.sqrt(-jnp.expm1(2.0 * log_a)) * (ig * xf)


def linear_scan(a, bx, h0):
    def combine(left, right):
        a_l, b_l = left
        a_r, b_r = right
        return a_r * a_l, a_r * b_l + b_r
    a_cum, b_cum = lax.associative_scan(combine, (a, bx), axis=1)
    return a_cum * h0[:, None, :] + b_cum


def rglru_mixer(h_lat, h_ctx, w_in, conv_w, conv_b, w_a, b_a, w_x, b_x, lam, w_out, with_ctx_out):
    def branches(h):
        xr, gate = jnp.split(h @ w_in, 2, axis=-1)
        return centred_depthwise_conv(xr, conv_w, conv_b), jax.nn.gelu(gate, approximate=True)

    xc, gc = branches(h_ctx)
    xl, gl = branches(h_lat)
    hc_sum = jnp.zeros(xc.shape, jnp.float32)
    hl_sum = jnp.zeros(xl.shape, jnp.float32)
    for d in range(2):
        ac, bc = rglru_coeffs(xc, w_a[d], b_a[d], w_x[d], b_x[d], lam[d])
        al, bl = rglru_coeffs(xl, w_a[d], b_a[d], w_x[d], b_x[d], lam[d])
        if d == 1:
            ac, bc, al, bl = jnp.flip(ac, 1), jnp.flip(bc, 1), jnp.flip(al, 1), jnp.flip(bl, 1)
        hc = linear_scan(ac, bc, jnp.zeros_like(bc[:, 0]))
        hl = linear_scan(al, bl, hc[:, -1])
        if d == 1:
            hc, hl = jnp.flip(hc, 1), jnp.flip(hl, 1)
        hc_sum = hc_sum + hc
        hl_sum = hl_sum + hl
    y_lat = (hl_sum.astype(h_lat.dtype) * gl) @ w_out
    y_ctx = (hc_sum.astype(h_ctx.dtype) * gc) @ w_out if with_ctx_out else None
    return y_lat, y_ctx


def swiglu_ffn(h, w_in, w_out):
    a, g = jnp.split(h @ w_in, 2, axis=-1)
    return (jax.nn.silu(a) * g) @ w_out


def setup_inputs(seed: int = 0) -> dict:
    key = jax.random.key(seed)
    ks = jax.random.split(key, 32)

    def nrm(k, shape, scale):
        return jax.random.normal(k, shape, jnp.float32) * scale

    def gain(k, shape):
        return 1.0 + 0.02 * jax.random.normal(k, shape, jnp.float32)

    d = D_MODEL
    a0 = jnp.sqrt(jax.random.uniform(ks[26], (N_LRU_LAYERS, 2, LRU_WIDTH), jnp.float32, 0.81, 0.998))
    return {
        'x': nrm(ks[0], (BATCH, SEQ, d), 1.0),
        'c': nrm(ks[1], (BATCH, d), 1.0),
        'ctx': nrm(ks[2], (BATCH, CTX_LEN, d), 1.0),
        'c_ctx': nrm(ks[3], (d,), 1.0),
        'ada_w': nrm(ks[4], (DEPTH, d, N_MOD * d), 0.5 * d ** -0.5),
        'ada_b': nrm(ks[5], (DEPTH, N_MOD * d), 0.02),
        'norm_mix': gain(ks[6], (DEPTH, d)),
        'norm_ffn': gain(ks[7], (DEPTH, d)),
        'norm_final': gain(ks[8], (d,)),
        'ffn_w_in': nrm(ks[9], (DEPTH, d, 2 * FFN_HIDDEN), d ** -0.5),
        'ffn_w_out': nrm(ks[10], (DEPTH, FFN_HIDDEN, d), FFN_HIDDEN ** -0.5),
        'na_w_qkv': nrm(ks[11], (N_NA_LAYERS, d, 3 * NA_HEADS * HEAD_DIM), d ** -0.5),
        'na_rpb': nrm(ks[12], (N_NA_LAYERS, NA_HEADS, 2 * NA_WIN_H - 1, 2 * NA_WIN_W - 1), 0.1),
        'na_w_o': nrm(ks[13], (N_NA_LAYERS, NA_HEADS * HEAD_DIM, d), (NA_HEADS * HEAD_DIM) ** -0.5),
        'gqa_w_qkv': nrm(ks[14], (N_GQA_LAYERS, d, (GQA_HEADS + 2 * GQA_KV_HEADS) * HEAD_DIM), d ** -0.5),
        'gqa_q_gain': gain(ks[15], (N_GQA_LAYERS, HEAD_DIM)),
        'gqa_k_gain': gain(ks[16], (N_GQA_LAYERS, HEAD_DIM)),
        'gqa_w_o': nrm(ks[17], (N_GQA_LAYERS, GQA_HEADS * HEAD_DIM, d), (GQA_HEADS * HEAD_DIM) ** -0.5),
        'swa_w_qkv': nrm(ks[18], (N_SWA_LAYERS, d, (SWA_HEADS + 2 * SWA_KV_HEADS) * HEAD_DIM), d ** -0.5),
        'swa_sinks': nrm(ks[19], (N_SWA_LAYERS, SWA_HEADS), 0.5),
        'swa_w_o': nrm(ks[20], (N_SWA_LAYERS, SWA_HEADS * HEAD_DIM, d), (SWA_HEADS * HEAD_DIM) ** -0.5),
        'lru_w_in': nrm(ks[21], (N_LRU_LAYERS, d, 2 * LRU_WIDTH), d ** -0.5),
        'lru_conv_w': nrm(ks[22], (N_LRU_LAYERS, CONV_WIDTH, LRU_WIDTH), CONV_WIDTH ** -0.5),
        'lru_conv_b': nrm(ks[23], (N_LRU_LAYERS, LRU_WIDTH), 0.02),
        'lru_w_a': nrm(ks[24], (N_LRU_LAYERS, 2, LRU_BLOCKS, LRU_BLOCK_DIM, LRU_BLOCK_DIM), LRU_BLOCK_DIM ** -0.5),
        'lru_b_a': nrm(ks[25], (N_LRU_LAYERS, 2, LRU_WIDTH), 0.02),
        'lru_w_x': nrm(ks[27], (N_LRU_LAYERS, 2, LRU_BLOCKS, LRU_BLOCK_DIM, LRU_BLOCK_DIM), LRU_BLOCK_DIM ** -0.5),
        'lru_b_x': nrm(ks[28], (N_LRU_LAYERS, 2, LRU_WIDTH), 0.02),
        'lru_lam': jnp.log(a0) - jnp.log1p(-a0),
        'lru_w_out': nrm(ks[29], (N_LRU_LAYERS, LRU_WIDTH, d), LRU_WIDTH ** -0.5),
    }


def reference(x, c, ctx, c_ctx, ada_w, ada_b, norm_mix, norm_ffn, norm_final, ffn_w_in, ffn_w_out,
              na_w_qkv, na_rpb, na_w_o, gqa_w_qkv, gqa_q_gain, gqa_k_gain, gqa_w_o,
              swa_w_qkv, swa_sinks, swa_w_o, lru_w_in, lru_conv_w, lru_conv_b, lru_w_a, lru_b_a,
              lru_w_x, lru_b_x, lru_lam, lru_w_out):
    s_len = x.shape[1]
    ang_row, ang_col = axial_angles(s_len)
    silu_c = jax.nn.silu(c)
    silu_cc = jax.nn.silu(c_ctx)
    h = x
    hc = ctx
    for i in range(DEPTH):
        kind = i % N_MIXERS
        slot = i // N_MIXERS
        ctx_out = i < DEPTH - 1
        mod_l = (silu_c @ ada_w[i] + ada_b[i])[:, None, :]
        mod_c = (silu_cc @ ada_w[i] + ada_b[i])[None, None, :]
        sh1_l, sc1_l, g1_l, sh2_l, sc2_l, g2_l = jnp.split(mod_l, N_MOD, axis=-1)
        sh1_c, sc1_c, g1_c, sh2_c, sc2_c, g2_c = jnp.split(mod_c, N_MOD, axis=-1)
        a_l = ada_norm(h, norm_mix[i], sh1_l, sc1_l)
        a_c = ada_norm(hc, norm_mix[i], sh1_c, sc1_c)
        if kind == 0:
            y_l, y_c = neighbourhood_attention_mixer(a_l, a_c, na_w_qkv[slot], na_rpb[slot], na_w_o[slot], ctx_out)
        elif kind == 1:
            y_l, y_c = qknorm_gqa_mixer(a_l, a_c, gqa_w_qkv[slot], gqa_q_gain[slot], gqa_k_gain[slot],
                                        gqa_w_o[slot], ang_row, ang_col, ctx_out)
        elif kind == 2:
            y_l, y_c = sliding_window_mixer(a_l, a_c, swa_w_qkv[slot], swa_sinks[slot], swa_w_o[slot],
                                            ang_row, ang_col, ctx_out)
        else:
            y_l, y_c = rglru_mixer(a_l, a_c, lru_w_in[slot], lru_conv_w[slot], lru_conv_b[slot],
                                   lru_w_a[slot], lru_b_a[slot], lru_w_x[slot], lru_b_x[slot],
                                   lru_lam[slot], lru_w_out[slot], ctx_out)
        h = h + g1_l * y_l
        h = h + g2_l * swiglu_ffn(ada_norm(h, norm_ffn[i], sh2_l, sc2_l), ffn_w_in[i], ffn_w_out[i])
        if ctx_out:
            hc = hc + g1_c * y_c
            hc = hc + g2_c * swiglu_ffn(ada_norm(hc, norm_ffn[i], sh2_c, sc2_c), ffn_w_in[i], ffn_w_out[i])
    return rms_norm(h, norm_final)
```

```python
import functools

import jax
import jax.numpy as jnp
from jax import lax
from jax.experimental import pallas as pl
from jax.experimental.pallas import tpu as pltpu

F32 = jnp.float32
BF16 = jnp.bfloat16

D_MODEL = 1024
BATCH = 8
SEQ = 2048
DEPTH = 4
GRID_W = 64
CTX_LEN = 256
HEAD_DIM = 64
ROPE_THETA = 10000.0
NA_HEADS = 16
NA_WIN_H = 8
NA_WIN_W = 16
GQA_HEADS = 16
GQA_KV_HEADS = 4
SWA_HEADS = 16
SWA_KV_HEADS = 2
SWA_WINDOW = 128
LRU_BLOCKS = 16
LRU_BLOCK_DIM = D_MODEL // LRU_BLOCKS
CONV_WIDTH = 4
LRU_C = 8.0
FFN_HIDDEN = 2816
N_MOD = 6
EPS = 1e-6
MASK_VALUE = -1e30

N_CTX_ROWS = BATCH * CTX_LEN
N_LAT_ROWS = BATCH * SEQ
N_ROWS = N_CTX_ROWS + N_LAT_ROWS
MOD_ROWS = 16
CTX_MOD_ROW = BATCH

LANES = 128
TM = 512
TQ = 256
FFN_CHUNK = 256
VMEM_LIMIT = 56 * 1024 * 1024

N_CTX_TILES = N_CTX_ROWS // TM
TILES_PER_BATCH = SEQ // TM
NA_Q_ROWS = TQ // GRID_W
NA_K_ROWS = 12
NA_KEYS = NA_K_ROWS * GRID_W

_NT = (((1,), (1,)), ((), ()))


def _cparams(sem):
    return pltpu.CompilerParams(dimension_semantics=sem, vmem_limit_bytes=VMEM_LIMIT)


def _tile_mod_row(i):
    return jnp.where(i < N_CTX_TILES, CTX_MOD_ROW, (i - N_CTX_TILES) // TILES_PER_BATCH)


def _tile_pos_block(i):
    return jnp.where(i < N_CTX_TILES, 0, 1 + (i - N_CTX_TILES) % TILES_PER_BATCH)


def _ada_norm(x, g, shift, scale):
    ms = jnp.mean(x * x, axis=-1, keepdims=True)
    y = x * lax.rsqrt(ms + EPS)
    return (y * g) * (1.0 + scale) + shift


def _mod_kernel(c_ref, w_ref, b_ref, o_ref):
    c = c_ref[...]
    sc = (c * jax.nn.sigmoid(c)).astype(BF16)
    o_ref[0] = jnp.dot(sc, w_ref[0].astype(BF16), preferred_element_type=F32) + b_ref[0]


def _modulation(c_all, ada_w, ada_b):
    tn = 1536
    n = N_MOD * D_MODEL
    return pl.pallas_call(
        _mod_kernel,
        out_shape=jax.ShapeDtypeStruct((DEPTH, MOD_ROWS, n), F32),
        grid=(DEPTH, n // tn),
        in_specs=[
            pl.BlockSpec((MOD_ROWS, D_MODEL), lambda l, j: (0, 0)),
            pl.BlockSpec((1, D_MODEL, tn), lambda l, j: (l, 0, j)),
            pl.BlockSpec((1, 1, tn), lambda l, j: (l, 0, j)),
        ],
        out_specs=pl.BlockSpec((1, MOD_ROWS, tn), lambda l, j: (l, 0, j)),
        compiler_params=_cparams(("arbitrary", "arbitrary")),
        name="modulation",
    )(c_all, ada_w, ada_b.reshape(DEPTH, 1, n))


def _proj_kernel(*refs, segs, has_rope):
    x_ref, mod_ref, g_ref, w_ref = refs[:4]
    k = 4
    if has_rope:
        cos_ref, s1_ref, s2_ref, gain_ref = refs[4:8]
        k = 8
    out_refs = refs[k:]
    xn = _ada_norm(x_ref[...], g_ref[...], mod_ref[0:1, :], mod_ref[1:2, :]).astype(BF16)
    lane = lax.broadcasted_iota(jnp.int32, (1, LANES), 1)
    lo = lane < HEAD_DIM
    col = 0
    for seg, o_ref in zip(segs, out_refs):
        width, kind, norm_idx, scale = seg
        chunk = 256
        for c0 in range(0, width, chunk):
            cw = min(chunk, width - c0)
            y = jnp.dot(xn, w_ref[:, col + c0:col + c0 + cw], preferred_element_type=F32)
            if kind == "gelu":
                y = jax.nn.gelu(y, approximate=True)
            elif kind == "rope":
                parts = []
                for b0 in range(0, cw, LANES):
                    blk = y[:, b0:b0 + LANES]
                    if norm_idx is not None:
                        ss = blk * blk
                        s_lo = jnp.sum(jnp.where(lo, ss, 0.0), axis=-1, keepdims=True)
                        s_hi = jnp.sum(jnp.where(lo, 0.0, ss), axis=-1, keepdims=True)
                        ms = jnp.where(lo, s_lo, s_hi) * (1.0 / HEAD_DIM)
                        blk = (blk * lax.rsqrt(ms + EPS)) * gain_ref[norm_idx:norm_idx + 1, :]
                    blk = (blk * cos_ref[...]
                           + pltpu.roll(blk, 16, 1) * s1_ref[...]
                           + pltpu.roll(blk, LANES - 16, 1) * s2_ref[...])
                    parts.append(blk)
                y = parts[0] if len(parts) == 1 else jnp.concatenate(parts, axis=-1)
            if scale != 1.0:
                y = y * scale
            o_ref[:, c0:c0 + cw] = y.astype(o_ref.dtype)
        col += width


def _project(hx, mod_l, g, w, segs, out_dtypes, rope_tabs=None, gains=None):
    n_tot = sum(s[0] for s in segs)
    in_specs = [
        pl.BlockSpec((TM, D_MODEL), lambda i: (i, 0)),
        pl.BlockSpec((None, N_MOD, D_MODEL), lambda i: (_tile_mod_row(i), 0, 0)),
        pl.BlockSpec((1, D_MODEL), lambda i: (0, 0)),
        pl.BlockSpec((D_MODEL, n_tot), lambda i: (0, 0), pipeline_mode=pl.Buffered(1)),
    ]
    args = [hx, mod_l, g.reshape(1, D_MODEL), w]
    has_rope = rope_tabs is not None
    if has_rope:
        for t in rope_tabs:
            in_specs.append(pl.BlockSpec((TM, LANES), lambda i: (_tile_pos_block(i), 0)))
            args.append(t)
        in_specs.append(pl.BlockSpec(gains.shape, lambda i: (0, 0)))
        args.append(gains)
    out_shape = [jax.ShapeDtypeStruct((N_ROWS, s[0]), dt) for s, dt in zip(segs, out_dtypes)]
    out_specs = [pl.BlockSpec((TM, s[0]), lambda i: (i, 0)) for s in segs]
    return pl.pallas_call(
        functools.partial(_proj_kernel, segs=segs, has_rope=has_rope),
        out_shape=out_shape,
        grid=(N_ROWS // TM,),
        in_specs=in_specs,
        out_specs=out_specs,
        compiler_params=_cparams(("arbitrary",)),
        name="ada_proj",
    )(*args)


def _softmax_pv(q, k_ctx, v_ctx, k_lat, v_lat, bias, sink):
    s_c = lax.dot_general(q, k_ctx, _NT, preferred_element_type=F32)
    m = jnp.max(s_c, axis=-1, keepdims=True)
    if k_lat is not None:
        s_l = lax.dot_general(q, k_lat, _NT, preferred_element_type=F32)
        if bias is not None:
            s_l = s_l + bias
        m = jnp.maximum(m, jnp.max(s_l, axis=-1, keepdims=True))
    if sink is not None:
        m = jnp.maximum(m, sink)
    e_c = jnp.exp(s_c - m)
    den = jnp.sum(e_c, axis=-1, keepdims=True)
    o = jnp.dot(e_c.astype(BF16), v_ctx, preferred_element_type=F32)
    if k_lat is not None:
        e_l = jnp.exp(s_l - m)
        den = den + jnp.sum(e_l, axis=-1, keepdims=True)
        o = o + jnp.dot(e_l.astype(BF16), v_lat, preferred_element_type=F32)
    if sink is not None:
        den = den + jnp.exp(sink - m)
    return o / den


def _head_pair(q128, group, pair, kv_blocks, bias_fn, sink_fn):
    lane = lax.broadcasted_iota(jnp.int32, (1, LANES), 1)
    lo = lane < HEAD_DIM
    outs = []
    for half in range(2):
        head = 2 * pair + half
        kv_head = head // group
        kv_half = kv_head % 2
        q = q128 if kv_half == half else pltpu.roll(q128, HEAD_DIM, 1)
        q = jnp.where(lo if kv_half == 0 else jnp.logical_not(lo), q, jnp.zeros_like(q))
        k_ctx, v_ctx, k_lat, v_lat = kv_blocks(kv_head // 2)
        o = _softmax_pv(q, k_ctx, v_ctx, k_lat, v_lat,
                        None if bias_fn is None else bias_fn(half),
                        None if sink_fn is None else sink_fn(head))
        if kv_half != half:
            o = pltpu.roll(o, HEAD_DIM, 1)
        outs.append(o)
    return jnp.where(lo, outs[0], outs[1]).astype(BF16)


def _ctx_attn_kernel(*refs, n_heads, group, has_sink):
    if has_sink:
        sink_ref, refs = refs[0], refs[1:]
    q_ref, k_ref, v_ref, _, o_ref = refs

    def kv_blocks(kp):
        sl = slice(kp * LANES, (kp + 1) * LANES)
        return k_ref[:, sl], v_ref[:, sl], None, None

    sink_fn = (lambda h: sink_ref[h]) if has_sink else None
    for pair in range(n_heads // 2):
        sl = slice(pair * LANES, (pair + 1) * LANES)
        o_ref[:, sl] = _head_pair(q_ref[:, sl], group, pair, kv_blocks, None, sink_fn)


def _ctx_attention(q, k, v, o_prev, n_heads, group, sinks=None):
    kw = k.shape[1]
    in_specs = [
        pl.BlockSpec((CTX_LEN, D_MODEL), lambda b: (b, 0)),
        pl.BlockSpec((CTX_LEN, kw), lambda b: (b, 0)),
        pl.BlockSpec((CTX_LEN, kw), lambda b: (b, 0)),
        pl.BlockSpec(memory_space=pl.ANY),
    ]
    args = [q, k, v, o_prev]
    if sinks is not None:
        in_specs.insert(0, pl.BlockSpec(memory_space=pltpu.SMEM))
        args.insert(0, sinks)
    return pl.pallas_call(
        functools.partial(_ctx_attn_kernel, n_heads=n_heads, group=group, has_sink=sinks is not None),
        out_shape=jax.ShapeDtypeStruct(o_prev.shape, o_prev.dtype),
        grid=(BATCH,),
        in_specs=in_specs,
        out_specs=pl.BlockSpec((CTX_LEN, D_MODEL), lambda b: (b, 0)),
        input_output_aliases={len(args) - 1: 0},
        compiler_params=_cparams(("arbitrary",)),
        name="ctx_attention",
    )(*args)


def _na_kernel(q_ref, kc_ref, vc_ref, kl_ref, vl_ref, bias_ref, o_ref):
    n_tiles = SEQ // TQ
    for j in range(n_tiles):
        case = 0 if j == 0 else (2 if j == n_tiles - 1 else 1)
        k0 = min(max(NA_Q_ROWS * j - NA_WIN_H // 2, 0), SEQ // GRID_W - NA_K_ROWS) * GRID_W
        rows = slice(j * TQ, (j + 1) * TQ)
        keys = slice(k0, k0 + NA_KEYS)

        def kv_blocks(kp):
            return kc_ref[...], vc_ref[...], kl_ref[keys, :], vl_ref[keys, :]

        o_ref[rows, :] = _head_pair(q_ref[rows, :], 1, 0, kv_blocks,
                                    lambda half: bias_ref[half, case], None)


def _na_attention(q, k, v, bias):
    lat0 = N_CTX_ROWS // SEQ
    return pl.pallas_call(
        _na_kernel,
        out_shape=jax.ShapeDtypeStruct((N_ROWS, D_MODEL), BF16),
        grid=(NA_HEADS // 2, BATCH),
        in_specs=[
            pl.BlockSpec((SEQ, LANES), lambda p, b: (lat0 + b, p)),
            pl.BlockSpec((CTX_LEN, LANES), lambda p, b: (b, p)),
            pl.BlockSpec((CTX_LEN, LANES), lambda p, b: (b, p)),
            pl.BlockSpec((SEQ, LANES), lambda p, b: (lat0 + b, p)),
            pl.BlockSpec((SEQ, LANES), lambda p, b: (lat0 + b, p)),
            pl.BlockSpec((None, 2, 3, TQ, NA_KEYS), lambda p, b: (p, 0, 0, 0, 0)),
        ],
        out_specs=pl.BlockSpec((SEQ, LANES), lambda p, b: (lat0 + b, p)),
        compiler_params=_cparams(("arbitrary", "arbitrary")),
        name="na_attention",
    )(q, k, v, k, v, bias)


def _gqa_kernel(q_ref, kc_ref, vc_ref, kl_ref, vl_ref, o_ref):
    group = GQA_HEADS // GQA_KV_HEADS

    def kv_blocks(kp):
        return kc_ref[...], vc_ref[...], kl_ref[...], vl_ref[...]

    for pair in range(group):
        sl = slice(pair * LANES, (pair + 1) * LANES)
        o_ref[:, sl] = _head_pair(q_ref[:, sl], group, pair, kv_blocks, None, None)


def _gqa_attention(q, k, v):
    lat0 = N_CTX_ROWS // SEQ
    qt0 = N_CTX_ROWS // TQ
    n_qt = SEQ // TQ
    qw = 2 * (GQA_HEADS // GQA_KV_HEADS) * HEAD_DIM
    return pl.pallas_call(
        _gqa_kernel,
        out_shape=jax.ShapeDtypeStruct((N_ROWS, D_MODEL), BF16),
        grid=(BATCH, GQA_KV_HEADS // 2, n_qt),
        in_specs=[
            pl.BlockSpec((TQ, qw), lambda b, p, j: (qt0 + b * n_qt + j, p)),
            pl.BlockSpec((CTX_LEN, LANES), lambda b, p, j: (b, p)),
            pl.BlockSpec((CTX_LEN, LANES), lambda b, p, j: (b, p)),
            pl.BlockSpec((SEQ, LANES), lambda b, p, j: (lat0 + b, p)),
            pl.BlockSpec((SEQ, LANES), lambda b, p, j: (lat0 + b, p)),
        ],
        out_specs=pl.BlockSpec((TQ, qw), lambda b, p, j: (qt0 + b * n_qt + j, p)),
        compiler_params=_cparams(("arbitrary", "arbitrary", "arbitrary")),
        name="gqa_attention",
    )(q, k, v, k, v)


def _swa_kernel(sink_ref, q_ref, kc_ref, vc_ref, kl_ref, vl_ref, o_ref):
    j = pl.program_id(1)
    band = TQ + 2 * SWA_WINDOW
    k0 = pl.multiple_of(jnp.clip(j * TQ - SWA_WINDOW, 0, SEQ - band), LANES)
    qpos = j * TQ + lax.broadcasted_iota(jnp.int32, (TQ, 1), 0)
    kpos = k0 + lax.broadcasted_iota(jnp.int32, (1, band), 1)
    mask = jnp.where(jnp.abs(qpos - kpos) <= SWA_WINDOW, 0.0, MASK_VALUE).astype(F32)
    kl = kl_ref[pl.ds(k0, band), :]
    vl = vl_ref[pl.ds(k0, band), :]

    def kv_blocks(kp):
        return kc_ref[...], vc_ref[...], kl, vl

    for pair in range(SWA_HEADS // 2):
        sl = slice(pair * LANES, (pair + 1) * LANES)
        o_ref[:, sl] = _head_pair(q_ref[:, sl], SWA_HEADS // SWA_KV_HEADS, pair, kv_blocks,
                                  lambda half: mask, lambda h: sink_ref[h])


def _swa_attention(q, k, v, sinks):
    lat0 = N_CTX_ROWS // SEQ
    qt0 = N_CTX_ROWS // TQ
    n_qt = SEQ // TQ
    return pl.pallas_call(
        _swa_kernel,
        out_shape=jax.ShapeDtypeStruct((N_ROWS, D_MODEL), BF16),
        grid=(BATCH, n_qt),
        in_specs=[
            pl.BlockSpec(memory_space=pltpu.SMEM),
            pl.BlockSpec((TQ, D_MODEL), lambda b, j: (qt0 + b * n_qt + j, 0)),
            pl.BlockSpec((CTX_LEN, LANES), lambda b, j: (b, 0)),
            pl.BlockSpec((CTX_LEN, LANES), lambda b, j: (b, 0)),
            pl.BlockSpec((SEQ, LANES), lambda b, j: (lat0 + b, 0)),
            pl.BlockSpec((SEQ, LANES), lambda b, j: (lat0 + b, 0)),
        ],
        out_specs=pl.BlockSpec((TQ, D_MODEL), lambda b, j: (qt0 + b * n_qt + j, 0)),
        compiler_params=_cparams(("arbitrary", "arbitrary")),
        name="swa_attention",
    )(sinks, q, k, v, k, v)


def _lru_kernel(xc_ref, xl_ref, cw_ref, cb_ref, wbd_ref, ba_ref, bx_ref, lam_ref, o_ref, a_s, b_s, h_s):
    chunk = CTX_LEN
    n_lat = SEQ // chunk
    sp = jax.nn.softplus(-lam_ref[...])
    row = lax.broadcasted_iota(jnp.int32, (chunk + 16, 1), 0)

    def conv_chunk(ref, n_chunks, c):
        zeros = jnp.zeros((8, D_MODEL), F32)
        before = ref[c * chunk - 8:c * chunk, :] if c > 0 else zeros
        after = ref[(c + 1) * chunk:(c + 1) * chunk + 8, :] if c < n_chunks - 1 else zeros
        ext = jnp.concatenate([before, ref[c * chunk:(c + 1) * chunk, :], after], axis=0)
        acc = ext * cw_ref[2:3, :]
        acc = acc + pltpu.roll(ext, 2, 0) * cw_ref[0:1, :]
        acc = acc + pltpu.roll(ext, 1, 0) * cw_ref[1:2, :]
        acc = acc + pltpu.roll(ext, chunk + 15, 0) * cw_ref[3:4, :]
        return acc[8:8 + chunk, :] + cb_ref[...]

    def gates(xc, d):
        xb = xc.astype(BF16)
        pre = []
        for kind in range(2):
            cols = [jnp.dot(xb[:, 256 * c:256 * (c + 1)], wbd_ref[d, kind, c], preferred_element_type=F32)
                    for c in range(D_MODEL // 256)]
            pre.append(jnp.concatenate(cols, axis=-1))
        r = jax.nn.sigmoid(pre[0] + ba_ref[d:d + 1, :])
        ig = jax.nn.sigmoid(pre[1] + bx_ref[d:d + 1, :])
        log_a = (-LRU_C * r) * sp[d:d + 1, :]
        a = jnp.exp(log_a)
        a_s[...] = a
        b_s[...] = jnp.sqrt(-jnp.tanh(log_a) * (a * a + 1.0)) * (ig * xc)

    def scan_chunk(h, reverse):
        def body(i, h):
            t = (chunk - 1 - i) if reverse else i
            h = a_s[pl.ds(t, 1), :] * h + b_s[pl.ds(t, 1), :]
            h_s[pl.ds(t, 1), :] = h
            return h
        return lax.fori_loop(0, chunk, body, h, unroll=8)

    for d in range(2):
        h = jnp.zeros((1, D_MODEL), F32)
        gates(conv_chunk(xc_ref, 1, 0), d)
        h = scan_chunk(h, d == 1)
        order = range(n_lat) if d == 0 else range(n_lat - 1, -1, -1)
        for c in order:
            gates(conv_chunk(xl_ref, n_lat, c), d)
            h = scan_chunk(h, d == 1)
            rows = slice(c * chunk, (c + 1) * chunk)
            if d == 0:
                o_ref[rows, :] = h_s[...]
            else:
                o_ref[rows, :] = o_ref[rows, :] + h_s[...]


def _lru_scan(xr, conv_w, conv_b, wbd, b_a, b_x, lam):
    lat0 = N_CTX_ROWS // SEQ
    const2 = lambda b: (0, 0)
    return pl.pallas_call(
        _lru_kernel,
        out_shape=jax.ShapeDtypeStruct((N_LAT_ROWS, D_MODEL), F32),
        grid=(BATCH,),
        in_specs=[
            pl.BlockSpec((CTX_LEN, D_MODEL), lambda b: (b, 0)),
            pl.BlockSpec((SEQ, D_MODEL), lambda b: (lat0 + b, 0)),
            pl.BlockSpec((CONV_WIDTH, D_MODEL), const2),
            pl.BlockSpec((1, D_MODEL), const2),
            pl.BlockSpec(wbd.shape, lambda b: (0, 0, 0, 0, 0)),
            pl.BlockSpec((2, D_MODEL), const2),
            pl.BlockSpec((2, D_MODEL), const2),
            pl.BlockSpec((2, D_MODEL), const2),
        ],
        out_specs=pl.BlockSpec((SEQ, D_MODEL), lambda b: (b, 0)),
        scratch_shapes=[pltpu.VMEM((CTX_LEN, D_MODEL), F32)] * 3,
        compiler_params=_cparams(("arbitrary",)),
        name="rglru_scan",
    )(xr, xr, conv_w, conv_b.reshape(1, D_MODEL), wbd, b_a, b_x, lam)


def _ffn_kernel(*refs, lru, final):
    h_ref, mod_ref, g_ref, wo_ref, win_ref, wout_ref = refs[:6]
    k = 6
    if lru:
        hs_ref, gl_ref = refs[k:k + 2]
        k += 2
        o_mix = (hs_ref[...] * gl_ref[...]).astype(BF16)
    else:
        o_mix = refs[k][...]
        k += 1
    if final:
        gf_ref = refs[k]
        k += 1
    out_ref = refs[k]
    y = jnp.dot(o_mix, wo_ref[...], preferred_element_type=F32)
    h1 = h_ref[...] + mod_ref[2:3, :] * y
    xn = _ada_norm(h1, g_ref[...], mod_ref[3:4, :], mod_ref[4:5, :]).astype(BF16)
    acc = jnp.zeros(h1.shape, F32)
    for c0 in range(0, FFN_HIDDEN, FFN_CHUNK):
        a = jnp.dot(xn, win_ref[:, c0:c0 + FFN_CHUNK], preferred_element_type=F32)
        g = jnp.dot(xn, win_ref[:, FFN_HIDDEN + c0:FFN_HIDDEN + c0 + FFN_CHUNK], preferred_element_type=F32)
        u = ((a * jax.nn.sigmoid(a)) * g).astype(BF16)
        acc = acc + jnp.dot(u, wout_ref[c0:c0 + FFN_CHUNK, :], preferred_element_type=F32)
    h2 = h1 + mod_ref[5:6, :] * acc
    if final:
        ms = jnp.mean(h2 * h2, axis=-1, keepdims=True)
        h2 = (h2 * lax.rsqrt(ms + EPS)) * gf_ref[...]
    out_ref[...] = h2


def _mix_ffn(hx, mod_l, g_ffn, w_o, w_in, w_out, mix_inputs, lru=False, final_g=None):
    final = final_g is not None
    t0 = N_CTX_TILES if final else 0
    n_tiles = N_ROWS // TM - t0
    resident = functools.partial(pl.BlockSpec, pipeline_mode=pl.Buffered(1))
    in_specs = [
        pl.BlockSpec((TM, D_MODEL), lambda i: (i + t0, 0)),
        pl.BlockSpec((None, N_MOD, D_MODEL), lambda i: (_tile_mod_row(i + t0), 0, 0)),
        pl.BlockSpec((1, D_MODEL), lambda i: (0, 0)),
        resident(w_o.shape, lambda i: (0, 0)),
        resident(w_in.shape, lambda i: (0, 0)),
        resident(w_out.shape, lambda i: (0, 0)),
    ]
    args = [hx, mod_l, g_ffn.reshape(1, D_MODEL), w_o, w_in, w_out]
    for m, off in mix_inputs:
        in_specs.append(pl.BlockSpec((TM, D_MODEL), lambda i, off=off: (i + off, 0)))
        args.append(m)
    if final:
        in_specs.append(pl.BlockSpec((1, D_MODEL), lambda i: (0, 0)))
        args.append(final_g.reshape(1, D_MODEL))
    return pl.pallas_call(
        functools.partial(_ffn_kernel, lru=lru, final=final),
        out_shape=jax.ShapeDtypeStruct((n_tiles * TM, D_MODEL), F32),
        grid=(n_tiles,),
        in_specs=in_specs,
        out_specs=pl.BlockSpec((TM, D_MODEL), lambda i: (i, 0)),
        compiler_params=_cparams(("arbitrary",)),
        name="mix_ffn",
    )(*args)


def _rope_tables():
    t = jnp.arange(SEQ, dtype=jnp.int32)
    row = (t // GRID_W).astype(F32)
    col = (t % GRID_W).astype(F32)
    half = HEAD_DIM // 2
    inv_freq = 1.0 / (ROPE_THETA ** (jnp.arange(0, half, 2, dtype=F32) / half))
    d = jnp.arange(LANES) % HEAD_DIM
    freq = inv_freq[(d % half) % (half // 2)]
    pos = jnp.where((d < half)[None, :], row[:, None], col[:, None])
    ang = pos * freq[None, :]
    first = ((d % half) < half // 2)[None, :]
    cos, sin = jnp.cos(ang), jnp.sin(ang)
    ident = jnp.ones((TM, LANES), F32)
    zero = jnp.zeros((TM, LANES), F32)
    return (jnp.concatenate([ident, cos]),
            jnp.concatenate([zero, jnp.where(first, 0.0, sin)]),
            jnp.concatenate([zero, jnp.where(first, -sin, 0.0)]))


def _na_bias_table(rpb):
    rows = SEQ // GRID_W
    qcol = jnp.arange(GRID_W)
    kcol = jnp.arange(GRID_W)
    col_start = jnp.clip(qcol - NA_WIN_W // 2, 0, GRID_W - NA_WIN_W)
    col_in = (kcol[None, :] >= col_start[:, None]) & (kcol[None, :] < col_start[:, None] + NA_WIN_W)
    dcol = jnp.clip(kcol[None, :] - qcol[:, None] + NA_WIN_W - 1, 0, 2 * NA_WIN_W - 2)
    cases = []
    for q0, k0 in ((0, 0), (NA_Q_ROWS, 0), (rows - NA_Q_ROWS, rows - NA_K_ROWS)):
        qrow = q0 + jnp.arange(NA_Q_ROWS)
        krow = k0 + jnp.arange(NA_K_ROWS)
        r0 = jnp.clip(qrow - NA_WIN_H // 2, 0, rows - NA_WIN_H)
        row_in = (krow[None, :] >= r0[:, None]) & (krow[None, :] < r0[:, None] + NA_WIN_H)
        drow = jnp.clip(krow[None, :] - qrow[:, None] + NA_WIN_H - 1, 0, 2 * NA_WIN_H - 2)
        bias = rpb[:, drow[:, None, :, None], dcol[None, :, None, :]]
        valid = row_in[:, None, :, None] & col_in[None, :, None, :]
        cases.append(jnp.where(valid[None], bias, MASK_VALUE).reshape(NA_HEADS, TQ, NA_KEYS))
    tab = jnp.stack(cases, axis=1)
    return tab.reshape(NA_HEADS // 2, 2, 3, TQ, NA_KEYS).astype(F32)


def _block_diag_256(w):
    per = 256 // LRU_BLOCK_DIM
    w4 = w.reshape(LRU_BLOCKS // per, per, LRU_BLOCK_DIM, LRU_BLOCK_DIM)
    eye = jnp.eye(per, dtype=w.dtype)
    return jnp.einsum("cide,ij->cidje", w4, eye).reshape(LRU_BLOCKS // per, 256, 256)


def kernel(x, c, ctx, c_ctx, ada_w, ada_b, norm_mix, norm_ffn, norm_final, ffn_w_in, ffn_w_out, na_w_qkv, na_rpb, na_w_o, gqa_w_qkv, gqa_q_gain, gqa_k_gain, gqa_w_o, swa_w_qkv, swa_sinks, swa_w_o, lru_w_in, lru_conv_w, lru_conv_b, lru_w_a, lru_b_a, lru_w_x, lru_b_x, lru_lam, lru_w_out):
    assert x.shape == (BATCH, SEQ, D_MODEL) and ctx.shape == (BATCH, CTX_LEN, D_MODEL)
    hx = jnp.concatenate([ctx.reshape(N_CTX_ROWS, D_MODEL), x.reshape(N_LAT_ROWS, D_MODEL)], axis=0)
    c_all = jnp.zeros((MOD_ROWS, D_MODEL), F32).at[:BATCH].set(c).at[CTX_MOD_ROW].set(c_ctx)
    mod = _modulation(c_all, ada_w, ada_b).reshape(DEPTH, MOD_ROWS, N_MOD, D_MODEL)
    rope_tabs = _rope_tables()
    scale = HEAD_DIM ** -0.5
    lat_tile0 = N_CTX_TILES

    q, k, v = _project(hx, mod[0], norm_mix[0], na_w_qkv[0].astype(BF16),
                       ((D_MODEL, "plain", None, scale), (D_MODEL, "plain", None, 1.0), (D_MODEL, "plain", None, 1.0)),
                       (BF16, BF16, BF16))
    o = _na_attention(q, k, v, _na_bias_table(na_rpb[0]))
    o = _ctx_attention(q, k, v, o, NA_HEADS, 1)
    hx = _mix_ffn(hx, mod[0], norm_ffn[0], na_w_o[0].astype(BF16), ffn_w_in[0].astype(BF16),
                  ffn_w_out[0].astype(BF16), [(o, 0)])

    kvw = GQA_KV_HEADS * HEAD_DIM
    gains = jnp.stack([jnp.tile(gqa_q_gain[0], 2), jnp.tile(gqa_k_gain[0], 2)])
    q, k, v = _project(hx, mod[1], norm_mix[1], gqa_w_qkv[0].astype(BF16),
                       ((D_MODEL, "rope", 0, scale), (kvw, "rope", 1, 1.0), (kvw, "plain", None, 1.0)),
                       (BF16, BF16, BF16), rope_tabs, gains)
    o = _gqa_attention(q, k, v)
    o = _ctx_attention(q, k, v, o, GQA_HEADS, GQA_HEADS // GQA_KV_HEADS)
    hx = _mix_ffn(hx, mod[1], norm_ffn[1], gqa_w_o[0].astype(BF16), ffn_w_in[1].astype(BF16),
                  ffn_w_out[1].astype(BF16), [(o, 0)])

    kvw = SWA_KV_HEADS * HEAD_DIM
    q, k, v = _project(hx, mod[2], norm_mix[2], swa_w_qkv[0].astype(BF16),
                       ((D_MODEL, "rope", None, scale), (kvw, "rope", None, 1.0), (kvw, "plain", None, 1.0)),
                       (BF16, BF16, BF16), rope_tabs, jnp.ones((2, LANES), F32))
    o = _swa_attention(q, k, v, swa_sinks[0])
    o = _ctx_attention(q, k, v, o, SWA_HEADS, SWA_HEADS // SWA_KV_HEADS, swa_sinks[0])
    hx = _mix_ffn(hx, mod[2], norm_ffn[2], swa_w_o[0].astype(BF16), ffn_w_in[2].astype(BF16),
                  ffn_w_out[2].astype(BF16), [(o, 0)])

    xr, gl = _project(hx, mod[3], norm_mix[3], lru_w_in[0].astype(BF16),
                      ((D_MODEL, "plain", None, 1.0), (D_MODEL, "gelu", None, 1.0)), (F32, F32))
    wbd = jnp.stack([jnp.stack([_block_diag_256(lru_w_a[0, d]), _block_diag_256(lru_w_x[0, d])])
                     for d in range(2)]).astype(BF16)
    hs = _lru_scan(xr, lru_conv_w[0], lru_conv_b[0], wbd, lru_b_a[0], lru_b_x[0], lru_lam[0])
    out = _mix_ffn(hx, mod[3], norm_ffn[3], lru_w_out[0].astype(BF16), ffn_w_in[3].astype(BF16),
                   ffn_w_out[3].astype(BF16), [(hs, 0), (gl, lat_tile0)], lru=True, final_g=norm_final)
    return out.reshape(BATCH, SEQ, D_MODEL)
```

```python
import functools

import jax
import jax.numpy as jnp
import numpy as np
from jax import lax
from jax.experimental import pallas as pl
from jax.experimental.pallas import tpu as pltpu

F32 = jnp.float32
BF16 = jnp.bfloat16

D_MODEL = 1024
BATCH = 8
SEQ = 2048
DEPTH = 4
GRID_W = 64
CTX_LEN = 256
HEAD_DIM = 64
ROPE_THETA = 10000.0
NA_HEADS = 16
NA_WIN_H = 8
NA_WIN_W = 16
GQA_HEADS = 16
GQA_KV_HEADS = 4
SWA_HEADS = 16
SWA_KV_HEADS = 2
SWA_WINDOW = 128
LRU_BLOCKS = 16
LRU_BLOCK_DIM = D_MODEL // LRU_BLOCKS
CONV_WIDTH = 4
LRU_C = 8.0
FFN_HIDDEN = 2816
N_MOD = 6
EPS = 1e-6
MASK_VALUE = -1e30

N_CTX_ROWS = BATCH * CTX_LEN
N_LAT_ROWS = BATCH * SEQ
N_ROWS = N_CTX_ROWS + N_LAT_ROWS
MOD_ROWS = 16
CTX_MOD_ROW = BATCH

LANES = 128
TM = 512
TQ = 256
FFN_CHUNK = 256
VMEM_LIMIT = 56 * 1024 * 1024

N_CTX_TILES = N_CTX_ROWS // TM
TILES_PER_BATCH = SEQ // TM
NA_Q_ROWS = TQ // GRID_W
NA_K_ROWS = 12
NA_KEYS = NA_K_ROWS * GRID_W

_NT = (((1,), (1,)), ((), ()))


def _cparams(sem):
    return pltpu.CompilerParams(dimension_semantics=sem, vmem_limit_bytes=VMEM_LIMIT)


def _tile_mod_row(i):
    return jnp.where(i < N_CTX_TILES, CTX_MOD_ROW, (i - N_CTX_TILES) // TILES_PER_BATCH)


def _tile_pos_block(i):
    return jnp.where(i < N_CTX_TILES, 0, 1 + (i - N_CTX_TILES) % TILES_PER_BATCH)


def _ada_norm(x, g, shift, scale):
    ms = jnp.mean(x * x, axis=-1, keepdims=True)
    y = x * lax.rsqrt(ms + EPS)
    return (y * g) * (1.0 + scale) + shift


def _mod_kernel(c_ref, w_ref, b_ref, o_ref):
    c = c_ref[...]
    sc = (c * jax.nn.sigmoid(c)).astype(BF16)
    o_ref[0] = jnp.dot(sc, w_ref[0].astype(BF16), preferred_element_type=F32) + b_ref[0]


def _modulation(c_all, ada_w, ada_b):
    tn = 1536
    n = N_MOD * D_MODEL
    return pl.pallas_call(
        _mod_kernel,
        out_shape=jax.ShapeDtypeStruct((DEPTH, MOD_ROWS, n), F32),
        grid=(DEPTH, n // tn),
        in_specs=[
            pl.BlockSpec((MOD_ROWS, D_MODEL), lambda l, j: (0, 0)),
            pl.BlockSpec((1, D_MODEL, tn), lambda l, j: (l, 0, j)),
            pl.BlockSpec((1, 1, tn), lambda l, j: (l, 0, j)),
        ],
        out_specs=pl.BlockSpec((1, MOD_ROWS, tn), lambda l, j: (l, 0, j)),
        compiler_params=_cparams(("arbitrary", "arbitrary")),
        name="modulation",
    )(c_all, ada_w, ada_b.reshape(DEPTH, 1, n))


def _proj_kernel(*refs, segs, has_rope):
    x_ref, mod_ref, g_ref, w_ref = refs[:4]
    k = 4
    if has_rope:
        cos_ref, s1_ref, s2_ref, gain_ref = refs[4:8]
        k = 8
    out_refs = refs[k:]
    xn = _ada_norm(x_ref[...], g_ref[...], mod_ref[0:1, :], mod_ref[1:2, :]).astype(BF16)
    lane = lax.broadcasted_iota(jnp.int32, (1, LANES), 1)
    lo = lane < HEAD_DIM
    col = 0
    for seg, o_ref in zip(segs, out_refs):
        width, kind, norm_idx, scale = seg
        chunk = 256
        for c0 in range(0, width, chunk):
            cw = min(chunk, width - c0)
            y = jnp.dot(xn, w_ref[:, col + c0:col + c0 + cw], preferred_element_type=F32)
            if kind == "gelu":
                y = jax.nn.gelu(y, approximate=True)
            elif kind == "rope":
                parts = []
                for b0 in range(0, cw, LANES):
                    blk = y[:, b0:b0 + LANES]
                    if norm_idx is not None:
                        ss = blk * blk
                        s_lo = jnp.sum(jnp.where(lo, ss, 0.0), axis=-1, keepdims=True)
                        s_hi = jnp.sum(jnp.where(lo, 0.0, ss), axis=-1, keepdims=True)
                        ms = jnp.where(lo, s_lo, s_hi) * (1.0 / HEAD_DIM)
                        blk = (blk * lax.rsqrt(ms + EPS)) * gain_ref[norm_idx:norm_idx + 1, :]
                    blk = (blk * cos_ref[...]
                           + pltpu.roll(blk, 16, 1) * s1_ref[...]
                           + pltpu.roll(blk, LANES - 16, 1) * s2_ref[...])
                    parts.append(blk)
                y = parts[0] if len(parts) == 1 else jnp.concatenate(parts, axis=-1)
            if scale != 1.0:
                y = y * scale
            o_ref[:, c0:c0 + cw] = y.astype(o_ref.dtype)
        col += width


def _project(hx, mod_l, g, w, segs, out_dtypes, rope_tabs=None, gains=None):
    n_tot = sum(s[0] for s in segs)
    in_specs = [
        pl.BlockSpec((TM, D_MODEL), lambda i: (i, 0)),
        pl.BlockSpec((None, N_MOD, D_MODEL), lambda i: (_tile_mod_row(i), 0, 0)),
        pl.BlockSpec((1, D_MODEL), lambda i: (0, 0)),
        pl.BlockSpec((D_MODEL, n_tot), lambda i: (0, 0), pipeline_mode=pl.Buffered(1)),
    ]
    args = [hx, mod_l, g.reshape(1, D_MODEL), w]
    has_rope = rope_tabs is not None
    if has_rope:
        for t in rope_tabs:
            in_specs.append(pl.BlockSpec((TM, LANES), lambda i: (_tile_pos_block(i), 0)))
            args.append(t)
        in_specs.append(pl.BlockSpec(gains.shape, lambda i: (0, 0)))
        args.append(gains)
    out_shape = [jax.ShapeDtypeStruct((N_ROWS, s[0]), dt) for s, dt in zip(segs, out_dtypes)]
    out_specs = [pl.BlockSpec((TM, s[0]), lambda i: (i, 0)) for s in segs]
    return pl.pallas_call(
        functools.partial(_proj_kernel, segs=segs, has_rope=has_rope),
        out_shape=out_shape,
        grid=(N_ROWS // TM,),
        in_specs=in_specs,
        out_specs=out_specs,
        compiler_params=_cparams(("arbitrary",)),
        name="ada_proj",
    )(*args)


def _softmax_pv(q, k_ctx, v_ctx, k_lat, v_lat, bias, sink):
    s_c = lax.dot_general(q, k_ctx, _NT, preferred_element_type=F32)
    m = jnp.max(s_c, axis=-1, keepdims=True)
    if k_lat is not None:
        s_l = lax.dot_general(q, k_lat, _NT, preferred_element_type=F32)
        if bias is not None:
            s_l = s_l + bias
        m = jnp.maximum(m, jnp.max(s_l, axis=-1, keepdims=True))
    if sink is not None:
        m = jnp.maximum(m, sink)
    e_c = jnp.exp(s_c - m)
    den = jnp.sum(e_c, axis=-1, keepdims=True)
    o = jnp.dot(e_c.astype(BF16), v_ctx, preferred_element_type=F32)
    if k_lat is not None:
        e_l = jnp.exp(s_l - m)
        den = den + jnp.sum(e_l, axis=-1, keepdims=True)
        o = o + jnp.dot(e_l.astype(BF16), v_lat, preferred_element_type=F32)
    if sink is not None:
        den = den + jnp.exp(sink - m)
    return o / den


def _head_pair(q128, group, pair, kv_blocks, bias_fn, sink_fn):
    lane = lax.broadcasted_iota(jnp.int32, (1, LANES), 1)
    lo = lane < HEAD_DIM
    outs = []
    for half in range(2):
        head = 2 * pair + half
        kv_head = head // group
        kv_half = kv_head % 2
        q = q128 if kv_half == half else pltpu.roll(q128, HEAD_DIM, 1)
        q = jnp.where(lo if kv_half == 0 else jnp.logical_not(lo), q, jnp.zeros_like(q))
        k_ctx, v_ctx, k_lat, v_lat = kv_blocks(kv_head // 2)
        o = _softmax_pv(q, k_ctx, v_ctx, k_lat, v_lat,
                        None if bias_fn is None else bias_fn(half),
                        None if sink_fn is None else sink_fn(head))
        if kv_half != half:
            o = pltpu.roll(o, HEAD_DIM, 1)
        outs.append(o)
    return jnp.where(lo, outs[0], outs[1]).astype(BF16)


def _ctx_attn_kernel(*refs, n_heads, group, has_sink):
    if has_sink:
        sink_ref, refs = refs[0], refs[1:]
    q_ref, k_ref, v_ref, _, o_ref = refs

    def kv_blocks(kp):
        sl = slice(kp * LANES, (kp + 1) * LANES)
        return k_ref[:, sl], v_ref[:, sl], None, None

    sink_fn = (lambda h: sink_ref[h]) if has_sink else None
    for pair in range(n_heads // 2):
        sl = slice(pair * LANES, (pair + 1) * LANES)
        o_ref[:, sl] = _head_pair(q_ref[:, sl], group, pair, kv_blocks, None, sink_fn)


def _ctx_attention(q, k, v, o_prev, n_heads, group, sinks=None):
    kw = k.shape[1]
    in_specs = [
        pl.BlockSpec((CTX_LEN, D_MODEL), lambda b: (b, 0)),
        pl.BlockSpec((CTX_LEN, kw), lambda b: (b, 0)),
        pl.BlockSpec((CTX_LEN, kw), lambda b: (b, 0)),
        pl.BlockSpec(memory_space=pl.ANY),
    ]
    args = [q, k, v, o_prev]
    if sinks is not None:
        in_specs.insert(0, pl.BlockSpec(memory_space=pltpu.SMEM))
        args.insert(0, sinks)
    return pl.pallas_call(
        functools.partial(_ctx_attn_kernel, n_heads=n_heads, group=group, has_sink=sinks is not None),
        out_shape=jax.ShapeDtypeStruct(o_prev.shape, o_prev.dtype),
        grid=(BATCH,),
        in_specs=in_specs,
        out_specs=pl.BlockSpec((CTX_LEN, D_MODEL), lambda b: (b, 0)),
        input_output_aliases={len(args) - 1: 0},
        compiler_params=_cparams(("arbitrary",)),
        name="ctx_attention",
    )(*args)


def _na_kernel(q_ref, kc_ref, vc_ref, kl_ref, vl_ref, bias_ref, o_ref):
    n_tiles = SEQ // TQ
    for j in range(n_tiles):
        case = 0 if j == 0 else (2 if j == n_tiles - 1 else 1)
        k0 = min(max(NA_Q_ROWS * j - NA_WIN_H // 2, 0), SEQ // GRID_W - NA_K_ROWS) * GRID_W
        rows = slice(j * TQ, (j + 1) * TQ)
        keys = slice(k0, k0 + NA_KEYS)

        def kv_blocks(kp):
            return kc_ref[...], vc_ref[...], kl_ref[keys, :], vl_ref[keys, :]

        o_ref[rows, :] = _head_pair(q_ref[rows, :], 1, 0, kv_blocks,
                                    lambda half: bias_ref[half, case], None)


def _na_attention(q, k, v, bias):
    lat0 = N_CTX_ROWS // SEQ
    return pl.pallas_call(
        _na_kernel,
        out_shape=jax.ShapeDtypeStruct((N_ROWS, D_MODEL), BF16),
        grid=(NA_HEADS // 2, BATCH),
        in_specs=[
            pl.BlockSpec((SEQ, LANES), lambda p, b: (lat0 + b, p)),
            pl.BlockSpec((CTX_LEN, LANES), lambda p, b: (b, p)),
            pl.BlockSpec((CTX_LEN, LANES), lambda p, b: (b, p)),
            pl.BlockSpec((SEQ, LANES), lambda p, b: (lat0 + b, p)),
            pl.BlockSpec((SEQ, LANES), lambda p, b: (lat0 + b, p)),
            pl.BlockSpec((None, 2, 3, TQ, NA_KEYS), lambda p, b: (p, 0, 0, 0, 0)),
        ],
        out_specs=pl.BlockSpec((SEQ, LANES), lambda p, b: (lat0 + b, p)),
        compiler_params=_cparams(("arbitrary", "arbitrary")),
        name="na_attention",
    )(q, k, v, k, v, bias)


def _gqa_kernel(q_ref, kc_ref, vc_ref, kl_ref, vl_ref, o_ref):
    group = GQA_HEADS // GQA_KV_HEADS

    def kv_blocks(kp):
        return kc_ref[...], vc_ref[...], kl_ref[...], vl_ref[...]

    for pair in range(group):
        sl = slice(pair * LANES, (pair + 1) * LANES)
        o_ref[:, sl] = _head_pair(q_ref[:, sl], group, pair, kv_blocks, None, None)


def _gqa_attention(q, k, v):
    lat0 = N_CTX_ROWS // SEQ
    qt0 = N_CTX_ROWS // TQ
    n_qt = SEQ // TQ
    qw = 2 * (GQA_HEADS // GQA_KV_HEADS) * HEAD_DIM
    return pl.pallas_call(
        _gqa_kernel,
        out_shape=jax.ShapeDtypeStruct((N_ROWS, D_MODEL), BF16),
        grid=(BATCH, GQA_KV_HEADS // 2, n_qt),
        in_specs=[
            pl.BlockSpec((TQ, qw), lambda b, p, j: (qt0 + b * n_qt + j, p)),
            pl.BlockSpec((CTX_LEN, LANES), lambda b, p, j: (b, p)),
            pl.BlockSpec((CTX_LEN, LANES), lambda b, p, j: (b, p)),
            pl.BlockSpec((SEQ, LANES), lambda b, p, j: (lat0 + b, p)),
            pl.BlockSpec((SEQ, LANES), lambda b, p, j: (lat0 + b, p)),
        ],
        out_specs=pl.BlockSpec((TQ, qw), lambda b, p, j: (qt0 + b * n_qt + j, p)),
        compiler_params=_cparams(("arbitrary", "arbitrary", "arbitrary")),
        name="gqa_attention",
    )(q, k, v, k, v)


def _swa_kernel(sink_ref, q_ref, kc_ref, vc_ref, kl_ref, vl_ref, o_ref):
    j = pl.program_id(1)
    band = TQ + 2 * SWA_WINDOW
    k0 = pl.multiple_of(jnp.clip(j * TQ - SWA_WINDOW, 0, SEQ - band), LANES)
    qpos = j * TQ + lax.broadcasted_iota(jnp.int32, (TQ, 1), 0)
    kpos = k0 + lax.broadcasted_iota(jnp.int32, (1, band), 1)
    mask = jnp.where(jnp.abs(qpos - kpos) <= SWA_WINDOW, 0.0, MASK_VALUE).astype(F32)
    kl = kl_ref[pl.ds(k0, band), :]
    vl = vl_ref[pl.ds(k0, band), :]

    def kv_blocks(kp):
        return kc_ref[...], vc_ref[...], kl, vl

    for pair in range(SWA_HEADS // 2):
        sl = slice(pair * LANES, (pair + 1) * LANES)
        o_ref[:, sl] = _head_pair(q_ref[:, sl], SWA_HEADS // SWA_KV_HEADS, pair, kv_blocks,
                                  lambda half: mask, lambda h: sink_ref[h])


def _swa_attention(q, k, v, sinks):
    lat0 = N_CTX_ROWS // SEQ
    qt0 = N_CTX_ROWS // TQ
    n_qt = SEQ // TQ
    return pl.pallas_call(
        _swa_kernel,
        out_shape=jax.ShapeDtypeStruct((N_ROWS, D_MODEL), BF16),
        grid=(BATCH, n_qt),
        in_specs=[
            pl.BlockSpec(memory_space=pltpu.SMEM),
            pl.BlockSpec((TQ, D_MODEL), lambda b, j: (qt0 + b * n_qt + j, 0)),
            pl.BlockSpec((CTX_LEN, LANES), lambda b, j: (b, 0)),
            pl.BlockSpec((CTX_LEN, LANES), lambda b, j: (b, 0)),
            pl.BlockSpec((SEQ, LANES), lambda b, j: (lat0 + b, 0)),
            pl.BlockSpec((SEQ, LANES), lambda b, j: (lat0 + b, 0)),
        ],
        out_specs=pl.BlockSpec((TQ, D_MODEL), lambda b, j: (qt0 + b * n_qt + j, 0)),
        compiler_params=_cparams(("arbitrary", "arbitrary")),
        name="swa_attention",
    )(sinks, q, k, v, k, v)


def _lru_kernel(xc_ref, xl_ref, cw_ref, cb_ref, wbd_ref, ba_ref, bx_ref, lam_ref, o_ref, a_s, b_s, h_s):
    chunk = CTX_LEN
    n_lat = SEQ // chunk
    sp = jax.nn.softplus(-lam_ref[...])
    row = lax.broadcasted_iota(jnp.int32, (chunk + 16, 1), 0)

    def conv_chunk(ref, n_chunks, c):
        zeros = jnp.zeros((8, D_MODEL), F32)
        before = ref[c * chunk - 8:c * chunk, :] if c > 0 else zeros
        after = ref[(c + 1) * chunk:(c + 1) * chunk + 8, :] if c < n_chunks - 1 else zeros
        ext = jnp.concatenate([before, ref[c * chunk:(c + 1) * chunk, :], after], axis=0)
        acc = ext * cw_ref[2:3, :]
        acc = acc + pltpu.roll(ext, 2, 0) * cw_ref[0:1, :]
        acc = acc + pltpu.roll(ext, 1, 0) * cw_ref[1:2, :]
        acc = acc + pltpu.roll(ext, chunk + 15, 0) * cw_ref[3:4, :]
        return acc[8:8 + chunk, :] + cb_ref[...]

    def gates(xc, d):
        xb = xc.astype(BF16)
        pre = []
        for kind in range(2):
            cols = [jnp.dot(xb[:, 256 * c:256 * (c + 1)], wbd_ref[d, kind, c], preferred_element_type=F32)
                    for c in range(D_MODEL // 256)]
            pre.append(jnp.concatenate(cols, axis=-1))
        r = jax.nn.sigmoid(pre[0] + ba_ref[d:d + 1, :])
        ig = jax.nn.sigmoid(pre[1] + bx_ref[d:d + 1, :])
        log_a = (-LRU_C * r) * sp[d:d + 1, :]
        a = jnp.exp(log_a)
        a_s[...] = a
        b_s[...] = jnp.sqrt(-jnp.tanh(log_a) * (a * a + 1.0)) * (ig * xc)

    def scan_chunk(h, reverse):
        def body(i, h):
            t = (chunk - 1 - i) if reverse else i
            h = a_s[pl.ds(t, 1), :] * h + b_s[pl.ds(t, 1), :]
            h_s[pl.ds(t, 1), :] = h
            return h
        return lax.fori_loop(0, chunk, body, h, unroll=8)

    for d in range(2):
        h = jnp.zeros((1, D_MODEL), F32)
        gates(conv_chunk(xc_ref, 1, 0), d)
        h = scan_chunk(h, d == 1)
        order = range(n_lat) if d == 0 else range(n_lat - 1, -1, -1)
        for c in order:
            gates(conv_chunk(xl_ref, n_lat, c), d)
            h = scan_chunk(h, d == 1)
            rows = slice(c * chunk, (c + 1) * chunk)
            if d == 0:
                o_ref[rows, :] = h_s[...]
            else:
                o_ref[rows, :] = o_ref[rows, :] + h_s[...]


def _lru_scan(xr, conv_w, conv_b, wbd, b_a, b_x, lam):
    lat0 = N_CTX_ROWS // SEQ
    const2 = lambda b: (0, 0)
    return pl.pallas_call(
        _lru_kernel,
        out_shape=jax.ShapeDtypeStruct((N_LAT_ROWS, D_MODEL), F32),
        grid=(BATCH,),
        in_specs=[
            pl.BlockSpec((CTX_LEN, D_MODEL), lambda b: (b, 0)),
            pl.BlockSpec((SEQ, D_MODEL), lambda b: (lat0 + b, 0)),
            pl.BlockSpec((CONV_WIDTH, D_MODEL), const2),
            pl.BlockSpec((1, D_MODEL), const2),
            pl.BlockSpec(wbd.shape, lambda b: (0, 0, 0, 0, 0)),
            pl.BlockSpec((2, D_MODEL), const2),
            pl.BlockSpec((2, D_MODEL), const2),
            pl.BlockSpec((2, D_MODEL), const2),
        ],
        out_specs=pl.BlockSpec((SEQ, D_MODEL), lambda b: (b, 0)),
        scratch_shapes=[pltpu.VMEM((CTX_LEN, D_MODEL), F32)] * 3,
        compiler_params=_cparams(("arbitrary",)),
        name="rglru_scan",
    )(xr, xr, conv_w, conv_b.reshape(1, D_MODEL), wbd, b_a, b_x, lam)


def _ffn_kernel(*refs, lru, final):
    h_ref, mod_ref, g_ref, wo_ref, win_ref, wout_ref = refs[:6]
    k = 6
    if lru:
        hs_ref, gl_ref = refs[k:k + 2]
        k += 2
        o_mix = (hs_ref[...] * gl_ref[...]).astype(BF16)
    else:
        o_mix = refs[k][...]
        k += 1
    if final:
        gf_ref = refs[k]
        k += 1
    out_ref = refs[k]
    y = jnp.dot(o_mix, wo_ref[...], preferred_element_type=F32)
    h1 = h_ref[...] + mod_ref[2:3, :] * y
    xn = _ada_norm(h1, g_ref[...], mod_ref[3:4, :], mod_ref[4:5, :]).astype(BF16)
    acc = jnp.zeros(h1.shape, F32)
    for c0 in range(0, FFN_HIDDEN, FFN_CHUNK):
        a = jnp.dot(xn, win_ref[:, c0:c0 + FFN_CHUNK], preferred_element_type=F32)
        g = jnp.dot(xn, win_ref[:, FFN_HIDDEN + c0:FFN_HIDDEN + c0 + FFN_CHUNK], preferred_element_type=F32)
        u = ((a * jax.nn.sigmoid(a)) * g).astype(BF16)
        acc = acc + jnp.dot(u, wout_ref[c0:c0 + FFN_CHUNK, :], preferred_element_type=F32)
    h2 = h1 + mod_ref[5:6, :] * acc
    if final:
        ms = jnp.mean(h2 * h2, axis=-1, keepdims=True)
        h2 = (h2 * lax.rsqrt(ms + EPS)) * gf_ref[...]
    out_ref[...] = h2


def _mix_ffn(hx, mod_l, g_ffn, w_o, w_in, w_out, mix_inputs, lru=False, final_g=None):
    final = final_g is not None
    t0 = N_CTX_TILES if final else 0
    n_tiles = N_ROWS // TM - t0
    resident = functools.partial(pl.BlockSpec, pipeline_mode=pl.Buffered(1))
    in_specs = [
        pl.BlockSpec((TM, D_MODEL), lambda i: (i + t0, 0)),
        pl.BlockSpec((None, N_MOD, D_MODEL), lambda i: (_tile_mod_row(i + t0), 0, 0)),
        pl.BlockSpec((1, D_MODEL), lambda i: (0, 0)),
        resident(w_o.shape, lambda i: (0, 0)),
        resident(w_in.shape, lambda i: (0, 0)),
        resident(w_out.shape, lambda i: (0, 0)),
    ]
    args = [hx, mod_l, g_ffn.reshape(1, D_MODEL), w_o, w_in, w_out]
    for m, off in mix_inputs:
        in_specs.append(pl.BlockSpec((TM, D_MODEL), lambda i, off=off: (i + off, 0)))
        args.append(m)
    if final:
        in_specs.append(pl.BlockSpec((1, D_MODEL), lambda i: (0, 0)))
        args.append(final_g.reshape(1, D_MODEL))
    return pl.pallas_call(
        functools.partial(_ffn_kernel, lru=lru, final=final),
        out_shape=jax.ShapeDtypeStruct((n_tiles * TM, D_MODEL), F32),
        grid=(n_tiles,),
        in_specs=in_specs,
        out_specs=pl.BlockSpec((TM, D_MODEL), lambda i: (i, 0)),
        compiler_params=_cparams(("arbitrary",)),
        name="mix_ffn",
    )(*args)


def _rope_tables():
    t = jnp.arange(SEQ, dtype=jnp.int32)
    row = (t // GRID_W).astype(F32)
    col = (t % GRID_W).astype(F32)
    half = HEAD_DIM // 2
    inv_freq = 1.0 / (ROPE_THETA ** (jnp.arange(0, half, 2, dtype=F32) / half))
    d = jnp.arange(LANES) % HEAD_DIM
    freq = inv_freq[(d % half) % (half // 2)]
    pos = jnp.where((d < half)[None, :], row[:, None], col[:, None])
    ang = pos * freq[None, :]
    first = ((d % half) < half // 2)[None, :]
    cos, sin = jnp.cos(ang), jnp.sin(ang)
    ident = jnp.ones((TM, LANES), F32)
    zero = jnp.zeros((TM, LANES), F32)
    return (jnp.concatenate([ident, cos]),
            jnp.concatenate([zero, jnp.where(first, 0.0, sin)]),
            jnp.concatenate([zero, jnp.where(first, -sin, 0.0)]))


def _na_bias_table(rpb):
    rows = SEQ // GRID_W
    n_drow = 2 * NA_WIN_H - 1
    n_dcol = 2 * NA_WIN_W - 1
    span = 2 * GRID_W
    left = GRID_W - NA_WIN_W
    w = jnp.pad(rpb.astype(F32), ((0, 0), (0, 0), (left, span - left - n_dcol)))
    flat = jnp.tile(w, (1, 1, GRID_W))[..., GRID_W - 1:GRID_W - 1 + GRID_W * (span - 1)]
    toep = flat.reshape(NA_HEADS, n_drow, GRID_W, span - 1)[..., :GRID_W]
    qcol = np.arange(GRID_W)
    col_start = np.clip(qcol - NA_WIN_W // 2, 0, GRID_W - NA_WIN_W)
    col_in = (qcol[None, :] >= col_start[:, None]) & (qcol[None, :] < col_start[:, None] + NA_WIN_W)
    blocks = jnp.where(col_in[None, None], toep, MASK_VALUE)
    blocks = jnp.concatenate([blocks, jnp.full((NA_HEADS, 1, GRID_W, GRID_W), MASK_VALUE, F32)], axis=1)
    idx = []
    for q0, k0 in ((0, 0), (NA_Q_ROWS, 0), (rows - NA_Q_ROWS, rows - NA_K_ROWS)):
        qrow = q0 + np.arange(NA_Q_ROWS)
        krow = k0 + np.arange(NA_K_ROWS)
        r0 = np.clip(qrow - NA_WIN_H // 2, 0, rows - NA_WIN_H)
        row_in = (krow[None, :] >= r0[:, None]) & (krow[None, :] < r0[:, None] + NA_WIN_H)
        idx.append(np.where(row_in, krow[None, :] - qrow[:, None] + NA_WIN_H - 1, n_drow))
    idx = np.stack(idx).astype(np.int32)
    tab = jnp.take(blocks, jnp.asarray(idx.reshape(-1)), axis=1)
    tab = tab.reshape(NA_HEADS // 2, 2, 3, NA_Q_ROWS, NA_K_ROWS, GRID_W, GRID_W)
    return tab.transpose(0, 1, 2, 3, 5, 4, 6).reshape(NA_HEADS // 2, 2, 3, TQ, NA_KEYS)


def _block_diag_256(w):
    per = 256 // LRU_BLOCK_DIM
    w4 = w.reshape(LRU_BLOCKS // per, per, LRU_BLOCK_DIM, LRU_BLOCK_DIM)
    eye = jnp.eye(per, dtype=w.dtype)
    return jnp.einsum("cide,ij->cidje", w4, eye).reshape(LRU_BLOCKS // per, 256, 256)


def kernel(x, c, ctx, c_ctx, ada_w, ada_b, norm_mix, norm_ffn, norm_final, ffn_w_in, ffn_w_out, na_w_qkv, na_rpb, na_w_o, gqa_w_qkv, gqa_q_gain, gqa_k_gain, gqa_w_o, swa_w_qkv, swa_sinks, swa_w_o, lru_w_in, lru_conv_w, lru_conv_b, lru_w_a, lru_b_a, lru_w_x, lru_b_x, lru_lam, lru_w_out):
    assert x.shape == (BATCH, SEQ, D_MODEL) and ctx.shape == (BATCH, CTX_LEN, D_MODEL)
    hx = jnp.concatenate([ctx.reshape(N_CTX_ROWS, D_MODEL), x.reshape(N_LAT_ROWS, D_MODEL)], axis=0)
    c_all = jnp.zeros((MOD_ROWS, D_MODEL), F32).at[:BATCH].set(c).at[CTX_MOD_ROW].set(c_ctx)
    mod = _modulation(c_all, ada_w, ada_b).reshape(DEPTH, MOD_ROWS, N_MOD, D_MODEL)
    rope_tabs = _rope_tables()
    scale = HEAD_DIM ** -0.5
    lat_tile0 = N_CTX_TILES

    q, k, v = _project(hx, mod[0], norm_mix[0], na_w_qkv[0].astype(BF16),
                       ((D_MODEL, "plain", None, scale), (D_MODEL, "plain", None, 1.0), (D_MODEL, "plain", None, 1.0)),
                       (BF16, BF16, BF16))
    o = _na_attention(q, k, v, _na_bias_table(na_rpb[0]))
    o = _ctx_attention(q, k, v, o, NA_HEADS, 1)
    hx = _mix_ffn(hx, mod[0], norm_ffn[0], na_w_o[0].astype(BF16), ffn_w_in[0].astype(BF16),
                  ffn_w_out[0].astype(BF16), [(o, 0)])

    kvw = GQA_KV_HEADS * HEAD_DIM
    gains = jnp.stack([jnp.tile(gqa_q_gain[0], 2), jnp.tile(gqa_k_gain[0], 2)])
    q, k, v = _project(hx, mod[1], norm_mix[1], gqa_w_qkv[0].astype(BF16),
                       ((D_MODEL, "rope", 0, scale), (kvw, "rope", 1, 1.0), (kvw, "plain", None, 1.0)),
                       (BF16, BF16, BF16), rope_tabs, gains)
    o = _gqa_attention(q, k, v)
    o = _ctx_attention(q, k, v, o, GQA_HEADS, GQA_HEADS // GQA_KV_HEADS)
    hx = _mix_ffn(hx, mod[1], norm_ffn[1], gqa_w_o[0].astype(BF16), ffn_w_in[1].astype(BF16),
                  ffn_w_out[1].astype(BF16), [(o, 0)])

    kvw = SWA_KV_HEADS * HEAD_DIM
    q, k, v = _project(hx, mod[2], norm_mix[2], swa_w_qkv[0].astype(BF16),
                       ((D_MODEL, "rope", None, scale), (kvw, "rope", None, 1.0), (kvw, "plain", None, 1.0)),
                       (BF16, BF16, BF16), rope_tabs, jnp.ones((2, LANES), F32))
    o = _swa_attention(q, k, v, swa_sinks[0])
    o = _ctx_attention(q, k, v, o, SWA_HEADS, SWA_HEADS // SWA_KV_HEADS, swa_sinks[0])
    hx = _mix_ffn(hx, mod[2], norm_ffn[2], swa_w_o[0].astype(BF16), ffn_w_in[2].astype(BF16),
                  ffn_w_out[2].astype(BF16), [(o, 0)])

    xr, gl = _project(hx, mod[3], norm_mix[3], lru_w_in[0].astype(BF16),
                      ((D_MODEL, "plain", None, 1.0), (D_MODEL, "gelu", None, 1.0)), (F32, F32))
    wbd = jnp.stack([jnp.stack([_block_diag_256(lru_w_a[0, d]), _block_diag_256(lru_w_x[0, d])])
                     for d in range(2)]).astype(BF16)
    hs = _lru_scan(xr, lru_conv_w[0], lru_conv_b[0], wbd, lru_b_a[0], lru_b_x[0], lru_lam[0])
    out = _mix_ffn(hx, mod[3], norm_ffn[3], lru_w_out[0].astype(BF16), ffn_w_in[3].astype(BF16),
                   ffn_w_out[3].astype(BF16), [(hs, 0), (gl, lat_tile0)], lru=True, final_g=norm_final)
    return out.reshape(BATCH, SEQ, D_MODEL)
```

```python
import functools

import jax
import jax.numpy as jnp
import numpy as np
from jax import lax
from jax.experimental import pallas as pl
from jax.experimental.pallas import tpu as pltpu

F32 = jnp.float32
BF16 = jnp.bfloat16

D_MODEL = 1024
BATCH = 8
SEQ = 2048
DEPTH = 4
GRID_W = 64
CTX_LEN = 256
HEAD_DIM = 64
ROPE_THETA = 10000.0
NA_HEADS = 16
NA_WIN_H = 8
NA_WIN_W = 16
GQA_HEADS = 16
GQA_KV_HEADS = 4
SWA_HEADS = 16
SWA_KV_HEADS = 2
SWA_WINDOW = 128
LRU_BLOCKS = 16
LRU_BLOCK_DIM = D_MODEL // LRU_BLOCKS
CONV_WIDTH = 4
LRU_C = 8.0
FFN_HIDDEN = 2816
N_MOD = 6
EPS = 1e-6
MASK_VALUE = -1e30
LOG2E = 1.4426950408889634

N_CTX_ROWS = BATCH * CTX_LEN
N_LAT_ROWS = BATCH * SEQ
N_ROWS = N_CTX_ROWS + N_LAT_ROWS
MOD_ROWS = 16
CTX_MOD_ROW = BATCH

LANES = 128
TM = 512
TQ = 256
FFN_CHUNK = 256
GQA_KEY_CHUNK = 512
VMEM_LIMIT = 56 * 1024 * 1024

N_CTX_TILES = N_CTX_ROWS // TM
TILES_PER_BATCH = SEQ // TM
NA_Q_ROWS = TQ // GRID_W
NA_K_ROWS = 12
NA_KEYS = NA_K_ROWS * GRID_W

_NT = (((1,), (1,)), ((), ()))


def _cparams(sem):
    return pltpu.CompilerParams(dimension_semantics=sem, vmem_limit_bytes=VMEM_LIMIT)


def _tile_mod_row(i):
    return jnp.where(i < N_CTX_TILES, CTX_MOD_ROW, (i - N_CTX_TILES) // TILES_PER_BATCH)


def _tile_pos_block(i):
    return jnp.where(i < N_CTX_TILES, 0, 1 + (i - N_CTX_TILES) % TILES_PER_BATCH)


def _ada_norm(x, g, shift, scale):
    ms = jnp.mean(x * x, axis=-1, keepdims=True)
    y = x * lax.rsqrt(ms + EPS)
    return (y * g) * (1.0 + scale) + shift


def _mod_kernel(c_ref, w_ref, b_ref, o_ref):
    c = c_ref[...]
    sc = (c * jax.nn.sigmoid(c)).astype(BF16)
    o_ref[0] = jnp.dot(sc, w_ref[0].astype(BF16), preferred_element_type=F32) + b_ref[0]


def _modulation(c_all, ada_w, ada_b):
    tn = 1536
    n = N_MOD * D_MODEL
    return pl.pallas_call(
        _mod_kernel,
        out_shape=jax.ShapeDtypeStruct((DEPTH, MOD_ROWS, n), F32),
        grid=(DEPTH, n // tn),
        in_specs=[
            pl.BlockSpec((MOD_ROWS, D_MODEL), lambda l, j: (0, 0)),
            pl.BlockSpec((1, D_MODEL, tn), lambda l, j: (l, 0, j)),
            pl.BlockSpec((1, 1, tn), lambda l, j: (l, 0, j)),
        ],
        out_specs=pl.BlockSpec((1, MOD_ROWS, tn), lambda l, j: (l, 0, j)),
        compiler_params=_cparams(("arbitrary", "arbitrary")),
        name="modulation",
    )(c_all, ada_w, ada_b.reshape(DEPTH, 1, n))


def _proj_kernel(*refs, segs, has_rope, has_vt):
    x_ref, mod_ref, g_ref, w_ref = refs[:4]
    k = 4
    if has_rope:
        cos_ref, s1_ref, s2_ref, gain_ref = refs[k:k + 4]
        k += 4
    if has_vt:
        wvt_ref = refs[k]
        k += 1
    out_refs = refs[k:]
    xn = _ada_norm(x_ref[...], g_ref[...], mod_ref[0:1, :], mod_ref[1:2, :]).astype(BF16)
    lane = lax.broadcasted_iota(jnp.int32, (1, LANES), 1)
    lo = lane < HEAD_DIM
    col = 0
    for seg, o_ref in zip(segs, out_refs):
        width, kind, norm_idx, scale = seg
        chunk = 256
        for c0 in range(0, width, chunk):
            cw = min(chunk, width - c0)
            y = jnp.dot(xn, w_ref[:, col + c0:col + c0 + cw], preferred_element_type=F32)
            if kind == "gelu":
                y = jax.nn.gelu(y, approximate=True)
            elif kind == "rope":
                parts = []
                for b0 in range(0, cw, LANES):
                    blk = y[:, b0:b0 + LANES]
                    if norm_idx is not None:
                        ss = blk * blk
                        s_lo = jnp.sum(jnp.where(lo, ss, 0.0), axis=-1, keepdims=True)
                        s_hi = jnp.sum(jnp.where(lo, 0.0, ss), axis=-1, keepdims=True)
                        ms = jnp.where(lo, s_lo, s_hi) * (1.0 / HEAD_DIM)
                        blk = (blk * lax.rsqrt(ms + EPS)) * gain_ref[norm_idx:norm_idx + 1, :]
                    blk = (blk * cos_ref[...]
                           + pltpu.roll(blk, 16, 1) * s1_ref[...]
                           + pltpu.roll(blk, LANES - 16, 1) * s2_ref[...])
                    parts.append(blk)
                y = parts[0] if len(parts) == 1 else jnp.concatenate(parts, axis=-1)
            if scale != 1.0:
                y = y * scale
            o_ref[:, c0:c0 + cw] = y.astype(o_ref.dtype)
        col += width
    if has_vt:
        vt_ref = out_refs[len(segs)]
        for r0 in range(0, wvt_ref.shape[0], 256):
            rw = min(256, wvt_ref.shape[0] - r0)
            vt = lax.dot_general(wvt_ref[r0:r0 + rw, :], xn, _NT, preferred_element_type=F32)
            vt_ref[r0:r0 + rw, :] = vt.astype(vt_ref.dtype)


def _project(hx, mod_l, g, w, segs, out_dtypes, rope_tabs=None, gains=None, w_vt=None):
    n_tot = sum(s[0] for s in segs)
    resident = functools.partial(pl.BlockSpec, pipeline_mode=pl.Buffered(1))
    in_specs = [
        pl.BlockSpec((TM, D_MODEL), lambda i: (i, 0)),
        pl.BlockSpec((None, N_MOD, D_MODEL), lambda i: (_tile_mod_row(i), 0, 0)),
        pl.BlockSpec((1, D_MODEL), lambda i: (0, 0)),
        resident((D_MODEL, n_tot), lambda i: (0, 0)),
    ]
    args = [hx, mod_l, g.reshape(1, D_MODEL), w]
    has_rope = rope_tabs is not None
    if has_rope:
        for t in rope_tabs:
            in_specs.append(pl.BlockSpec((TM, LANES), lambda i: (_tile_pos_block(i), 0)))
            args.append(t)
        in_specs.append(pl.BlockSpec(gains.shape, lambda i: (0, 0)))
        args.append(gains)
    out_shape = [jax.ShapeDtypeStruct((N_ROWS, s[0]), dt) for s, dt in zip(segs, out_dtypes)]
    out_specs = [pl.BlockSpec((TM, s[0]), lambda i: (i, 0)) for s in segs]
    if w_vt is not None:
        in_specs.append(resident(w_vt.shape, lambda i: (0, 0)))
        args.append(w_vt)
        out_shape.append(jax.ShapeDtypeStruct((w_vt.shape[0], N_ROWS), BF16))
        out_specs.append(pl.BlockSpec((w_vt.shape[0], TM), lambda i: (0, i)))
    return pl.pallas_call(
        functools.partial(_proj_kernel, segs=segs, has_rope=has_rope, has_vt=w_vt is not None),
        out_shape=out_shape,
        grid=(N_ROWS // TM,),
        in_specs=in_specs,
        out_specs=out_specs,
        compiler_params=_cparams(("arbitrary",)),
        name="ada_proj",
    )(*args)


def _head_query(q_ref, rows, head, kv_half):
    pair, half = divmod(head, 2)
    q = q_ref[rows, pair * LANES:(pair + 1) * LANES]
    if half != kv_half:
        q = pltpu.roll(q, HEAD_DIM, 1)
    lane = lax.broadcasted_iota(jnp.int32, (1, LANES), 1)
    keep = (lane < HEAD_DIM) if kv_half == 0 else (lane >= HEAD_DIM)
    return jnp.where(keep, q, jnp.zeros_like(q))


def _attention_units(units):
    prev = None
    for idx in range(len(units) + 1):
        cur = units[idx] if idx < len(units) else None
        cur_s, cur_m8 = [], None
        if cur is not None:
            q = cur[0]()
        if prev is not None:
            (_, p_chunks, p_sink, p_store), p_s, p_m = prev
            d8 = jnp.zeros((8, TQ), F32)
            o = jnp.zeros((HEAD_DIM, TQ), F32)
        n_cur = len(cur[1]) if cur is not None else 0
        n_prev = len(p_chunks) if prev is not None else 0
        for c in range(max(n_cur, n_prev)):
            if c < n_cur:
                k_fn, _, bias_fn = cur[1][c]
                s = lax.dot_general(k_fn(), q, _NT, preferred_element_type=F32)
                if bias_fn is not None:
                    s = s + bias_fn()
                cur_s.append(s)
                mc = jnp.max(s.reshape(s.shape[0] // 8, 8, TQ), axis=0)
                cur_m8 = mc if cur_m8 is None else jnp.maximum(cur_m8, mc)
            if c < n_prev:
                e = jnp.exp2(p_s[c] - p_m)
                d8 = d8 + jnp.sum(e.reshape(e.shape[0] // 8, 8, TQ), axis=0)
                o = o + jnp.dot(p_chunks[c][1](), e.astype(BF16), preferred_element_type=F32)
        if prev is not None:
            den = jnp.sum(d8, axis=0, keepdims=True)
            if p_sink is not None:
                den = den + jnp.exp2(p_sink - p_m)
            p_store(o / den)
        if cur is not None:
            m = jnp.max(cur_m8, axis=0, keepdims=True)
            if cur[2] is not None:
                m = jnp.maximum(m, cur[2])
            prev = (cur, cur_s, m)
        else:
            prev = None


def _store_head(o_ref, head, cols):
    def store(o):
        o_ref[head * HEAD_DIM:(head + 1) * HEAD_DIM, cols] = o.astype(o_ref.dtype)
    return store


def _chunks(k_ref, vt_ref, vrows, k0, n_keys, chunk, bias_fn=None):
    out = []
    for r0 in range(0, n_keys, chunk):
        n = min(chunk, n_keys - r0)
        out.append((
            functools.partial(lambda r0, n: k_ref[pl.ds(k0 + r0, n), :], r0, n),
            functools.partial(lambda r0, n: vt_ref[vrows, pl.ds(k0 + r0, n)], r0, n),
            None if bias_fn is None else functools.partial(bias_fn, r0, n)))
    return out


def _head_units(q_ref, rows, o_ref, cols, n_heads, group, kc_ref, vtc_ref, lat=None, bias_fn=None, sink_fn=None):
    units = []
    for head in range(n_heads):
        kv_half = (head // group) % 2
        vrows = slice(kv_half * HEAD_DIM, (kv_half + 1) * HEAD_DIM)
        chunks = _chunks(kc_ref, vtc_ref, vrows, 0, CTX_LEN, CTX_LEN)
        if lat is not None:
            k_ref, vt_ref, k0, n_keys, chunk = lat
            chunks += _chunks(k_ref, vt_ref, vrows, k0, n_keys, chunk,
                              None if bias_fn is None else functools.partial(bias_fn, head))
        units.append((functools.partial(_head_query, q_ref, rows, head, kv_half), chunks,
                      None if sink_fn is None else sink_fn(head), _store_head(o_ref, head, cols)))
    return units


def _na_kernel(qc_ref, ql_ref, kc_ref, vtc_ref, kl_ref, vtl_ref, bias_ref, oc_ref, ol_ref):
    all_rows = slice(None)
    units = _head_units(qc_ref, all_rows, oc_ref, all_rows, 2, 1, kc_ref, vtc_ref)
    n_tiles = SEQ // TQ
    for j in range(n_tiles):
        case = 0 if j == 0 else (2 if j == n_tiles - 1 else 1)
        k0 = min(max(NA_Q_ROWS * j - NA_WIN_H // 2, 0), SEQ // GRID_W - NA_K_ROWS) * GRID_W
        rows = slice(j * TQ, (j + 1) * TQ)
        units += _head_units(ql_ref, rows, ol_ref, rows, 2, 1, kc_ref, vtc_ref, (kl_ref, vtl_ref, k0, NA_KEYS, NA_KEYS),
                             functools.partial(lambda case, head, r0, n: bias_ref[head, case, r0:r0 + n, :], case))
    _attention_units(units)


def _na_attention(q, k, vt, bias):
    lat0 = N_CTX_ROWS // SEQ
    return pl.pallas_call(
        _na_kernel,
        out_shape=[jax.ShapeDtypeStruct((D_MODEL, N_CTX_ROWS), BF16),
                   jax.ShapeDtypeStruct((D_MODEL, N_LAT_ROWS), BF16)],
        grid=(NA_HEADS // 2, BATCH),
        in_specs=[
            pl.BlockSpec((CTX_LEN, LANES), lambda p, b: (b, p)),
            pl.BlockSpec((SEQ, LANES), lambda p, b: (lat0 + b, p)),
            pl.BlockSpec((CTX_LEN, LANES), lambda p, b: (b, p)),
            pl.BlockSpec((LANES, CTX_LEN), lambda p, b: (p, b)),
            pl.BlockSpec((SEQ, LANES), lambda p, b: (lat0 + b, p)),
            pl.BlockSpec((LANES, SEQ), lambda p, b: (p, lat0 + b)),
            pl.BlockSpec((None, 2, 3, NA_KEYS, TQ), lambda p, b: (p, 0, 0, 0, 0)),
        ],
        out_specs=[pl.BlockSpec((LANES, CTX_LEN), lambda p, b: (p, b)),
                   pl.BlockSpec((LANES, SEQ), lambda p, b: (p, b))],
        compiler_params=_cparams(("arbitrary", "arbitrary")),
        name="na_attention",
    )(q, q, k, vt, k, vt, bias)


def _q_tile_index(b, j):
    n_qt = SEQ // TQ
    return jnp.where(j == 0, b, N_CTX_ROWS // TQ + b * n_qt + j - 1)


def _lat_out_index(b, j):
    return b * (SEQ // TQ) + jnp.maximum(j - 1, 0)


def _gqa_kernel(q_ref, kc_ref, vtc_ref, kl_ref, vtl_ref, oc_ref, ol_ref):
    j = pl.program_id(2)
    group = GQA_HEADS // GQA_KV_HEADS
    all_rows = slice(None)

    @pl.when(j == 0)
    def _():
        _attention_units(_head_units(q_ref, all_rows, oc_ref, all_rows, 2 * group, group, kc_ref, vtc_ref))

    @pl.when(j > 0)
    def _():
        _attention_units(_head_units(q_ref, all_rows, ol_ref, all_rows, 2 * group, group, kc_ref, vtc_ref,
                                     (kl_ref, vtl_ref, 0, SEQ, GQA_KEY_CHUNK)))


def _gqa_attention(q, k, vt):
    lat0 = N_CTX_ROWS // SEQ
    n_qt = SEQ // TQ
    qw = 2 * (GQA_HEADS // GQA_KV_HEADS) * HEAD_DIM
    return pl.pallas_call(
        _gqa_kernel,
        out_shape=[jax.ShapeDtypeStruct((D_MODEL, N_CTX_ROWS), BF16),
                   jax.ShapeDtypeStruct((D_MODEL, N_LAT_ROWS), BF16)],
        grid=(BATCH, GQA_KV_HEADS // 2, n_qt + 1),
        in_specs=[
            pl.BlockSpec((TQ, qw), lambda b, p, j: (_q_tile_index(b, j), p)),
            pl.BlockSpec((CTX_LEN, LANES), lambda b, p, j: (b, p)),
            pl.BlockSpec((LANES, CTX_LEN), lambda b, p, j: (p, b)),
            pl.BlockSpec((SEQ, LANES), lambda b, p, j: (lat0 + b, p)),
            pl.BlockSpec((LANES, SEQ), lambda b, p, j: (p, lat0 + b)),
        ],
        out_specs=[pl.BlockSpec((qw, CTX_LEN), lambda b, p, j: (p, b)),
                   pl.BlockSpec((qw, TQ), lambda b, p, j: (p, _lat_out_index(b, j)))],
        compiler_params=_cparams(("arbitrary", "arbitrary", "arbitrary")),
        name="gqa_attention",
    )(q, k, vt, k, vt)


def _swa_kernel(sink_ref, q_ref, kc_ref, vtc_ref, kl_ref, vtl_ref, oc_ref, ol_ref):
    j = pl.program_id(1)
    group = SWA_HEADS // SWA_KV_HEADS
    all_rows = slice(None)
    sink_fn = lambda head: sink_ref[head] * LOG2E

    @pl.when(j == 0)
    def _():
        _attention_units(_head_units(q_ref, all_rows, oc_ref, all_rows, SWA_HEADS, group, kc_ref, vtc_ref,
                                     sink_fn=sink_fn))

    @pl.when(j > 0)
    def _():
        band = TQ + 2 * SWA_WINDOW
        q0 = (j - 1) * TQ
        k0 = pl.multiple_of(jnp.clip(q0 - SWA_WINDOW, 0, SEQ - band), LANES)
        kpos = k0 + lax.broadcasted_iota(jnp.int32, (band, 1), 0)
        qpos = q0 + lax.broadcasted_iota(jnp.int32, (1, TQ), 1)
        mask = jnp.where(jnp.abs(qpos - kpos) <= SWA_WINDOW, 0.0, MASK_VALUE).astype(F32)
        _attention_units(_head_units(q_ref, all_rows, ol_ref, all_rows, SWA_HEADS, group, kc_ref, vtc_ref,
                                     (kl_ref, vtl_ref, k0, band, band),
                                     bias_fn=lambda head, r0, n: mask[r0:r0 + n, :], sink_fn=sink_fn))


def _swa_attention(q, k, vt, sinks):
    lat0 = N_CTX_ROWS // SEQ
    n_qt = SEQ // TQ
    return pl.pallas_call(
        _swa_kernel,
        out_shape=[jax.ShapeDtypeStruct((D_MODEL, N_CTX_ROWS), BF16),
                   jax.ShapeDtypeStruct((D_MODEL, N_LAT_ROWS), BF16)],
        grid=(BATCH, n_qt + 1),
        in_specs=[
            pl.BlockSpec(memory_space=pltpu.SMEM),
            pl.BlockSpec((TQ, D_MODEL), lambda b, j: (_q_tile_index(b, j), 0)),
            pl.BlockSpec((CTX_LEN, LANES), lambda b, j: (b, 0)),
            pl.BlockSpec((LANES, CTX_LEN), lambda b, j: (0, b)),
            pl.BlockSpec((SEQ, LANES), lambda b, j: (lat0 + b, 0)),
            pl.BlockSpec((LANES, SEQ), lambda b, j: (0, lat0 + b)),
        ],
        out_specs=[pl.BlockSpec((D_MODEL, CTX_LEN), lambda b, j: (0, b)),
                   pl.BlockSpec((D_MODEL, TQ), lambda b, j: (0, _lat_out_index(b, j)))],
        compiler_params=_cparams(("arbitrary", "arbitrary")),
        name="swa_attention",
    )(sinks, q, k, vt, k, vt)


def _lru_kernel(xc_ref, xl_ref, cw_ref, cb_ref, wbd_ref, ba_ref, bx_ref, lam_ref, o_ref, a_s, b_s, h_s):
    chunk = CTX_LEN
    n_lat = SEQ // chunk
    sp = jax.nn.softplus(-lam_ref[...])

    def conv_chunk(ref, n_chunks, c):
        zeros = jnp.zeros((8, D_MODEL), F32)
        before = ref[c * chunk - 8:c * chunk, :] if c > 0 else zeros
        after = ref[(c + 1) * chunk:(c + 1) * chunk + 8, :] if c < n_chunks - 1 else zeros
        ext = jnp.concatenate([before, ref[c * chunk:(c + 1) * chunk, :], after], axis=0)
        acc = ext * cw_ref[2:3, :]
        acc = acc + pltpu.roll(ext, 2, 0) * cw_ref[0:1, :]
        acc = acc + pltpu.roll(ext, 1, 0) * cw_ref[1:2, :]
        acc = acc + pltpu.roll(ext, chunk + 15, 0) * cw_ref[3:4, :]
        return acc[8:8 + chunk, :] + cb_ref[...]

    def gates(xc, d):
        xb = xc.astype(BF16)
        pre = []
        for kind in range(2):
            cols = [jnp.dot(xb[:, 256 * c:256 * (c + 1)], wbd_ref[d, kind, c], preferred_element_type=F32)
                    for c in range(D_MODEL // 256)]
            pre.append(jnp.concatenate(cols, axis=-1))
        r = jax.nn.sigmoid(pre[0] + ba_ref[d:d + 1, :])
        ig = jax.nn.sigmoid(pre[1] + bx_ref[d:d + 1, :])
        log_a = (-LRU_C * r) * sp[d:d + 1, :]
        a = jnp.exp(log_a)
        a_s[...] = a
        b_s[...] = jnp.sqrt(-jnp.tanh(log_a) * (a * a + 1.0)) * (ig * xc)

    def scan_chunk(h, reverse):
        def body(i, h):
            t = (chunk - 1 - i) if reverse else i
            h = a_s[pl.ds(t, 1), :] * h + b_s[pl.ds(t, 1), :]
            h_s[pl.ds(t, 1), :] = h
            return h
        return lax.fori_loop(0, chunk, body, h, unroll=8)

    for d in range(2):
        h = jnp.zeros((1, D_MODEL), F32)
        gates(conv_chunk(xc_ref, 1, 0), d)
        h = scan_chunk(h, d == 1)
        order = range(n_lat) if d == 0 else range(n_lat - 1, -1, -1)
        for c in order:
            gates(conv_chunk(xl_ref, n_lat, c), d)
            h = scan_chunk(h, d == 1)
            rows = slice(c * chunk, (c + 1) * chunk)
            if d == 0:
                o_ref[rows, :] = h_s[...]
            else:
                o_ref[rows, :] = o_ref[rows, :] + h_s[...]


def _lru_scan(xr, conv_w, conv_b, wbd, b_a, b_x, lam):
    lat0 = N_CTX_ROWS // SEQ
    const2 = lambda b: (0, 0)
    return pl.pallas_call(
        _lru_kernel,
        out_shape=jax.ShapeDtypeStruct((N_LAT_ROWS, D_MODEL), F32),
        grid=(BATCH,),
        in_specs=[
            pl.BlockSpec((CTX_LEN, D_MODEL), lambda b: (b, 0)),
            pl.BlockSpec((SEQ, D_MODEL), lambda b: (lat0 + b, 0)),
            pl.BlockSpec((CONV_WIDTH, D_MODEL), const2),
            pl.BlockSpec((1, D_MODEL), const2),
            pl.BlockSpec(wbd.shape, lambda b: (0, 0, 0, 0, 0)),
            pl.BlockSpec((2, D_MODEL), const2),
            pl.BlockSpec((2, D_MODEL), const2),
            pl.BlockSpec((2, D_MODEL), const2),
        ],
        out_specs=pl.BlockSpec((SEQ, D_MODEL), lambda b: (b, 0)),
        scratch_shapes=[pltpu.VMEM((CTX_LEN, D_MODEL), F32)] * 3,
        compiler_params=_cparams(("arbitrary",)),
        name="rglru_scan",
    )(xr, xr, conv_w, conv_b.reshape(1, D_MODEL), wbd, b_a, b_x, lam)


def _ffn_kernel(*refs, lru, final):
    h_ref, mod_ref, g_ref, wo_ref, win_ref, wout_ref = refs[:6]
    k = 6
    if lru:
        hs_ref, gl_ref = refs[k:k + 2]
        k += 2
        o_mix = (hs_ref[...] * gl_ref[...]).astype(BF16)
        y = jnp.dot(o_mix, wo_ref[...], preferred_element_type=F32)
    else:
        oc_ref, ol_ref = refs[k:k + 2]
        k += 2
        o_t = jnp.where(pl.program_id(0) < N_CTX_TILES, oc_ref[...], ol_ref[...])
        y = lax.dot_general(o_t, wo_ref[...], (((0,), (0,)), ((), ())), preferred_element_type=F32)
    if final:
        gf_ref = refs[k]
        k += 1
    out_ref = refs[k]
    h1 = h_ref[...] + mod_ref[2:3, :] * y
    xn = _ada_norm(h1, g_ref[...], mod_ref[3:4, :], mod_ref[4:5, :]).astype(BF16)
    acc = jnp.zeros(h1.shape, F32)
    for c0 in range(0, FFN_HIDDEN, FFN_CHUNK):
        a = jnp.dot(xn, win_ref[:, c0:c0 + FFN_CHUNK], preferred_element_type=F32)
        g = jnp.dot(xn, win_ref[:, FFN_HIDDEN + c0:FFN_HIDDEN + c0 + FFN_CHUNK], preferred_element_type=F32)
        u = ((a * jax.nn.sigmoid(a)) * g).astype(BF16)
        acc = acc + jnp.dot(u, wout_ref[c0:c0 + FFN_CHUNK, :], preferred_element_type=F32)
    h2 = h1 + mod_ref[5:6, :] * acc
    if final:
        ms = jnp.mean(h2 * h2, axis=-1, keepdims=True)
        h2 = (h2 * lax.rsqrt(ms + EPS)) * gf_ref[...]
    out_ref[...] = h2


def _mix_ffn(hx, mod_l, g_ffn, w_o, w_in, w_out, mix_inputs, lru=False, final_g=None):
    final = final_g is not None
    t0 = N_CTX_TILES if final else 0
    n_tiles = N_ROWS // TM - t0
    resident = functools.partial(pl.BlockSpec, pipeline_mode=pl.Buffered(1))
    in_specs = [
        pl.BlockSpec((TM, D_MODEL), lambda i: (i + t0, 0)),
        pl.BlockSpec((None, N_MOD, D_MODEL), lambda i: (_tile_mod_row(i + t0), 0, 0)),
        pl.BlockSpec((1, D_MODEL), lambda i: (0, 0)),
        resident(w_o.shape, lambda i: (0, 0)),
        resident(w_in.shape, lambda i: (0, 0)),
        resident(w_out.shape, lambda i: (0, 0)),
    ]
    args = [hx, mod_l, g_ffn.reshape(1, D_MODEL), w_o, w_in, w_out]
    if lru:
        for m, off in mix_inputs:
            in_specs.append(pl.BlockSpec((TM, D_MODEL), lambda i, off=off: (i + off, 0)))
            args.append(m)
    else:
        o_ctx, o_lat = mix_inputs
        in_specs.append(pl.BlockSpec((D_MODEL, TM), lambda i: (0, jnp.minimum(i, N_CTX_TILES - 1))))
        in_specs.append(pl.BlockSpec((D_MODEL, TM), lambda i: (0, jnp.maximum(i - N_CTX_TILES, 0))))
        args += [o_ctx, o_lat]
    if final:
        in_specs.append(pl.BlockSpec((1, D_MODEL), lambda i: (0, 0)))
        args.append(final_g.reshape(1, D_MODEL))
    return pl.pallas_call(
        functools.partial(_ffn_kernel, lru=lru, final=final),
        out_shape=jax.ShapeDtypeStruct((n_tiles * TM, D_MODEL), F32),
        grid=(n_tiles,),
        in_specs=in_specs,
        out_specs=pl.BlockSpec((TM, D_MODEL), lambda i: (i, 0)),
        compiler_params=_cparams(("arbitrary",)),
        name="mix_ffn",
    )(*args)


def _rope_tables():
    t = jnp.arange(SEQ, dtype=jnp.int32)
    row = (t // GRID_W).astype(F32)
    col = (t % GRID_W).astype(F32)
    half = HEAD_DIM // 2
    inv_freq = 1.0 / (ROPE_THETA ** (jnp.arange(0, half, 2, dtype=F32) / half))
    d = jnp.arange(LANES) % HEAD_DIM
    freq = inv_freq[(d % half) % (half // 2)]
    pos = jnp.where((d < half)[None, :], row[:, None], col[:, None])
    ang = pos * freq[None, :]
    first = ((d % half) < half // 2)[None, :]
    cos, sin = jnp.cos(ang), jnp.sin(ang)
    ident = jnp.ones((TM, LANES), F32)
    zero = jnp.zeros((TM, LANES), F32)
    return (jnp.concatenate([ident, cos]),
            jnp.concatenate([zero, jnp.where(first, 0.0, sin)]),
            jnp.concatenate([zero, jnp.where(first, -sin, 0.0)]))


def _na_bias_table(rpb):
    rows = SEQ // GRID_W
    n_drow = 2 * NA_WIN_H - 1
    n_dcol = 2 * NA_WIN_W - 1
    span = 2 * GRID_W
    left = GRID_W - NA_WIN_W
    w = jnp.pad(rpb.astype(F32), ((0, 0), (0, 0), (left, span - left - n_dcol)))
    flat = jnp.tile(w, (1, 1, GRID_W))[..., GRID_W - 1:GRID_W - 1 + GRID_W * (span - 1)]
    toep = flat.reshape(NA_HEADS, n_drow, GRID_W, span - 1)[..., :GRID_W]
    qcol = np.arange(GRID_W)
    col_start = np.clip(qcol - NA_WIN_W // 2, 0, GRID_W - NA_WIN_W)
    col_in = (qcol[None, :] >= col_start[:, None]) & (qcol[None, :] < col_start[:, None] + NA_WIN_W)
    blocks = jnp.where(col_in[None, None], toep, MASK_VALUE)
    blocks = jnp.concatenate([blocks, jnp.full((NA_HEADS, 1, GRID_W, GRID_W), MASK_VALUE, F32)], axis=1)
    idx = []
    for q0, k0 in ((0, 0), (NA_Q_ROWS, 0), (rows - NA_Q_ROWS, rows - NA_K_ROWS)):
        qrow = q0 + np.arange(NA_Q_ROWS)
        krow = k0 + np.arange(NA_K_ROWS)
        r0 = np.clip(qrow - NA_WIN_H // 2, 0, rows - NA_WIN_H)
        row_in = (krow[None, :] >= r0[:, None]) & (krow[None, :] < r0[:, None] + NA_WIN_H)
        idx.append(np.where(row_in, krow[None, :] - qrow[:, None] + NA_WIN_H - 1, n_drow))
    idx = np.stack(idx).astype(np.int32)
    tab = jnp.take(blocks, jnp.asarray(idx.reshape(-1)), axis=1)
    tab = tab.reshape(NA_HEADS // 2, 2, 3, NA_Q_ROWS, NA_K_ROWS, GRID_W, GRID_W)
    return (tab * LOG2E).transpose(0, 1, 2, 4, 6, 3, 5).reshape(NA_HEADS // 2, 2, 3, NA_KEYS, TQ)


def _block_diag_256(w):
    per = 256 // LRU_BLOCK_DIM
    w4 = w.reshape(LRU_BLOCKS // per, per, LRU_BLOCK_DIM, LRU_BLOCK_DIM)
    eye = jnp.eye(per, dtype=w.dtype)
    return jnp.einsum("cide,ij->cidje", w4, eye).reshape(LRU_BLOCKS // per, 256, 256)


def kernel(x, c, ctx, c_ctx, ada_w, ada_b, norm_mix, norm_ffn, norm_final, ffn_w_in, ffn_w_out, na_w_qkv, na_rpb, na_w_o, gqa_w_qkv, gqa_q_gain, gqa_k_gain, gqa_w_o, swa_w_qkv, swa_sinks, swa_w_o, lru_w_in, lru_conv_w, lru_conv_b, lru_w_a, lru_b_a, lru_w_x, lru_b_x, lru_lam, lru_w_out):
    assert x.shape == (BATCH, SEQ, D_MODEL) and ctx.shape == (BATCH, CTX_LEN, D_MODEL)
    hx = jnp.concatenate([ctx.reshape(N_CTX_ROWS, D_MODEL), x.reshape(N_LAT_ROWS, D_MODEL)], axis=0)
    c_all = jnp.zeros((MOD_ROWS, D_MODEL), F32).at[:BATCH].set(c).at[CTX_MOD_ROW].set(c_ctx)
    mod = _modulation(c_all, ada_w, ada_b).reshape(DEPTH, MOD_ROWS, N_MOD, D_MODEL)
    rope_tabs = _rope_tables()
    q_scale = HEAD_DIM ** -0.5 * LOG2E
    lat_tile0 = N_CTX_TILES

    def qkv_split(w, n_q, n_kv):
        wq_k = w[:, :(n_q + n_kv) * HEAD_DIM].astype(BF16)
        return wq_k, w[:, (n_q + n_kv) * HEAD_DIM:].T.astype(BF16)

    w_qk, w_vt = qkv_split(na_w_qkv[0], NA_HEADS, NA_HEADS)
    q, k, vt = _project(hx, mod[0], norm_mix[0], w_qk,
                        ((D_MODEL, "plain", None, q_scale), (D_MODEL, "plain", None, 1.0)), (BF16, BF16), w_vt=w_vt)
    o = _na_attention(q, k, vt, _na_bias_table(na_rpb[0]))
    hx = _mix_ffn(hx, mod[0], norm_ffn[0], na_w_o[0].astype(BF16), ffn_w_in[0].astype(BF16),
                  ffn_w_out[0].astype(BF16), o)

    kvw = GQA_KV_HEADS * HEAD_DIM
    gains = jnp.stack([jnp.tile(gqa_q_gain[0], 2), jnp.tile(gqa_k_gain[0], 2)])
    w_qk, w_vt = qkv_split(gqa_w_qkv[0], GQA_HEADS, GQA_KV_HEADS)
    q, k, vt = _project(hx, mod[1], norm_mix[1], w_qk,
                        ((D_MODEL, "rope", 0, q_scale), (kvw, "rope", 1, 1.0)), (BF16, BF16),
                        rope_tabs, gains, w_vt=w_vt)
    o = _gqa_attention(q, k, vt)
    hx = _mix_ffn(hx, mod[1], norm_ffn[1], gqa_w_o[0].astype(BF16), ffn_w_in[1].astype(BF16),
                  ffn_w_out[1].astype(BF16), o)

    kvw = SWA_KV_HEADS * HEAD_DIM
    w_qk, w_vt = qkv_split(swa_w_qkv[0], SWA_HEADS, SWA_KV_HEADS)
    q, k, vt = _project(hx, mod[2], norm_mix[2], w_qk,
                        ((D_MODEL, "rope", None, q_scale), (kvw, "rope", None, 1.0)), (BF16, BF16),
                        rope_tabs, jnp.ones((2, LANES), F32), w_vt=w_vt)
    o = _swa_attention(q, k, vt, swa_sinks[0])
    hx = _mix_ffn(hx, mod[2], norm_ffn[2], swa_w_o[0].astype(BF16), ffn_w_in[2].astype(BF16),
                  ffn_w_out[2].astype(BF16), o)

    xr, gl = _project(hx, mod[3], norm_mix[3], lru_w_in[0].astype(BF16),
                      ((D_MODEL, "plain", None, 1.0), (D_MODEL, "gelu", None, 1.0)), (F32, F32))
    wbd = jnp.stack([jnp.stack([_block_diag_256(lru_w_a[0, d]), _block_diag_256(lru_w_x[0, d])])
                     for d in range(2)]).astype(BF16)
    hs = _lru_scan(xr, lru_conv_w[0], lru_conv_b[0], wbd, lru_b_a[0], lru_b_x[0], lru_lam[0])
    out = _mix_ffn(hx, mod[3], norm_ffn[3], lru_w_out[0].astype(BF16), ffn_w_in[3].astype(BF16),
                   ffn_w_out[3].astype(BF16), [(hs, 0), (gl, lat_tile0)], lru=True, final_g=norm_final)
    return out.reshape(BATCH, SEQ, D_MODEL)
```

```python
import functools

import jax
import jax.numpy as jnp
import numpy as np
from jax import lax
from jax.experimental import pallas as pl
from jax.experimental.pallas import tpu as pltpu

F32 = jnp.float32
BF16 = jnp.bfloat16

D_MODEL = 1024
BATCH = 8
SEQ = 2048
DEPTH = 4
GRID_W = 64
CTX_LEN = 256
HEAD_DIM = 64
ROPE_THETA = 10000.0
NA_HEADS = 16
NA_WIN_H = 8
NA_WIN_W = 16
GQA_HEADS = 16
GQA_KV_HEADS = 4
SWA_HEADS = 16
SWA_KV_HEADS = 2
SWA_WINDOW = 128
LRU_BLOCKS = 16
LRU_BLOCK_DIM = D_MODEL // LRU_BLOCKS
CONV_WIDTH = 4
LRU_C = 8.0
FFN_HIDDEN = 2816
N_MOD = 6
EPS = 1e-6
MASK_VALUE = -1e30
LOG2E = 1.4426950408889634

N_CTX_ROWS = BATCH * CTX_LEN
N_LAT_ROWS = BATCH * SEQ
N_ROWS = N_CTX_ROWS + N_LAT_ROWS
MOD_ROWS = 16
CTX_MOD_ROW = BATCH

LANES = 128
TM = 512
TQ = 256
FFN_CHUNK = 256
GQA_KEY_CHUNK = 512
SHORT_UNIT_WIDTH = 2
VMEM_LIMIT = 56 * 1024 * 1024

N_CTX_TILES = N_CTX_ROWS // TM
TILES_PER_BATCH = SEQ // TM
NA_Q_ROWS = TQ // GRID_W
NA_K_ROWS = 12
NA_KEYS = NA_K_ROWS * GRID_W

_NT = (((1,), (1,)), ((), ()))


def _cparams(sem):
    return pltpu.CompilerParams(dimension_semantics=sem, vmem_limit_bytes=VMEM_LIMIT)


def _tile_mod_row(i):
    return jnp.where(i < N_CTX_TILES, CTX_MOD_ROW, (i - N_CTX_TILES) // TILES_PER_BATCH)


def _tile_pos_block(i):
    return jnp.where(i < N_CTX_TILES, 0, 1 + (i - N_CTX_TILES) % TILES_PER_BATCH)


def _ada_norm(x, g, shift, scale):
    ms = jnp.mean(x * x, axis=-1, keepdims=True)
    y = x * lax.rsqrt(ms + EPS)
    return (y * g) * (1.0 + scale) + shift


def _mod_kernel(c_ref, w_ref, b_ref, o_ref):
    c = c_ref[...]
    sc = (c * jax.nn.sigmoid(c)).astype(BF16)
    o_ref[0] = jnp.dot(sc, w_ref[0].astype(BF16), preferred_element_type=F32) + b_ref[0]


def _modulation(c_all, ada_w, ada_b):
    tn = 1536
    n = N_MOD * D_MODEL
    return pl.pallas_call(
        _mod_kernel,
        out_shape=jax.ShapeDtypeStruct((DEPTH, MOD_ROWS, n), F32),
        grid=(DEPTH, n // tn),
        in_specs=[
            pl.BlockSpec((MOD_ROWS, D_MODEL), lambda l, j: (0, 0)),
            pl.BlockSpec((1, D_MODEL, tn), lambda l, j: (l, 0, j)),
            pl.BlockSpec((1, 1, tn), lambda l, j: (l, 0, j)),
        ],
        out_specs=pl.BlockSpec((1, MOD_ROWS, tn), lambda l, j: (l, 0, j)),
        compiler_params=_cparams(("arbitrary", "arbitrary")),
        name="modulation",
    )(c_all, ada_w, ada_b.reshape(DEPTH, 1, n))


def _tile_rows(ctx_ref, lat_ref, tile):
    return jnp.where(tile < N_CTX_TILES, ctx_ref[...], lat_ref[...])


def _tile_row_specs(ctx_rows, lat_rows, t0=0):
    return [pl.BlockSpec((TM, D_MODEL), lambda i: (jnp.minimum(i + t0, N_CTX_TILES - 1), 0)),
            pl.BlockSpec((TM, D_MODEL), lambda i: (jnp.maximum(i + t0 - N_CTX_TILES, 0), 0))], [ctx_rows, lat_rows]


def _proj_kernel(*refs, segs, has_rope, has_vt, split_in):
    if split_in:
        x = _tile_rows(refs[0], refs[1], pl.program_id(0))
        refs = refs[1:]
    else:
        x = refs[0][...]
    mod_ref, g_ref, w_ref = refs[1:4]
    k = 4
    if has_rope:
        wsw_ref, cos_ref, sin_ref, gain_ref = refs[k:k + 4]
        k += 4
    if has_vt:
        wvt_ref = refs[k]
        k += 1
    out_refs = refs[k:]
    xn = _ada_norm(x, g_ref[...], mod_ref[0:1, :], mod_ref[1:2, :]).astype(BF16)
    lane = lax.broadcasted_iota(jnp.int32, (1, LANES), 1)
    lo = lane < HEAD_DIM
    col = 0
    for seg, o_ref in zip(segs, out_refs):
        width, kind, norm_idx, scale = seg
        chunk = 256
        for c0 in range(0, width, chunk):
            cw = min(chunk, width - c0)
            y = jnp.dot(xn, w_ref[:, col + c0:col + c0 + cw], preferred_element_type=F32)
            if kind == "gelu":
                y = jax.nn.gelu(y, approximate=True)
            elif kind == "rope":
                ysw = jnp.dot(xn, wsw_ref[:, col + c0:col + c0 + cw], preferred_element_type=F32)
                cos_t, sin_t = cos_ref[...], sin_ref[...]
                if norm_idx is not None:
                    cos_t = cos_t * gain_ref[norm_idx:norm_idx + 1, :]
                    sin_t = sin_t * gain_ref[2 + norm_idx:3 + norm_idx, :]
                parts = []
                for b0 in range(0, cw, LANES):
                    blk = y[:, b0:b0 + LANES]
                    rot = blk * cos_t + ysw[:, b0:b0 + LANES] * sin_t
                    if norm_idx is not None:
                        ss = blk * blk
                        s_lo = jnp.sum(jnp.where(lo, ss, 0.0), axis=-1, keepdims=True)
                        s_hi = jnp.sum(jnp.where(lo, 0.0, ss), axis=-1, keepdims=True)
                        ms = jnp.where(lo, s_lo, s_hi) * (1.0 / HEAD_DIM)
                        rot = rot * lax.rsqrt(ms + EPS)
                    parts.append(rot)
                y = parts[0] if len(parts) == 1 else jnp.concatenate(parts, axis=-1)
            if scale != 1.0:
                y = y * scale
            o_ref[:, c0:c0 + cw] = y.astype(o_ref.dtype)
        col += width
    if has_vt:
        vt_ref = out_refs[len(segs)]
        for r0 in range(0, wvt_ref.shape[0], 256):
            rw = min(256, wvt_ref.shape[0] - r0)
            vt = lax.dot_general(wvt_ref[r0:r0 + rw, :], xn, _NT, preferred_element_type=F32)
            vt_ref[r0:r0 + rw, :] = vt.astype(vt_ref.dtype)


def _swap_rotary_halves(a):
    quarter = HEAD_DIM // 4
    return a.reshape(a.shape[:-1] + (-1, 2, quarter))[..., ::-1, :].reshape(a.shape)


def _project(hx, mod_l, g, w, segs, out_dtypes, rope_tabs=None, gains=None, w_vt=None):
    n_tot = sum(s[0] for s in segs)
    resident = functools.partial(pl.BlockSpec, pipeline_mode=pl.Buffered(1))
    split_in = isinstance(hx, tuple)
    if split_in:
        in_specs, args = _tile_row_specs(*hx)
    else:
        in_specs, args = [pl.BlockSpec((TM, D_MODEL), lambda i: (i, 0))], [hx]
    in_specs += [
        pl.BlockSpec((None, N_MOD, D_MODEL), lambda i: (_tile_mod_row(i), 0, 0)),
        pl.BlockSpec((1, D_MODEL), lambda i: (0, 0)),
        resident((D_MODEL, n_tot), lambda i: (0, 0)),
    ]
    args += [mod_l, g.reshape(1, D_MODEL), w]
    has_rope = rope_tabs is not None
    if has_rope:
        in_specs.append(resident((D_MODEL, n_tot), lambda i: (0, 0)))
        args.append(_swap_rotary_halves(w))
        for t in rope_tabs:
            in_specs.append(pl.BlockSpec((TM, LANES), lambda i: (_tile_pos_block(i), 0)))
            args.append(t)
        in_specs.append(pl.BlockSpec(gains.shape, lambda i: (0, 0)))
        args.append(gains)
    out_shape = [jax.ShapeDtypeStruct((N_ROWS, s[0]), dt) for s, dt in zip(segs, out_dtypes)]
    out_specs = [pl.BlockSpec((TM, s[0]), lambda i: (i, 0)) for s in segs]
    if w_vt is not None:
        in_specs.append(resident(w_vt.shape, lambda i: (0, 0)))
        args.append(w_vt)
        out_shape.append(jax.ShapeDtypeStruct((w_vt.shape[0], N_ROWS), BF16))
        out_specs.append(pl.BlockSpec((w_vt.shape[0], TM), lambda i: (0, i)))
    return pl.pallas_call(
        functools.partial(_proj_kernel, segs=segs, has_rope=has_rope, has_vt=w_vt is not None, split_in=split_in),
        out_shape=out_shape,
        grid=(N_ROWS // TM,),
        in_specs=in_specs,
        out_specs=out_specs,
        compiler_params=_cparams(("arbitrary",)),
        name="ada_proj",
    )(*args)


def _head_query(q_ref, rows, head, kv_half):
    pair, half = divmod(head, 2)
    q = q_ref[rows, pair * LANES:(pair + 1) * LANES]
    if half != kv_half:
        q = pltpu.roll(q, HEAD_DIM, 1)
    lane = lax.broadcasted_iota(jnp.int32, (1, LANES), 1)
    keep = (lane < HEAD_DIM) if kv_half == 0 else (lane >= HEAD_DIM)
    return jnp.where(keep, q, jnp.zeros_like(q))


def _attention_units(units, width=1):
    groups = [units[i:i + width] for i in range(0, len(units), width)]
    prev = []
    for idx in range(len(groups) + 1):
        cur = groups[idx] if idx < len(groups) else []
        qs = [u[0]() for u in cur]
        cur_s = [[] for _ in cur]
        cur_m8 = [None for _ in cur]
        acc = [[jnp.zeros((8, TQ), F32), jnp.zeros((HEAD_DIM, TQ), F32)] for _ in prev]
        n_steps = max([len(u[1]) for u in cur] + [len(p[0][1]) for p in prev] + [0])
        for c in range(n_steps):
            for i, u in enumerate(cur):
                if c < len(u[1]):
                    k_fn, _, bias_fn = u[1][c]
                    s = lax.dot_general(k_fn(), qs[i], _NT, preferred_element_type=F32)
                    if bias_fn is not None:
                        s = s + bias_fn()
                    cur_s[i].append(s)
                    mc = jnp.max(s.reshape(s.shape[0] // 8, 8, TQ), axis=0)
                    cur_m8[i] = mc if cur_m8[i] is None else jnp.maximum(cur_m8[i], mc)
            for i, (u, p_s, p_m) in enumerate(prev):
                if c < len(u[1]):
                    e = jnp.exp2(p_s[c] - p_m)
                    acc[i][0] = acc[i][0] + jnp.sum(e.reshape(e.shape[0] // 8, 8, TQ), axis=0)
                    acc[i][1] = acc[i][1] + jnp.dot(u[1][c][1](), e.astype(BF16), preferred_element_type=F32)
        for i, (u, p_s, p_m) in enumerate(prev):
            den = jnp.sum(acc[i][0], axis=0, keepdims=True)
            if u[2] is not None:
                den = den + jnp.exp2(u[2] - p_m)
            u[3](acc[i][1] / den)
        prev = []
        for i, u in enumerate(cur):
            m = jnp.max(cur_m8[i], axis=0, keepdims=True)
            if u[2] is not None:
                m = jnp.maximum(m, u[2])
            prev.append((u, cur_s[i], m))


def _store_head(o_ref, head, cols):
    def store(o):
        o_ref[head * HEAD_DIM:(head + 1) * HEAD_DIM, cols] = o.astype(o_ref.dtype)
    return store


def _chunks(k_ref, vt_ref, vrows, k0, n_keys, chunk, bias_fn=None):
    out = []
    for r0 in range(0, n_keys, chunk):
        n = min(chunk, n_keys - r0)
        out.append((
            functools.partial(lambda r0, n: k_ref[pl.ds(k0 + r0, n), :], r0, n),
            functools.partial(lambda r0, n: vt_ref[vrows, pl.ds(k0 + r0, n)], r0, n),
            None if bias_fn is None else functools.partial(bias_fn, r0, n)))
    return out


def _head_units(q_ref, rows, o_ref, cols, n_heads, group, kc_ref, vtc_ref, lat=None, bias_fn=None, sink_fn=None):
    units = []
    for head in range(n_heads):
        kv_half = (head // group) % 2
        vrows = slice(kv_half * HEAD_DIM, (kv_half + 1) * HEAD_DIM)
        chunks = _chunks(kc_ref, vtc_ref, vrows, 0, CTX_LEN, CTX_LEN)
        if lat is not None:
            k_ref, vt_ref, k0, n_keys, chunk = lat
            chunks += _chunks(k_ref, vt_ref, vrows, k0, n_keys, chunk,
                              None if bias_fn is None else functools.partial(bias_fn, head))
        units.append((functools.partial(_head_query, q_ref, rows, head, kv_half), chunks,
                      None if sink_fn is None else sink_fn(head), _store_head(o_ref, head, cols)))
    return units


def _na_kernel(qc_ref, ql_ref, kc_ref, vtc_ref, kl_ref, vtl_ref, bias_ref, oc_ref, ol_ref):
    all_rows = slice(None)
    units = _head_units(qc_ref, all_rows, oc_ref, all_rows, 2, 1, kc_ref, vtc_ref)
    n_tiles = SEQ // TQ
    for j in range(n_tiles):
        case = 0 if j == 0 else (2 if j == n_tiles - 1 else 1)
        k0 = min(max(NA_Q_ROWS * j - NA_WIN_H // 2, 0), SEQ // GRID_W - NA_K_ROWS) * GRID_W
        rows = slice(j * TQ, (j + 1) * TQ)
        units += _head_units(ql_ref, rows, ol_ref, rows, 2, 1, kc_ref, vtc_ref, (kl_ref, vtl_ref, k0, NA_KEYS, NA_KEYS),
                             functools.partial(lambda case, head, r0, n: bias_ref[head, case, r0:r0 + n, :], case))
    _attention_units(units)


def _na_attention(q, k, vt, bias):
    lat0 = N_CTX_ROWS // SEQ
    return pl.pallas_call(
        _na_kernel,
        out_shape=[jax.ShapeDtypeStruct((D_MODEL, N_CTX_ROWS), BF16),
                   jax.ShapeDtypeStruct((D_MODEL, N_LAT_ROWS), BF16)],
        grid=(NA_HEADS // 2, BATCH),
        in_specs=[
            pl.BlockSpec((CTX_LEN, LANES), lambda p, b: (b, p)),
            pl.BlockSpec((SEQ, LANES), lambda p, b: (lat0 + b, p)),
            pl.BlockSpec((CTX_LEN, LANES), lambda p, b: (b, p)),
            pl.BlockSpec((LANES, CTX_LEN), lambda p, b: (p, b)),
            pl.BlockSpec((SEQ, LANES), lambda p, b: (lat0 + b, p)),
            pl.BlockSpec((LANES, SEQ), lambda p, b: (p, lat0 + b)),
            pl.BlockSpec((None, 2, 3, NA_KEYS, TQ), lambda p, b: (p, 0, 0, 0, 0)),
        ],
        out_specs=[pl.BlockSpec((LANES, CTX_LEN), lambda p, b: (p, b)),
                   pl.BlockSpec((LANES, SEQ), lambda p, b: (p, b))],
        compiler_params=_cparams(("arbitrary", "arbitrary")),
        name="na_attention",
    )(q, q, k, vt, k, vt, bias)


def _q_tile_index(b, j):
    n_qt = SEQ // TQ
    return jnp.where(j == 0, b, N_CTX_ROWS // TQ + b * n_qt + j - 1)


def _lat_out_index(b, j):
    return b * (SEQ // TQ) + jnp.maximum(j - 1, 0)


def _gqa_kernel(q_ref, kc_ref, vtc_ref, kl_ref, vtl_ref, oc_ref, ol_ref):
    j = pl.program_id(2)
    group = GQA_HEADS // GQA_KV_HEADS
    all_rows = slice(None)

    @pl.when(j == 0)
    def _():
        _attention_units(_head_units(q_ref, all_rows, oc_ref, all_rows, 2 * group, group, kc_ref, vtc_ref),
                         SHORT_UNIT_WIDTH)

    @pl.when(j > 0)
    def _():
        _attention_units(_head_units(q_ref, all_rows, ol_ref, all_rows, 2 * group, group, kc_ref, vtc_ref,
                                     (kl_ref, vtl_ref, 0, SEQ, GQA_KEY_CHUNK)))


def _gqa_attention(q, k, vt):
    lat0 = N_CTX_ROWS // SEQ
    n_qt = SEQ // TQ
    qw = 2 * (GQA_HEADS // GQA_KV_HEADS) * HEAD_DIM
    return pl.pallas_call(
        _gqa_kernel,
        out_shape=[jax.ShapeDtypeStruct((D_MODEL, N_CTX_ROWS), BF16),
                   jax.ShapeDtypeStruct((D_MODEL, N_LAT_ROWS), BF16)],
        grid=(BATCH, GQA_KV_HEADS // 2, n_qt + 1),
        in_specs=[
            pl.BlockSpec((TQ, qw), lambda b, p, j: (_q_tile_index(b, j), p)),
            pl.BlockSpec((CTX_LEN, LANES), lambda b, p, j: (b, p)),
            pl.BlockSpec((LANES, CTX_LEN), lambda b, p, j: (p, b)),
            pl.BlockSpec((SEQ, LANES), lambda b, p, j: (lat0 + b, p)),
            pl.BlockSpec((LANES, SEQ), lambda b, p, j: (p, lat0 + b)),
        ],
        out_specs=[pl.BlockSpec((qw, CTX_LEN), lambda b, p, j: (p, b)),
                   pl.BlockSpec((qw, TQ), lambda b, p, j: (p, _lat_out_index(b, j)))],
        compiler_params=_cparams(("arbitrary", "arbitrary", "arbitrary")),
        name="gqa_attention",
    )(q, k, vt, k, vt)


def _swa_kernel(sink_ref, q_ref, kc_ref, vtc_ref, kl_ref, vtl_ref, oc_ref, ol_ref):
    j = pl.program_id(1)
    group = SWA_HEADS // SWA_KV_HEADS
    all_rows = slice(None)
    sink_fn = lambda head: sink_ref[head] * LOG2E

    @pl.when(j == 0)
    def _():
        _attention_units(_head_units(q_ref, all_rows, oc_ref, all_rows, SWA_HEADS, group, kc_ref, vtc_ref,
                                     sink_fn=sink_fn), SHORT_UNIT_WIDTH)

    @pl.when(j > 0)
    def _():
        band = TQ + 2 * SWA_WINDOW
        q0 = (j - 1) * TQ
        k0 = pl.multiple_of(jnp.clip(q0 - SWA_WINDOW, 0, SEQ - band), LANES)
        kpos = k0 + lax.broadcasted_iota(jnp.int32, (band, 1), 0)
        qpos = q0 + lax.broadcasted_iota(jnp.int32, (1, TQ), 1)
        mask = jnp.where(jnp.abs(qpos - kpos) <= SWA_WINDOW, 0.0, MASK_VALUE).astype(F32)
        _attention_units(_head_units(q_ref, all_rows, ol_ref, all_rows, SWA_HEADS, group, kc_ref, vtc_ref,
                                     (kl_ref, vtl_ref, k0, band, band),
                                     bias_fn=lambda head, r0, n: mask[r0:r0 + n, :], sink_fn=sink_fn),
                         SHORT_UNIT_WIDTH)


def _swa_attention(q, k, vt, sinks):
    lat0 = N_CTX_ROWS // SEQ
    n_qt = SEQ // TQ
    return pl.pallas_call(
        _swa_kernel,
        out_shape=[jax.ShapeDtypeStruct((D_MODEL, N_CTX_ROWS), BF16),
                   jax.ShapeDtypeStruct((D_MODEL, N_LAT_ROWS), BF16)],
        grid=(BATCH, n_qt + 1),
        in_specs=[
            pl.BlockSpec(memory_space=pltpu.SMEM),
            pl.BlockSpec((TQ, D_MODEL), lambda b, j: (_q_tile_index(b, j), 0)),
            pl.BlockSpec((CTX_LEN, LANES), lambda b, j: (b, 0)),
            pl.BlockSpec((LANES, CTX_LEN), lambda b, j: (0, b)),
            pl.BlockSpec((SEQ, LANES), lambda b, j: (lat0 + b, 0)),
            pl.BlockSpec((LANES, SEQ), lambda b, j: (0, lat0 + b)),
        ],
        out_specs=[pl.BlockSpec((D_MODEL, CTX_LEN), lambda b, j: (0, b)),
                   pl.BlockSpec((D_MODEL, TQ), lambda b, j: (0, _lat_out_index(b, j)))],
        compiler_params=_cparams(("arbitrary", "arbitrary")),
        name="swa_attention",
    )(sinks, q, k, vt, k, vt)


def _lru_kernel(xc_ref, xl_ref, cw_ref, cb_ref, wbd_ref, ba_ref, bx_ref, lam_ref, o_ref, a_s, b_s, h_s):
    chunk = CTX_LEN
    n_lat = SEQ // chunk
    sp = jax.nn.softplus(-lam_ref[...])

    def conv_chunk(ref, n_chunks, c):
        zeros = jnp.zeros((8, D_MODEL), F32)
        before = ref[c * chunk - 8:c * chunk, :] if c > 0 else zeros
        after = ref[(c + 1) * chunk:(c + 1) * chunk + 8, :] if c < n_chunks - 1 else zeros
        ext = jnp.concatenate([before, ref[c * chunk:(c + 1) * chunk, :], after], axis=0)
        acc = ext * cw_ref[2:3, :]
        acc = acc + pltpu.roll(ext, 2, 0) * cw_ref[0:1, :]
        acc = acc + pltpu.roll(ext, 1, 0) * cw_ref[1:2, :]
        acc = acc + pltpu.roll(ext, chunk + 15, 0) * cw_ref[3:4, :]
        return acc[8:8 + chunk, :] + cb_ref[...]

    def gates(xc, d):
        xb = xc.astype(BF16)
        pre = []
        for kind in range(2):
            cols = [jnp.dot(xb[:, 256 * c:256 * (c + 1)], wbd_ref[d, kind, c], preferred_element_type=F32)
                    for c in range(D_MODEL // 256)]
            pre.append(jnp.concatenate(cols, axis=-1))
        r = 0.5 * jnp.tanh(0.5 * (pre[0] + ba_ref[d:d + 1, :])) + 0.5
        ig = 0.5 * jnp.tanh(0.5 * (pre[1] + bx_ref[d:d + 1, :])) + 0.5
        log_a = (-LRU_C * r) * sp[d:d + 1, :]
        a = jnp.exp(log_a)
        a_s[...] = a
        b_s[...] = jnp.sqrt(-jnp.tanh(log_a) * (a * a + 1.0)) * (ig * xc)

    def scan_chunk(h, reverse):
        def body(i, h):
            t = (chunk - 1 - i) if reverse else i
            h = a_s[pl.ds(t, 1), :] * h + b_s[pl.ds(t, 1), :]
            h_s[pl.ds(t, 1), :] = h
            return h
        return lax.fori_loop(0, chunk, body, h, unroll=8)

    for d in range(2):
        h = jnp.zeros((1, D_MODEL), F32)
        gates(conv_chunk(xc_ref, 1, 0), d)
        h = scan_chunk(h, d == 1)
        order = range(n_lat) if d == 0 else range(n_lat - 1, -1, -1)
        for c in order:
            gates(conv_chunk(xl_ref, n_lat, c), d)
            h = scan_chunk(h, d == 1)
            rows = slice(c * chunk, (c + 1) * chunk)
            if d == 0:
                o_ref[rows, :] = h_s[...]
            else:
                o_ref[rows, :] = o_ref[rows, :] + h_s[...]


def _lru_scan(xr, conv_w, conv_b, wbd, b_a, b_x, lam):
    lat0 = N_CTX_ROWS // SEQ
    const2 = lambda b: (0, 0)
    return pl.pallas_call(
        _lru_kernel,
        out_shape=jax.ShapeDtypeStruct((N_LAT_ROWS, D_MODEL), F32),
        grid=(BATCH,),
        in_specs=[
            pl.BlockSpec((CTX_LEN, D_MODEL), lambda b: (b, 0)),
            pl.BlockSpec((SEQ, D_MODEL), lambda b: (lat0 + b, 0)),
            pl.BlockSpec((CONV_WIDTH, D_MODEL), const2),
            pl.BlockSpec((1, D_MODEL), const2),
            pl.BlockSpec(wbd.shape, lambda b: (0, 0, 0, 0, 0)),
            pl.BlockSpec((2, D_MODEL), const2),
            pl.BlockSpec((2, D_MODEL), const2),
            pl.BlockSpec((2, D_MODEL), const2),
        ],
        out_specs=pl.BlockSpec((SEQ, D_MODEL), lambda b: (b, 0)),
        scratch_shapes=[pltpu.VMEM((CTX_LEN, D_MODEL), F32)] * 3,
        compiler_params=_cparams(("arbitrary",)),
        name="rglru_scan",
    )(xr, xr, conv_w, conv_b.reshape(1, D_MODEL), wbd, b_a, b_x, lam)


def _ffn_kernel(*refs, lru, final, split_in):
    if split_in:
        h = _tile_rows(refs[0], refs[1], pl.program_id(0))
        refs = refs[1:]
    else:
        h = refs[0][...]
    mod_ref, g_ref, wo_ref, win_ref, wout_ref = refs[1:6]
    k = 6
    if lru:
        hs_ref, gl_ref = refs[k:k + 2]
        k += 2
        o_mix = (hs_ref[...] * gl_ref[...]).astype(BF16)
        y = jnp.dot(o_mix, wo_ref[...], preferred_element_type=F32)
    else:
        oc_ref, ol_ref = refs[k:k + 2]
        k += 2
        o_t = jnp.where(pl.program_id(0) < N_CTX_TILES, oc_ref[...], ol_ref[...])
        y = lax.dot_general(o_t, wo_ref[...], (((0,), (0,)), ((), ())), preferred_element_type=F32)
    if final:
        gf_ref = refs[k]
        k += 1
    out_ref = refs[k]
    h1 = h + mod_ref[2:3, :] * y
    xn = _ada_norm(h1, g_ref[...], mod_ref[3:4, :], mod_ref[4:5, :]).astype(BF16)
    acc = jnp.zeros(h1.shape, F32)
    for c0 in range(0, FFN_HIDDEN, FFN_CHUNK):
        a = jnp.dot(xn, win_ref[:, c0:c0 + FFN_CHUNK], preferred_element_type=F32)
        g = jnp.dot(xn, win_ref[:, FFN_HIDDEN + c0:FFN_HIDDEN + c0 + FFN_CHUNK], preferred_element_type=F32)
        u = ((a * jax.nn.sigmoid(a)) * g).astype(BF16)
        acc = acc + jnp.dot(u, wout_ref[c0:c0 + FFN_CHUNK, :], preferred_element_type=F32)
    h2 = h1 + mod_ref[5:6, :] * acc
    if final:
        ms = jnp.mean(h2 * h2, axis=-1, keepdims=True)
        h2 = (h2 * lax.rsqrt(ms + EPS)) * gf_ref[...]
    out_ref[...] = h2


def _mix_ffn(hx, mod_l, g_ffn, w_o, w_in, w_out, mix_inputs, lru=False, final_g=None):
    final = final_g is not None
    t0 = N_CTX_TILES if final else 0
    n_tiles = N_ROWS // TM - t0
    resident = functools.partial(pl.BlockSpec, pipeline_mode=pl.Buffered(1))
    split_in = isinstance(hx, tuple)
    assert not (split_in and final)
    if split_in:
        in_specs, args = _tile_row_specs(*hx)
    else:
        in_specs, args = [pl.BlockSpec((TM, D_MODEL), lambda i: (i + t0, 0))], [hx]
    in_specs += [
        pl.BlockSpec((None, N_MOD, D_MODEL), lambda i: (_tile_mod_row(i + t0), 0, 0)),
        pl.BlockSpec((1, D_MODEL), lambda i: (0, 0)),
        resident(w_o.shape, lambda i: (0, 0)),
        resident(w_in.shape, lambda i: (0, 0)),
        resident(w_out.shape, lambda i: (0, 0)),
    ]
    args += [mod_l, g_ffn.reshape(1, D_MODEL), w_o, w_in, w_out]
    if lru:
        for m, off in mix_inputs:
            in_specs.append(pl.BlockSpec((TM, D_MODEL), lambda i, off=off: (i + off, 0)))
            args.append(m)
    else:
        o_ctx, o_lat = mix_inputs
        in_specs.append(pl.BlockSpec((D_MODEL, TM), lambda i: (0, jnp.minimum(i, N_CTX_TILES - 1))))
        in_specs.append(pl.BlockSpec((D_MODEL, TM), lambda i: (0, jnp.maximum(i - N_CTX_TILES, 0))))
        args += [o_ctx, o_lat]
    if final:
        in_specs.append(pl.BlockSpec((1, D_MODEL), lambda i: (0, 0)))
        args.append(final_g.reshape(1, D_MODEL))
    return pl.pallas_call(
        functools.partial(_ffn_kernel, lru=lru, final=final, split_in=split_in),
        out_shape=jax.ShapeDtypeStruct((n_tiles * TM, D_MODEL), F32),
        grid=(n_tiles,),
        in_specs=in_specs,
        out_specs=pl.BlockSpec((TM, D_MODEL), lambda i: (i, 0)),
        compiler_params=_cparams(("arbitrary",)),
        name="mix_ffn",
    )(*args)


def _rope_tables():
    t = jnp.arange(SEQ, dtype=jnp.int32)
    row = (t // GRID_W).astype(F32)
    col = (t % GRID_W).astype(F32)
    half = HEAD_DIM // 2
    inv_freq = 1.0 / (ROPE_THETA ** (jnp.arange(0, half, 2, dtype=F32) / half))
    d = jnp.arange(LANES) % HEAD_DIM
    freq = inv_freq[(d % half) % (half // 2)]
    pos = jnp.where((d < half)[None, :], row[:, None], col[:, None])
    ang = pos * freq[None, :]
    first = ((d % half) < half // 2)[None, :]
    cos, sin = jnp.cos(ang), jnp.sin(ang)
    ident = jnp.ones((TM, LANES), F32)
    zero = jnp.zeros((TM, LANES), F32)
    return jnp.concatenate([ident, cos]), jnp.concatenate([zero, jnp.where(first, -sin, sin)])


def _na_bias_table(rpb):
    rows = SEQ // GRID_W
    n_drow = 2 * NA_WIN_H - 1
    n_dcol = 2 * NA_WIN_W - 1
    span = 2 * GRID_W
    left = GRID_W - NA_WIN_W
    w = jnp.pad(rpb.astype(F32), ((0, 0), (0, 0), (left, span - left - n_dcol)))
    flat = jnp.tile(w, (1, 1, GRID_W))[..., GRID_W - 1:GRID_W - 1 + GRID_W * (span - 1)]
    toep = flat.reshape(NA_HEADS, n_drow, GRID_W, span - 1)[..., :GRID_W]
    qcol = np.arange(GRID_W)
    col_start = np.clip(qcol - NA_WIN_W // 2, 0, GRID_W - NA_WIN_W)
    col_in = (qcol[None, :] >= col_start[:, None]) & (qcol[None, :] < col_start[:, None] + NA_WIN_W)
    blocks = jnp.where(col_in[None, None], toep * LOG2E, MASK_VALUE)
    blocks = jnp.concatenate([blocks, jnp.full((NA_HEADS, 1, GRID_W, GRID_W), MASK_VALUE, F32)], axis=1)
    idx = []
    for q0, k0 in ((0, 0), (NA_Q_ROWS, 0), (rows - NA_Q_ROWS, rows - NA_K_ROWS)):
        qrow = q0 + np.arange(NA_Q_ROWS)
        krow = k0 + np.arange(NA_K_ROWS)
        r0 = np.clip(qrow - NA_WIN_H // 2, 0, rows - NA_WIN_H)
        row_in = (krow[None, :] >= r0[:, None]) & (krow[None, :] < r0[:, None] + NA_WIN_H)
        idx.append(np.where(row_in, krow[None, :] - qrow[:, None] + NA_WIN_H - 1, n_drow))
    idx = np.stack(idx).astype(np.int32)
    tab = jnp.take(blocks, jnp.asarray(idx.reshape(-1)), axis=1)
    tab = tab.reshape(NA_HEADS // 2, 2, 3, NA_Q_ROWS, NA_K_ROWS, GRID_W, GRID_W)
    return tab.transpose(0, 1, 2, 4, 6, 3, 5).reshape(NA_HEADS // 2, 2, 3, NA_KEYS, TQ)


def _block_diag_256(w):
    per = 256 // LRU_BLOCK_DIM
    w4 = w.reshape(LRU_BLOCKS // per, per, LRU_BLOCK_DIM, LRU_BLOCK_DIM)
    eye = jnp.eye(per, dtype=w.dtype)
    return jnp.einsum("cide,ij->cidje", w4, eye).reshape(LRU_BLOCKS // per, 256, 256)


def kernel(x, c, ctx, c_ctx, ada_w, ada_b, norm_mix, norm_ffn, norm_final, ffn_w_in, ffn_w_out, na_w_qkv, na_rpb, na_w_o, gqa_w_qkv, gqa_q_gain, gqa_k_gain, gqa_w_o, swa_w_qkv, swa_sinks, swa_w_o, lru_w_in, lru_conv_w, lru_conv_b, lru_w_a, lru_b_a, lru_w_x, lru_b_x, lru_lam, lru_w_out):
    assert x.shape == (BATCH, SEQ, D_MODEL) and ctx.shape == (BATCH, CTX_LEN, D_MODEL)
    hx = (ctx.reshape(N_CTX_ROWS, D_MODEL), x.reshape(N_LAT_ROWS, D_MODEL))
    c_all = jnp.zeros((MOD_ROWS, D_MODEL), F32).at[:BATCH].set(c).at[CTX_MOD_ROW].set(c_ctx)
    mod = _modulation(c_all, ada_w, ada_b).reshape(DEPTH, MOD_ROWS, N_MOD, D_MODEL)
    rope_tabs = _rope_tables()
    q_scale = HEAD_DIM ** -0.5 * LOG2E
    lat_tile0 = N_CTX_TILES

    def qkv_split(w, n_q, n_kv):
        wq_k = w[:, :(n_q + n_kv) * HEAD_DIM].astype(BF16)
        return wq_k, w[:, (n_q + n_kv) * HEAD_DIM:].T.astype(BF16)

    w_qk, w_vt = qkv_split(na_w_qkv[0], NA_HEADS, NA_HEADS)
    q, k, vt = _project(hx, mod[0], norm_mix[0], w_qk,
                        ((D_MODEL, "plain", None, q_scale), (D_MODEL, "plain", None, 1.0)), (BF16, BF16), w_vt=w_vt)
    o = _na_attention(q, k, vt, _na_bias_table(na_rpb[0]))
    hx = _mix_ffn(hx, mod[0], norm_ffn[0], na_w_o[0].astype(BF16), ffn_w_in[0].astype(BF16),
                  ffn_w_out[0].astype(BF16), o)

    kvw = GQA_KV_HEADS * HEAD_DIM
    gains = jnp.stack([jnp.tile(g_, 2) for g_ in (gqa_q_gain[0], gqa_k_gain[0],
                                                  _swap_rotary_halves(gqa_q_gain[0]), _swap_rotary_halves(gqa_k_gain[0]))])
    w_qk, w_vt = qkv_split(gqa_w_qkv[0], GQA_HEADS, GQA_KV_HEADS)
    q, k, vt = _project(hx, mod[1], norm_mix[1], w_qk,
                        ((D_MODEL, "rope", 0, q_scale), (kvw, "rope", 1, 1.0)), (BF16, BF16),
                        rope_tabs, gains, w_vt=w_vt)
    o = _gqa_attention(q, k, vt)
    hx = _mix_ffn(hx, mod[1], norm_ffn[1], gqa_w_o[0].astype(BF16), ffn_w_in[1].astype(BF16),
                  ffn_w_out[1].astype(BF16), o)

    kvw = SWA_KV_HEADS * HEAD_DIM
    w_qk, w_vt = qkv_split(swa_w_qkv[0], SWA_HEADS, SWA_KV_HEADS)
    q, k, vt = _project(hx, mod[2], norm_mix[2], w_qk,
                        ((D_MODEL, "rope", None, q_scale), (kvw, "rope", None, 1.0)), (BF16, BF16),
                        rope_tabs, jnp.ones((4, LANES), F32), w_vt=w_vt)
    o = _swa_attention(q, k, vt, swa_sinks[0])
    hx = _mix_ffn(hx, mod[2], norm_ffn[2], swa_w_o[0].astype(BF16), ffn_w_in[2].astype(BF16),
                  ffn_w_out[2].astype(BF16), o)

    xr, gl = _project(hx, mod[3], norm_mix[3], lru_w_in[0].astype(BF16),
                      ((D_MODEL, "plain", None, 1.0), (D_MODEL, "gelu", None, 1.0)), (F32, F32))
    wbd = jnp.stack([jnp.stack([_block_diag_256(lru_w_a[0, d]), _block_diag_256(lru_w_x[0, d])])
                     for d in range(2)]).astype(BF16)
    hs = _lru_scan(xr, lru_conv_w[0], lru_conv_b[0], wbd, lru_b_a[0], lru_b_x[0], lru_lam[0])
    out = _mix_ffn(hx, mod[3], norm_ffn[3], lru_w_out[0].astype(BF16), ffn_w_in[3].astype(BF16),
                   ffn_w_out[3].astype(BF16), [(hs, 0), (gl, lat_tile0)], lru=True, final_g=norm_final)
    return out.reshape(BATCH, SEQ, D_MODEL)
```

```python
import functools

import jax
import jax.numpy as jnp
import numpy as np
from jax import lax
from jax.experimental import pallas as pl
from jax.experimental.pallas import tpu as pltpu

F32 = jnp.float32
BF16 = jnp.bfloat16

D_MODEL = 1024
BATCH = 8
SEQ = 2048
DEPTH = 4
GRID_W = 64
CTX_LEN = 256
HEAD_DIM = 64
ROPE_THETA = 10000.0
NA_HEADS = 16
NA_WIN_H = 8
NA_WIN_W = 16
GQA_HEADS = 16
GQA_KV_HEADS = 4
SWA_HEADS = 16
SWA_KV_HEADS = 2
SWA_WINDOW = 128
LRU_BLOCKS = 16
LRU_BLOCK_DIM = D_MODEL // LRU_BLOCKS
CONV_WIDTH = 4
LRU_C = 8.0
FFN_HIDDEN = 2816
N_MOD = 6
EPS = 1e-6
MASK_VALUE = -1e30
LOG2E = 1.4426950408889634

N_CTX_ROWS = BATCH * CTX_LEN
N_LAT_ROWS = BATCH * SEQ
N_ROWS = N_CTX_ROWS + N_LAT_ROWS
MOD_ROWS = 16
CTX_MOD_ROW = BATCH

LANES = 128
TM = 512
TQ = 256
FFN_CHUNK = 256
GQA_KEY_CHUNK = 512
SHORT_UNIT_WIDTH = 2
VMEM_LIMIT = 56 * 1024 * 1024

N_CTX_TILES = N_CTX_ROWS // TM
TILES_PER_BATCH = SEQ // TM
NA_Q_ROWS = TQ // GRID_W
NA_K_ROWS = 12
NA_KEYS = NA_K_ROWS * GRID_W
NA_MASKED = 2 * NA_WIN_H - 1

_NT = (((1,), (1,)), ((), ()))


def _cparams(sem):
    return pltpu.CompilerParams(dimension_semantics=sem, vmem_limit_bytes=VMEM_LIMIT)


def _tile_mod_row(i):
    return jnp.where(i < N_CTX_TILES, CTX_MOD_ROW, (i - N_CTX_TILES) // TILES_PER_BATCH)


def _tile_pos_block(i):
    return jnp.where(i < N_CTX_TILES, 0, 1 + (i - N_CTX_TILES) % TILES_PER_BATCH)


def _ada_norm(x, g, shift, scale):
    ms = jnp.mean(x * x, axis=-1, keepdims=True)
    y = x * lax.rsqrt(ms + EPS)
    return (y * g) * (1.0 + scale) + shift


def _mod_kernel(c_ref, w_ref, b_ref, o_ref):
    c = c_ref[...]
    sc = (c * jax.nn.sigmoid(c)).astype(BF16)
    o_ref[0] = jnp.dot(sc, w_ref[0].astype(BF16), preferred_element_type=F32) + b_ref[0]


def _modulation(c_all, ada_w, ada_b):
    tn = 1536
    n = N_MOD * D_MODEL
    return pl.pallas_call(
        _mod_kernel,
        out_shape=jax.ShapeDtypeStruct((DEPTH, MOD_ROWS, n), F32),
        grid=(DEPTH, n // tn),
        in_specs=[
            pl.BlockSpec((MOD_ROWS, D_MODEL), lambda l, j: (0, 0)),
            pl.BlockSpec((1, D_MODEL, tn), lambda l, j: (l, 0, j)),
            pl.BlockSpec((1, 1, tn), lambda l, j: (l, 0, j)),
        ],
        out_specs=pl.BlockSpec((1, MOD_ROWS, tn), lambda l, j: (l, 0, j)),
        compiler_params=_cparams(("arbitrary", "arbitrary")),
        name="modulation",
    )(c_all, ada_w, ada_b.reshape(DEPTH, 1, n))


def _tile_rows(ctx_ref, lat_ref, tile):
    return jnp.where(tile < N_CTX_TILES, ctx_ref[...], lat_ref[...])


def _tile_row_specs(ctx_rows, lat_rows, t0=0):
    return [pl.BlockSpec((TM, D_MODEL), lambda i: (jnp.minimum(i + t0, N_CTX_TILES - 1), 0)),
            pl.BlockSpec((TM, D_MODEL), lambda i: (jnp.maximum(i + t0 - N_CTX_TILES, 0), 0))], [ctx_rows, lat_rows]


def _proj_kernel(*refs, segs, has_rope, has_vt, split_in):
    if split_in:
        x = _tile_rows(refs[0], refs[1], pl.program_id(0))
        refs = refs[1:]
    else:
        x = refs[0][...]
    mod_ref, g_ref, w_ref = refs[1:4]
    k = 4
    if has_rope:
        wsw_ref, cos_ref, sin_ref, gain_ref = refs[k:k + 4]
        k += 4
    if has_vt:
        wvt_ref = refs[k]
        k += 1
    out_refs = refs[k:]
    xn = _ada_norm(x, g_ref[...], mod_ref[0:1, :], mod_ref[1:2, :]).astype(BF16)
    lane = lax.broadcasted_iota(jnp.int32, (1, LANES), 1)
    lo = lane < HEAD_DIM
    col = 0
    for seg, o_ref in zip(segs, out_refs):
        width, kind, norm_idx, scale = seg
        chunk = 256
        for c0 in range(0, width, chunk):
            cw = min(chunk, width - c0)
            y = jnp.dot(xn, w_ref[:, col + c0:col + c0 + cw], preferred_element_type=F32)
            if kind == "gelu":
                y = jax.nn.gelu(y, approximate=True)
            elif kind == "rope":
                ysw = jnp.dot(xn, wsw_ref[:, col + c0:col + c0 + cw], preferred_element_type=F32)
                cos_t, sin_t = cos_ref[...], sin_ref[...]
                if norm_idx is not None:
                    cos_t = cos_t * gain_ref[norm_idx:norm_idx + 1, :]
                    sin_t = sin_t * gain_ref[2 + norm_idx:3 + norm_idx, :]
                parts = []
                for b0 in range(0, cw, LANES):
                    blk = y[:, b0:b0 + LANES]
                    rot = blk * cos_t + ysw[:, b0:b0 + LANES] * sin_t
                    if norm_idx is not None:
                        ss = blk * blk
                        s_lo = jnp.sum(jnp.where(lo, ss, 0.0), axis=-1, keepdims=True)
                        s_hi = jnp.sum(jnp.where(lo, 0.0, ss), axis=-1, keepdims=True)
                        ms = jnp.where(lo, s_lo, s_hi) * (1.0 / HEAD_DIM)
                        rot = rot * lax.rsqrt(ms + EPS)
                    parts.append(rot)
                y = parts[0] if len(parts) == 1 else jnp.concatenate(parts, axis=-1)
            if scale != 1.0:
                y = y * scale
            o_ref[:, c0:c0 + cw] = y.astype(o_ref.dtype)
        col += width
    if has_vt:
        vt_ref = out_refs[len(segs)]
        for r0 in range(0, wvt_ref.shape[0], 256):
            rw = min(256, wvt_ref.shape[0] - r0)
            vt = lax.dot_general(wvt_ref[r0:r0 + rw, :], xn, _NT, preferred_element_type=F32)
            vt_ref[r0:r0 + rw, :] = vt.astype(vt_ref.dtype)


def _swap_rotary_halves(a):
    quarter = HEAD_DIM // 4
    return a.reshape(a.shape[:-1] + (-1, 2, quarter))[..., ::-1, :].reshape(a.shape)


def _project(hx, mod_l, g, w, segs, out_dtypes, rope_tabs=None, gains=None, w_vt=None):
    n_tot = sum(s[0] for s in segs)
    resident = functools.partial(pl.BlockSpec, pipeline_mode=pl.Buffered(1))
    split_in = isinstance(hx, tuple)
    if split_in:
        in_specs, args = _tile_row_specs(*hx)
    else:
        in_specs, args = [pl.BlockSpec((TM, D_MODEL), lambda i: (i, 0))], [hx]
    in_specs += [
        pl.BlockSpec((None, N_MOD, D_MODEL), lambda i: (_tile_mod_row(i), 0, 0)),
        pl.BlockSpec((1, D_MODEL), lambda i: (0, 0)),
        resident((D_MODEL, n_tot), lambda i: (0, 0)),
    ]
    args += [mod_l, g.reshape(1, D_MODEL), w]
    has_rope = rope_tabs is not None
    if has_rope:
        in_specs.append(resident((D_MODEL, n_tot), lambda i: (0, 0)))
        args.append(_swap_rotary_halves(w))
        for t in rope_tabs:
            in_specs.append(pl.BlockSpec((TM, LANES), lambda i: (_tile_pos_block(i), 0)))
            args.append(t)
        in_specs.append(pl.BlockSpec(gains.shape, lambda i: (0, 0)))
        args.append(gains)
    out_shape = [jax.ShapeDtypeStruct((N_ROWS, s[0]), dt) for s, dt in zip(segs, out_dtypes)]
    out_specs = [pl.BlockSpec((TM, s[0]), lambda i: (i, 0)) for s in segs]
    if w_vt is not None:
        in_specs.append(resident(w_vt.shape, lambda i: (0, 0)))
        args.append(w_vt)
        out_shape.append(jax.ShapeDtypeStruct((w_vt.shape[0], N_ROWS), BF16))
        out_specs.append(pl.BlockSpec((w_vt.shape[0], TM), lambda i: (0, i)))
    return pl.pallas_call(
        functools.partial(_proj_kernel, segs=segs, has_rope=has_rope, has_vt=w_vt is not None, split_in=split_in),
        out_shape=out_shape,
        grid=(N_ROWS // TM,),
        in_specs=in_specs,
        out_specs=out_specs,
        compiler_params=_cparams(("arbitrary",)),
        name="ada_proj",
    )(*args)


def _head_query(q_ref, rows, head, kv_half):
    pair, half = divmod(head, 2)
    q = q_ref[rows, pair * LANES:(pair + 1) * LANES]
    if half != kv_half:
        q = pltpu.roll(q, HEAD_DIM, 1)
    lane = lax.broadcasted_iota(jnp.int32, (1, LANES), 1)
    keep = (lane < HEAD_DIM) if kv_half == 0 else (lane >= HEAD_DIM)
    return jnp.where(keep, q, jnp.zeros_like(q))


def _attention_units(units, width=1):
    groups = [units[i:i + width] for i in range(0, len(units), width)]
    prev = []
    for idx in range(len(groups) + 1):
        cur = groups[idx] if idx < len(groups) else []
        qs = [u[0]() for u in cur]
        cur_s = [[] for _ in cur]
        cur_m8 = [None for _ in cur]
        acc = [[jnp.zeros((8, TQ), F32), jnp.zeros((HEAD_DIM, TQ), F32)] for _ in prev]
        n_steps = max([len(u[1]) for u in cur] + [len(p[0][1]) for p in prev] + [0])
        for c in range(n_steps):
            for i, u in enumerate(cur):
                if c < len(u[1]):
                    k_fn, _, bias_fn = u[1][c]
                    s = lax.dot_general(k_fn(), qs[i], _NT, preferred_element_type=F32)
                    if bias_fn is not None:
                        s = s + bias_fn()
                    cur_s[i].append(s)
                    mc = jnp.max(s.reshape(s.shape[0] // 8, 8, TQ), axis=0)
                    cur_m8[i] = mc if cur_m8[i] is None else jnp.maximum(cur_m8[i], mc)
            for i, (u, p_s, p_m) in enumerate(prev):
                if c < len(u[1]):
                    e = jnp.exp2(p_s[c] - p_m)
                    acc[i][0] = acc[i][0] + jnp.sum(e.reshape(e.shape[0] // 8, 8, TQ), axis=0)
                    acc[i][1] = acc[i][1] + jnp.dot(u[1][c][1](), e.astype(BF16), preferred_element_type=F32)
        for i, (u, p_s, p_m) in enumerate(prev):
            den = jnp.sum(acc[i][0], axis=0, keepdims=True)
            if u[2] is not None:
                den = den + jnp.exp2(u[2] - p_m)
            u[3](acc[i][1] / den)
        prev = []
        for i, u in enumerate(cur):
            m = jnp.max(cur_m8[i], axis=0, keepdims=True)
            if u[2] is not None:
                m = jnp.maximum(m, u[2])
            prev.append((u, cur_s[i], m))


def _store_head(o_ref, head, cols):
    def store(o):
        o_ref[head * HEAD_DIM:(head + 1) * HEAD_DIM, cols] = o.astype(o_ref.dtype)
    return store


def _chunks(k_ref, vt_ref, vrows, k0, n_keys, chunk, bias_fn=None):
    out = []
    for r0 in range(0, n_keys, chunk):
        n = min(chunk, n_keys - r0)
        out.append((
            functools.partial(lambda r0, n: k_ref[pl.ds(k0 + r0, n), :], r0, n),
            functools.partial(lambda r0, n: vt_ref[vrows, pl.ds(k0 + r0, n)], r0, n),
            None if bias_fn is None else functools.partial(bias_fn, r0, n)))
    return out


def _head_units(q_ref, rows, o_ref, cols, n_heads, group, kc_ref, vtc_ref, lat=None, bias_fn=None, sink_fn=None):
    units = []
    for head in range(n_heads):
        kv_half = (head // group) % 2
        vrows = slice(kv_half * HEAD_DIM, (kv_half + 1) * HEAD_DIM)
        chunks = _chunks(kc_ref, vtc_ref, vrows, 0, CTX_LEN, CTX_LEN)
        if lat is not None:
            k_ref, vt_ref, k0, n_keys, chunk = lat
            chunks += _chunks(k_ref, vt_ref, vrows, k0, n_keys, chunk,
                              None if bias_fn is None else functools.partial(bias_fn, head))
        units.append((functools.partial(_head_query, q_ref, rows, head, kv_half), chunks,
                      None if sink_fn is None else sink_fn(head), _store_head(o_ref, head, cols)))
    return units


def _na_row_index():
    rows = SEQ // GRID_W
    idx = []
    for q0, k0 in ((0, 0), (NA_Q_ROWS, 0), (rows - NA_Q_ROWS, rows - NA_K_ROWS)):
        qrow = q0 + np.arange(NA_Q_ROWS)
        krow = k0 + np.arange(NA_K_ROWS)
        r0 = np.clip(qrow - NA_WIN_H // 2, 0, rows - NA_WIN_H)
        row_in = (krow[None, :] >= r0[:, None]) & (krow[None, :] < r0[:, None] + NA_WIN_H)
        idx.append(np.where(row_in, krow[None, :] - qrow[:, None] + NA_WIN_H - 1, NA_MASKED))
    return np.stack(idx)


def _na_strip_plan():
    lo = NA_WIN_H - 1 - NA_WIN_H // 2
    interior = [NA_MASKED] * NA_Q_ROWS + list(range(lo + NA_WIN_H - 1, lo - 1, -1)) + [NA_MASKED] * (NA_Q_ROWS - 1)
    edge = list(range(NA_MASKED - 1, -1, -1))
    strips = [interior + [NA_MASKED], [NA_MASKED] + interior, edge + [NA_MASKED], [NA_MASKED] + edge]
    idx = _na_row_index()
    plan = []
    for case in range(idx.shape[0]):
        rows = []
        for kr in range(NA_K_ROWS):
            want = [int(v) for v in idx[case, :, kr]]
            if all(v == NA_MASKED for v in want):
                rows.append(None)
                continue
            hits = [(s, p) for s, seq in enumerate(strips) for p in range(0, len(seq) - NA_Q_ROWS + 1, 2)
                    if seq[p:p + NA_Q_ROWS] == want]
            rows.append(hits[0])
        plan.append(rows)
    return strips, plan


def _na_fill_bias(strip_ref, bias_s):
    _, plan = _na_strip_plan()
    for head in range(2):
        for case in range(len(plan)):
            for kr, hit in enumerate(plan[case]):
                rows = slice(kr * GRID_W, (kr + 1) * GRID_W)
                if hit is None:
                    bias_s[head, case, rows, :] = jnp.full((GRID_W, TQ), MASK_VALUE, F32)
                else:
                    strip, first = hit
                    bias_s[head, case, rows, :] = strip_ref[head, strip, :, first * GRID_W:first * GRID_W + TQ]


def _na_kernel(qc_ref, ql_ref, kc_ref, vtc_ref, kl_ref, vtl_ref, strip_ref, oc_ref, ol_ref, bias_ref):
    @pl.when(pl.program_id(1) == 0)
    def _():
        _na_fill_bias(strip_ref, bias_ref)

    all_rows = slice(None)
    units = _head_units(qc_ref, all_rows, oc_ref, all_rows, 2, 1, kc_ref, vtc_ref)
    n_tiles = SEQ // TQ
    for j in range(n_tiles):
        case = 0 if j == 0 else (2 if j == n_tiles - 1 else 1)
        k0 = min(max(NA_Q_ROWS * j - NA_WIN_H // 2, 0), SEQ // GRID_W - NA_K_ROWS) * GRID_W
        rows = slice(j * TQ, (j + 1) * TQ)
        units += _head_units(ql_ref, rows, ol_ref, rows, 2, 1, kc_ref, vtc_ref, (kl_ref, vtl_ref, k0, NA_KEYS, NA_KEYS),
                             functools.partial(lambda case, head, r0, n: bias_ref[head, case, r0:r0 + n, :], case))
    _attention_units(units)


def _na_attention(q, k, vt, strips):
    lat0 = N_CTX_ROWS // SEQ
    return pl.pallas_call(
        _na_kernel,
        out_shape=[jax.ShapeDtypeStruct((D_MODEL, N_CTX_ROWS), BF16),
                   jax.ShapeDtypeStruct((D_MODEL, N_LAT_ROWS), BF16)],
        grid=(NA_HEADS // 2, BATCH),
        in_specs=[
            pl.BlockSpec((CTX_LEN, LANES), lambda p, b: (b, p)),
            pl.BlockSpec((SEQ, LANES), lambda p, b: (lat0 + b, p)),
            pl.BlockSpec((CTX_LEN, LANES), lambda p, b: (b, p)),
            pl.BlockSpec((LANES, CTX_LEN), lambda p, b: (p, b)),
            pl.BlockSpec((SEQ, LANES), lambda p, b: (lat0 + b, p)),
            pl.BlockSpec((LANES, SEQ), lambda p, b: (p, lat0 + b)),
            pl.BlockSpec((None,) + strips.shape[1:], lambda p, b: (p, 0, 0, 0, 0)),
        ],
        out_specs=[pl.BlockSpec((LANES, CTX_LEN), lambda p, b: (p, b)),
                   pl.BlockSpec((LANES, SEQ), lambda p, b: (p, b))],
        scratch_shapes=[pltpu.VMEM((2, 3, NA_KEYS, TQ), F32)],
        compiler_params=_cparams(("arbitrary", "arbitrary")),
        name="na_attention",
    )(q, q, k, vt, k, vt, strips)


def _q_tile_index(b, j):
    n_qt = SEQ // TQ
    return jnp.where(j == 0, b, N_CTX_ROWS // TQ + b * n_qt + j - 1)


def _lat_out_index(b, j):
    return b * (SEQ // TQ) + jnp.maximum(j - 1, 0)


def _gqa_kernel(q_ref, kc_ref, vtc_ref, kl_ref, vtl_ref, oc_ref, ol_ref):
    j = pl.program_id(2)
    group = GQA_HEADS // GQA_KV_HEADS
    all_rows = slice(None)

    @pl.when(j == 0)
    def _():
        _attention_units(_head_units(q_ref, all_rows, oc_ref, all_rows, 2 * group, group, kc_ref, vtc_ref),
                         SHORT_UNIT_WIDTH)

    @pl.when(j > 0)
    def _():
        _attention_units(_head_units(q_ref, all_rows, ol_ref, all_rows, 2 * group, group, kc_ref, vtc_ref,
                                     (kl_ref, vtl_ref, 0, SEQ, GQA_KEY_CHUNK)))


def _gqa_attention(q, k, vt):
    lat0 = N_CTX_ROWS // SEQ
    n_qt = SEQ // TQ
    qw = 2 * (GQA_HEADS // GQA_KV_HEADS) * HEAD_DIM
    return pl.pallas_call(
        _gqa_kernel,
        out_shape=[jax.ShapeDtypeStruct((D_MODEL, N_CTX_ROWS), BF16),
                   jax.ShapeDtypeStruct((D_MODEL, N_LAT_ROWS), BF16)],
        grid=(BATCH, GQA_KV_HEADS // 2, n_qt + 1),
        in_specs=[
            pl.BlockSpec((TQ, qw), lambda b, p, j: (_q_tile_index(b, j), p)),
            pl.BlockSpec((CTX_LEN, LANES), lambda b, p, j: (b, p)),
            pl.BlockSpec((LANES, CTX_LEN), lambda b, p, j: (p, b)),
            pl.BlockSpec((SEQ, LANES), lambda b, p, j: (lat0 + b, p)),
            pl.BlockSpec((LANES, SEQ), lambda b, p, j: (p, lat0 + b)),
        ],
        out_specs=[pl.BlockSpec((qw, CTX_LEN), lambda b, p, j: (p, b)),
                   pl.BlockSpec((qw, TQ), lambda b, p, j: (p, _lat_out_index(b, j)))],
        compiler_params=_cparams(("arbitrary", "arbitrary", "arbitrary")),
        name="gqa_attention",
    )(q, k, vt, k, vt)


def _swa_kernel(sink_ref, q_ref, kc_ref, vtc_ref, kl_ref, vtl_ref, oc_ref, ol_ref):
    j = pl.program_id(1)
    group = SWA_HEADS // SWA_KV_HEADS
    all_rows = slice(None)
    sink_fn = lambda head: sink_ref[head] * LOG2E

    @pl.when(j == 0)
    def _():
        _attention_units(_head_units(q_ref, all_rows, oc_ref, all_rows, SWA_HEADS, group, kc_ref, vtc_ref,
                                     sink_fn=sink_fn), SHORT_UNIT_WIDTH)

    @pl.when(j > 0)
    def _():
        band = TQ + 2 * SWA_WINDOW
        q0 = (j - 1) * TQ
        k0 = pl.multiple_of(jnp.clip(q0 - SWA_WINDOW, 0, SEQ - band), LANES)
        kpos = k0 + lax.broadcasted_iota(jnp.int32, (band, 1), 0)
        qpos = q0 + lax.broadcasted_iota(jnp.int32, (1, TQ), 1)
        mask = jnp.where(jnp.abs(qpos - kpos) <= SWA_WINDOW, 0.0, MASK_VALUE).astype(F32)
        _attention_units(_head_units(q_ref, all_rows, ol_ref, all_rows, SWA_HEADS, group, kc_ref, vtc_ref,
                                     (kl_ref, vtl_ref, k0, band, band),
                                     bias_fn=lambda head, r0, n: mask[r0:r0 + n, :], sink_fn=sink_fn),
                         SHORT_UNIT_WIDTH)


def _swa_attention(q, k, vt, sinks):
    lat0 = N_CTX_ROWS // SEQ
    n_qt = SEQ // TQ
    return pl.pallas_call(
        _swa_kernel,
        out_shape=[jax.ShapeDtypeStruct((D_MODEL, N_CTX_ROWS), BF16),
                   jax.ShapeDtypeStruct((D_MODEL, N_LAT_ROWS), BF16)],
        grid=(BATCH, n_qt + 1),
        in_specs=[
            pl.BlockSpec(memory_space=pltpu.SMEM),
            pl.BlockSpec((TQ, D_MODEL), lambda b, j: (_q_tile_index(b, j), 0)),
            pl.BlockSpec((CTX_LEN, LANES), lambda b, j: (b, 0)),
            pl.BlockSpec((LANES, CTX_LEN), lambda b, j: (0, b)),
            pl.BlockSpec((SEQ, LANES), lambda b, j: (lat0 + b, 0)),
            pl.BlockSpec((LANES, SEQ), lambda b, j: (0, lat0 + b)),
        ],
        out_specs=[pl.BlockSpec((D_MODEL, CTX_LEN), lambda b, j: (0, b)),
                   pl.BlockSpec((D_MODEL, TQ), lambda b, j: (0, _lat_out_index(b, j)))],
        compiler_params=_cparams(("arbitrary", "arbitrary")),
        name="swa_attention",
    )(sinks, q, k, vt, k, vt)


def _scan_rows(a, b, h, reverse):
    sub = 8
    groups = a.shape[0] // sub
    a3 = a.reshape(groups, sub, D_MODEL)
    b3 = b.reshape(groups, sub, D_MODEL)
    row = lax.broadcasted_iota(jnp.int32, (1, sub, 1), 1)
    step = 1
    while step < sub:
        keep = (row < sub - step) if reverse else (row >= step)
        shift = sub - step if reverse else step
        a_prev = jnp.where(keep, pltpu.roll(a3, shift, 1), 1.0)
        b_prev = jnp.where(keep, pltpu.roll(b3, shift, 1), 0.0)
        b3 = b3 + a3 * b_prev
        a3 = a3 * a_prev
        step *= 2
    out = [None] * groups
    for g in (range(groups - 1, -1, -1) if reverse else range(groups)):
        hg = a3[g] * h + b3[g]
        out[g] = hg
        h = hg[0:1, :] if reverse else hg[sub - 1:sub, :]
    return jnp.concatenate(out, axis=0), h


def _lru_kernel(xc_ref, xl_ref, cw_ref, cb_ref, wbd_ref, ba_ref, bx_ref, lam_ref, o_ref):
    chunk = CTX_LEN
    n_lat = SEQ // chunk
    sp = jax.nn.softplus(-lam_ref[...])

    def conv(before, cur, after):
        ext = jnp.concatenate([before, cur, after], axis=0)
        acc = ext * cw_ref[2:3, :]
        acc = acc + pltpu.roll(ext, 2, 0) * cw_ref[0:1, :]
        acc = acc + pltpu.roll(ext, 1, 0) * cw_ref[1:2, :]
        acc = acc + pltpu.roll(ext, chunk + 15, 0) * cw_ref[3:4, :]
        return acc[8:8 + chunk, :] + cb_ref[...]

    def gates(xc, d):
        xb = xc.astype(BF16)
        pre = []
        for kind in range(2):
            cols = [jnp.dot(xb[:, 256 * c:256 * (c + 1)], wbd_ref[d, kind, c], preferred_element_type=F32)
                    for c in range(D_MODEL // 256)]
            pre.append(jnp.concatenate(cols, axis=-1))
        r = 0.5 * jnp.tanh(0.5 * (pre[0] + ba_ref[d:d + 1, :])) + 0.5
        ig = 0.5 * jnp.tanh(0.5 * (pre[1] + bx_ref[d:d + 1, :])) + 0.5
        log_a = (-LRU_C * r) * sp[d:d + 1, :]
        a = jnp.exp(log_a)
        return a, jnp.sqrt(-jnp.tanh(log_a) * (a * a + 1.0)) * (ig * xc)

    def lat_chunk(c, h, d, reverse):
        start = pl.multiple_of(c * chunk, chunk)
        before = jnp.where(c > 0, xl_ref[pl.ds(pl.multiple_of(jnp.maximum(start - 8, 0), 8), 8), :], 0.0)
        after = jnp.where(c < n_lat - 1, xl_ref[pl.ds(pl.multiple_of(jnp.minimum(start + chunk, SEQ - 8), 8), 8), :], 0.0)
        states, h = _scan_rows(*gates(conv(before, xl_ref[pl.ds(start, chunk), :], after), d), h, reverse)
        rows = pl.ds(start, chunk)
        if d == 0:
            o_ref[rows, :] = states
        else:
            o_ref[rows, :] = o_ref[rows, :] + states
        return h

    zeros = jnp.zeros((8, D_MODEL), F32)
    for d in range(2):
        reverse = d == 1
        _, h = _scan_rows(*gates(conv(zeros, xc_ref[...], zeros), d), jnp.zeros((1, D_MODEL), F32), reverse)
        lax.fori_loop(0, n_lat, lambda i, h: lat_chunk((n_lat - 1 - i) if reverse else i, h, d, reverse), h)


def _lru_scan(xr, conv_w, conv_b, wbd, b_a, b_x, lam):
    lat0 = N_CTX_ROWS // SEQ
    const2 = lambda b: (0, 0)
    return pl.pallas_call(
        _lru_kernel,
        out_shape=jax.ShapeDtypeStruct((N_LAT_ROWS, D_MODEL), F32),
        grid=(BATCH,),
        in_specs=[
            pl.BlockSpec((CTX_LEN, D_MODEL), lambda b: (b, 0)),
            pl.BlockSpec((SEQ, D_MODEL), lambda b: (lat0 + b, 0)),
            pl.BlockSpec((CONV_WIDTH, D_MODEL), const2),
            pl.BlockSpec((1, D_MODEL), const2),
            pl.BlockSpec(wbd.shape, lambda b: (0, 0, 0, 0, 0)),
            pl.BlockSpec((2, D_MODEL), const2),
            pl.BlockSpec((2, D_MODEL), const2),
            pl.BlockSpec((2, D_MODEL), const2),
        ],
        out_specs=pl.BlockSpec((SEQ, D_MODEL), lambda b: (b, 0)),
        compiler_params=_cparams(("arbitrary",)),
        name="rglru_scan",
    )(xr, xr, conv_w, conv_b.reshape(1, D_MODEL), wbd, b_a, b_x, lam)


def _ffn_kernel(*refs, lru, final, split_in):
    if split_in:
        h = _tile_rows(refs[0], refs[1], pl.program_id(0))
        refs = refs[1:]
    else:
        h = refs[0][...]
    mod_ref, g_ref, wo_ref, win_ref, wout_ref = refs[1:6]
    k = 6
    if lru:
        hs_ref, gl_ref = refs[k:k + 2]
        k += 2
        o_mix = (hs_ref[...] * gl_ref[...]).astype(BF16)
        y = jnp.dot(o_mix, wo_ref[...], preferred_element_type=F32)
    else:
        oc_ref, ol_ref = refs[k:k + 2]
        k += 2
        o_t = jnp.where(pl.program_id(0) < N_CTX_TILES, oc_ref[...], ol_ref[...])
        y = lax.dot_general(o_t, wo_ref[...], (((0,), (0,)), ((), ())), preferred_element_type=F32)
    if final:
        gf_ref = refs[k]
        k += 1
    out_ref = refs[k]
    h1 = h + mod_ref[2:3, :] * y
    xn = _ada_norm(h1, g_ref[...], mod_ref[3:4, :], mod_ref[4:5, :]).astype(BF16)
    acc = jnp.zeros(h1.shape, F32)
    for c0 in range(0, FFN_HIDDEN, FFN_CHUNK):
        a = jnp.dot(xn, win_ref[:, c0:c0 + FFN_CHUNK], preferred_element_type=F32)
        g = jnp.dot(xn, win_ref[:, FFN_HIDDEN + c0:FFN_HIDDEN + c0 + FFN_CHUNK], preferred_element_type=F32)
        u = ((a * jax.nn.sigmoid(a)) * g).astype(BF16)
        acc = acc + jnp.dot(u, wout_ref[c0:c0 + FFN_CHUNK, :], preferred_element_type=F32)
    h2 = h1 + mod_ref[5:6, :] * acc
    if final:
        ms = jnp.mean(h2 * h2, axis=-1, keepdims=True)
        h2 = (h2 * lax.rsqrt(ms + EPS)) * gf_ref[...]
    out_ref[...] = h2


def _mix_ffn(hx, mod_l, g_ffn, w_o, w_in, w_out, layer, mix_inputs, lru=False, final_g=None):
    final = final_g is not None
    t0 = N_CTX_TILES if final else 0
    n_tiles = N_ROWS // TM - t0
    resident = functools.partial(pl.BlockSpec, pipeline_mode=pl.Buffered(1))
    split_in = isinstance(hx, tuple)
    assert not (split_in and final)
    if split_in:
        in_specs, args = _tile_row_specs(*hx)
    else:
        in_specs, args = [pl.BlockSpec((TM, D_MODEL), lambda i: (i + t0, 0))], [hx]
    in_specs += [
        pl.BlockSpec((None, N_MOD, D_MODEL), lambda i: (_tile_mod_row(i + t0), 0, 0)),
        pl.BlockSpec((1, D_MODEL), lambda i: (0, 0)),
        resident(w_o.shape, lambda i: (0, 0)),
        resident((None,) + w_in.shape[1:], lambda i: (layer, 0, 0)),
        resident((None,) + w_out.shape[1:], lambda i: (layer, 0, 0)),
    ]
    args += [mod_l, g_ffn.reshape(1, D_MODEL), w_o, w_in, w_out]
    if lru:
        for m, off in mix_inputs:
            in_specs.append(pl.BlockSpec((TM, D_MODEL), lambda i, off=off: (i + off, 0)))
            args.append(m)
    else:
        o_ctx, o_lat = mix_inputs
        in_specs.append(pl.BlockSpec((D_MODEL, TM), lambda i: (0, jnp.minimum(i, N_CTX_TILES - 1))))
        in_specs.append(pl.BlockSpec((D_MODEL, TM), lambda i: (0, jnp.maximum(i - N_CTX_TILES, 0))))
        args += [o_ctx, o_lat]
    if final:
        in_specs.append(pl.BlockSpec((1, D_MODEL), lambda i: (0, 0)))
        args.append(final_g.reshape(1, D_MODEL))
    return pl.pallas_call(
        functools.partial(_ffn_kernel, lru=lru, final=final, split_in=split_in),
        out_shape=jax.ShapeDtypeStruct((n_tiles * TM, D_MODEL), F32),
        grid=(n_tiles,),
        in_specs=in_specs,
        out_specs=pl.BlockSpec((TM, D_MODEL), lambda i: (i, 0)),
        compiler_params=_cparams(("arbitrary",)),
        name="mix_ffn",
    )(*args)


def _rope_tables():
    t = jnp.arange(SEQ, dtype=jnp.int32)
    row = (t // GRID_W).astype(F32)
    col = (t % GRID_W).astype(F32)
    half = HEAD_DIM // 2
    inv_freq = 1.0 / (ROPE_THETA ** (jnp.arange(0, half, 2, dtype=F32) / half))
    d = jnp.arange(LANES) % HEAD_DIM
    freq = inv_freq[(d % half) % (half // 2)]
    pos = jnp.where((d < half)[None, :], row[:, None], col[:, None])
    ang = pos * freq[None, :]
    first = ((d % half) < half // 2)[None, :]
    cos, sin = jnp.cos(ang), jnp.sin(ang)
    ident = jnp.ones((TM, LANES), F32)
    zero = jnp.zeros((TM, LANES), F32)
    return jnp.concatenate([ident, cos]), jnp.concatenate([zero, jnp.where(first, -sin, sin)])


def _na_bias_strips(rpb):
    n_drow = NA_MASKED
    n_dcol = 2 * NA_WIN_W - 1
    span = 2 * GRID_W
    left = GRID_W - NA_WIN_W
    w = jnp.pad(rpb.astype(F32), ((0, 0), (0, 0), (left, span - left - n_dcol)))
    flat = jnp.tile(w, (1, 1, GRID_W))[..., GRID_W - 1:GRID_W - 1 + GRID_W * (span - 1)]
    toep = flat.reshape(NA_HEADS, n_drow, GRID_W, span - 1)[..., :GRID_W]
    qcol = np.arange(GRID_W)
    col_start = np.clip(qcol - NA_WIN_W // 2, 0, GRID_W - NA_WIN_W)
    col_in = (qcol[None, :] >= col_start[:, None]) & (qcol[None, :] < col_start[:, None] + NA_WIN_W)
    blocks = jnp.where(col_in[None, None], toep * LOG2E, MASK_VALUE)
    blocks = jnp.concatenate([blocks, jnp.full((NA_HEADS, 1, GRID_W, GRID_W), MASK_VALUE, F32)], axis=1)
    blocks_t = blocks.transpose(0, 1, 3, 2)
    seqs, _ = _na_strip_plan()
    order = np.asarray(seqs, np.int32)
    strips = jnp.take(blocks_t, jnp.asarray(order.reshape(-1)), axis=1)
    strips = strips.reshape(NA_HEADS, order.shape[0], order.shape[1], GRID_W, GRID_W).transpose(0, 1, 3, 2, 4)
    return strips.reshape(NA_HEADS // 2, 2, order.shape[0], GRID_W, order.shape[1] * GRID_W)


def _block_diag_256(w):
    per = 256 // LRU_BLOCK_DIM
    w4 = w.reshape(LRU_BLOCKS // per, per, LRU_BLOCK_DIM, LRU_BLOCK_DIM)
    eye = jnp.eye(per, dtype=w.dtype)
    return jnp.einsum("cide,ij->cidje", w4, eye).reshape(LRU_BLOCKS // per, 256, 256)


def kernel(x, c, ctx, c_ctx, ada_w, ada_b, norm_mix, norm_ffn, norm_final, ffn_w_in, ffn_w_out, na_w_qkv, na_rpb, na_w_o, gqa_w_qkv, gqa_q_gain, gqa_k_gain, gqa_w_o, swa_w_qkv, swa_sinks, swa_w_o, lru_w_in, lru_conv_w, lru_conv_b, lru_w_a, lru_b_a, lru_w_x, lru_b_x, lru_lam, lru_w_out):
    assert x.shape == (BATCH, SEQ, D_MODEL) and ctx.shape == (BATCH, CTX_LEN, D_MODEL)
    hx = (ctx.reshape(N_CTX_ROWS, D_MODEL), x.reshape(N_LAT_ROWS, D_MODEL))
    c_all = jnp.zeros((MOD_ROWS, D_MODEL), F32).at[:BATCH].set(c).at[CTX_MOD_ROW].set(c_ctx)
    mod = _modulation(c_all, ada_w, ada_b).reshape(DEPTH, MOD_ROWS, N_MOD, D_MODEL)
    rope_tabs = _rope_tables()
    q_scale = HEAD_DIM ** -0.5 * LOG2E
    lat_tile0 = N_CTX_TILES
    w_in, w_out = ffn_w_in.astype(BF16), ffn_w_out.astype(BF16)

    def qkv_split(w, n_q, n_kv):
        wq_k = w[:, :(n_q + n_kv) * HEAD_DIM].astype(BF16)
        return wq_k, w[:, (n_q + n_kv) * HEAD_DIM:].T.astype(BF16)

    w_qk, w_vt = qkv_split(na_w_qkv[0], NA_HEADS, NA_HEADS)
    q, k, vt = _project(hx, mod[0], norm_mix[0], w_qk,
                        ((D_MODEL, "plain", None, q_scale), (D_MODEL, "plain", None, 1.0)), (BF16, BF16), w_vt=w_vt)
    o = _na_attention(q, k, vt, _na_bias_strips(na_rpb[0]))
    hx = _mix_ffn(hx, mod[0], norm_ffn[0], na_w_o[0].astype(BF16), w_in, w_out, 0, o)

    kvw = GQA_KV_HEADS * HEAD_DIM
    gains = jnp.stack([jnp.tile(g_, 2) for g_ in (gqa_q_gain[0], gqa_k_gain[0],
                                                  _swap_rotary_halves(gqa_q_gain[0]), _swap_rotary_halves(gqa_k_gain[0]))])
    w_qk, w_vt = qkv_split(gqa_w_qkv[0], GQA_HEADS, GQA_KV_HEADS)
    q, k, vt = _project(hx, mod[1], norm_mix[1], w_qk,
                        ((D_MODEL, "rope", 0, q_scale), (kvw, "rope", 1, 1.0)), (BF16, BF16),
                        rope_tabs, gains, w_vt=w_vt)
    o = _gqa_attention(q, k, vt)
    hx = _mix_ffn(hx, mod[1], norm_ffn[1], gqa_w_o[0].astype(BF16), w_in, w_out, 1, o)

    kvw = SWA_KV_HEADS * HEAD_DIM
    w_qk, w_vt = qkv_split(swa_w_qkv[0], SWA_HEADS, SWA_KV_HEADS)
    q, k, vt = _project(hx, mod[2], norm_mix[2], w_qk,
                        ((D_MODEL, "rope", None, q_scale), (kvw, "rope", None, 1.0)), (BF16, BF16),
                        rope_tabs, jnp.ones((4, LANES), F32), w_vt=w_vt)
    o = _swa_attention(q, k, vt, swa_sinks[0])
    hx = _mix_ffn(hx, mod[2], norm_ffn[2], swa_w_o[0].astype(BF16), w_in, w_out, 2, o)

    xr, gl = _project(hx, mod[3], norm_mix[3], lru_w_in[0].astype(BF16),
                      ((D_MODEL, "plain", None, 1.0), (D_MODEL, "gelu", None, 1.0)), (F32, F32))
    wbd = jnp.stack([jnp.stack([_block_diag_256(lru_w_a[0, d]), _block_diag_256(lru_w_x[0, d])])
                     for d in range(2)]).astype(BF16)
    hs = _lru_scan(xr, lru_conv_w[0], lru_conv_b[0], wbd, lru_b_a[0], lru_b_x[0], lru_lam[0])
    out = _mix_ffn(hx, mod[3], norm_ffn[3], lru_w_out[0].astype(BF16), w_in, w_out, 3,
                   [(hs, 0), (gl, lat_tile0)], lru=True, final_g=norm_final)
    return out.reshape(BATCH, SEQ, D_MODEL)
```

```python
import functools

import jax
import jax.numpy as jnp
import numpy as np
from jax import lax
from jax.experimental import pallas as pl
from jax.experimental.pallas import tpu as pltpu

F32 = jnp.float32
BF16 = jnp.bfloat16

D_MODEL = 1024
BATCH = 8
SEQ = 2048
DEPTH = 4
GRID_W = 64
CTX_LEN = 256
HEAD_DIM = 64
ROPE_THETA = 10000.0
NA_HEADS = 16
NA_WIN_H = 8
NA_WIN_W = 16
GQA_HEADS = 16
GQA_KV_HEADS = 4
SWA_HEADS = 16
SWA_KV_HEADS = 2
SWA_WINDOW = 128
LRU_BLOCKS = 16
LRU_BLOCK_DIM = D_MODEL // LRU_BLOCKS
CONV_WIDTH = 4
LRU_C = 8.0
FFN_HIDDEN = 2816
N_MOD = 6
EPS = 1e-6
MASK_VALUE = -1e30
LOG2E = 1.4426950408889634

N_CTX_ROWS = BATCH * CTX_LEN
N_LAT_ROWS = BATCH * SEQ
N_ROWS = N_CTX_ROWS + N_LAT_ROWS
MOD_ROWS = 16
CTX_MOD_ROW = BATCH

LANES = 128
TM = 512
TQ = 256
FFN_CHUNK = 256
MOD_K_TILE = 256
GQA_KEY_CHUNK = 256
SHORT_UNIT_WIDTH = 2
VMEM_LIMIT = 56 * 1024 * 1024

N_CTX_TILES = N_CTX_ROWS // TM
TILES_PER_BATCH = SEQ // TM
NA_Q_ROWS = TQ // GRID_W
NA_K_ROWS = 12
NA_KEYS = NA_K_ROWS * GRID_W
NA_MASKED = 2 * NA_WIN_H - 1

_NT = (((1,), (1,)), ((), ()))


def _cparams(sem):
    return pltpu.CompilerParams(dimension_semantics=sem, vmem_limit_bytes=VMEM_LIMIT)


def _tile_mod_row(i):
    return jnp.where(i < N_CTX_TILES, CTX_MOD_ROW, (i - N_CTX_TILES) // TILES_PER_BATCH)


def _tile_pos_block(i):
    return jnp.where(i < N_CTX_TILES, 0, 1 + (i - N_CTX_TILES) % TILES_PER_BATCH)


def _ada_norm(x, g, shift, scale):
    ms = jnp.mean(x * x, axis=-1, keepdims=True)
    y = x * lax.rsqrt(ms + EPS)
    return (y * g) * (1.0 + scale) + shift


def _mod_kernel(c_ref, w_ref, b_ref, o_ref):
    kk = pl.program_id(1)
    c = c_ref[...]
    sc = (c * jax.nn.sigmoid(c)).astype(BF16)
    part = jnp.dot(sc, w_ref[0].astype(BF16), preferred_element_type=F32)

    @pl.when(kk == 0)
    def _():
        o_ref[0] = part + b_ref[0]

    @pl.when(kk > 0)
    def _():
        o_ref[0] = o_ref[0] + part


def _modulation(c_all, ada_w, ada_b):
    tk = MOD_K_TILE
    n = N_MOD * D_MODEL
    return pl.pallas_call(
        _mod_kernel,
        out_shape=jax.ShapeDtypeStruct((DEPTH, MOD_ROWS, n), F32),
        grid=(DEPTH, D_MODEL // tk),
        in_specs=[
            pl.BlockSpec((MOD_ROWS, tk), lambda l, k: (0, k)),
            pl.BlockSpec((1, tk, n), lambda l, k: (l, k, 0)),
            pl.BlockSpec((1, 1, n), lambda l, k: (l, 0, 0)),
        ],
        out_specs=pl.BlockSpec((1, MOD_ROWS, n), lambda l, k: (l, 0, 0)),
        compiler_params=_cparams(("arbitrary", "arbitrary")),
        name="modulation",
    )(c_all, ada_w, ada_b.reshape(DEPTH, 1, n))


def _tile_rows(ctx_ref, lat_ref, tile):
    return jnp.where(tile < N_CTX_TILES, ctx_ref[...], lat_ref[...])


def _tile_row_specs(ctx_rows, lat_rows, t0=0):
    return [pl.BlockSpec((TM, D_MODEL), lambda i: (jnp.minimum(i + t0, N_CTX_TILES - 1), 0)),
            pl.BlockSpec((TM, D_MODEL), lambda i: (jnp.maximum(i + t0 - N_CTX_TILES, 0), 0))], [ctx_rows, lat_rows]


def _proj_kernel(*refs, segs, has_rope, has_vt, split_in):
    if split_in:
        x = _tile_rows(refs[0], refs[1], pl.program_id(0))
        refs = refs[1:]
    else:
        x = refs[0][...]
    mod_ref, g_ref, w_ref = refs[1:4]
    k = 4
    if has_rope:
        wsw_ref, cos_ref, sin_ref, gain_ref = refs[k:k + 4]
        k += 4
    if has_vt:
        wvt_ref = refs[k]
        k += 1
    out_refs = refs[k:]
    xn = _ada_norm(x, g_ref[...], mod_ref[0:1, :], mod_ref[1:2, :]).astype(BF16)
    lane = lax.broadcasted_iota(jnp.int32, (1, LANES), 1)
    lo = lane < HEAD_DIM
    col = 0
    for seg, o_ref in zip(segs, out_refs):
        width, kind, norm_idx, scale = seg
        chunk = 256
        for c0 in range(0, width, chunk):
            cw = min(chunk, width - c0)
            y = jnp.dot(xn, w_ref[:, col + c0:col + c0 + cw], preferred_element_type=F32)
            if kind == "gelu":
                y = jax.nn.gelu(y, approximate=True)
            elif kind == "rope":
                ysw = jnp.dot(xn, wsw_ref[:, col + c0:col + c0 + cw], preferred_element_type=F32)
                cos_t, sin_t = cos_ref[...], sin_ref[...]
                if norm_idx is not None:
                    cos_t = cos_t * gain_ref[norm_idx:norm_idx + 1, :]
                    sin_t = sin_t * gain_ref[2 + norm_idx:3 + norm_idx, :]
                parts = []
                for b0 in range(0, cw, LANES):
                    blk = y[:, b0:b0 + LANES]
                    rot = blk * cos_t + ysw[:, b0:b0 + LANES] * sin_t
                    if norm_idx is not None:
                        ss = blk * blk
                        s_lo = jnp.sum(jnp.where(lo, ss, 0.0), axis=-1, keepdims=True)
                        s_hi = jnp.sum(jnp.where(lo, 0.0, ss), axis=-1, keepdims=True)
                        ms = jnp.where(lo, s_lo, s_hi) * (1.0 / HEAD_DIM)
                        rot = rot * lax.rsqrt(ms + EPS)
                    parts.append(rot)
                y = parts[0] if len(parts) == 1 else jnp.concatenate(parts, axis=-1)
            if scale != 1.0:
                y = y * scale
            o_ref[:, c0:c0 + cw] = y.astype(o_ref.dtype)
        col += width
    if has_vt:
        vt_ref = out_refs[len(segs)]
        for r0 in range(0, wvt_ref.shape[0], 256):
            rw = min(256, wvt_ref.shape[0] - r0)
            vt = lax.dot_general(wvt_ref[r0:r0 + rw, :], xn, _NT, preferred_element_type=F32)
            vt_ref[r0:r0 + rw, :] = vt.astype(vt_ref.dtype)


def _swap_rotary_halves(a):
    quarter = HEAD_DIM // 4
    return a.reshape(a.shape[:-1] + (-1, 2, quarter))[..., ::-1, :].reshape(a.shape)


def _project(hx, mod_l, g, w, segs, out_dtypes, rope_tabs=None, gains=None, w_vt=None):
    n_tot = sum(s[0] for s in segs)
    resident = functools.partial(pl.BlockSpec, pipeline_mode=pl.Buffered(1))
    split_in = isinstance(hx, tuple)
    if split_in:
        in_specs, args = _tile_row_specs(*hx)
    else:
        in_specs, args = [pl.BlockSpec((TM, D_MODEL), lambda i: (i, 0))], [hx]
    in_specs += [
        pl.BlockSpec((None, N_MOD, D_MODEL), lambda i: (_tile_mod_row(i), 0, 0)),
        pl.BlockSpec((1, D_MODEL), lambda i: (0, 0)),
        resident((D_MODEL, n_tot), lambda i: (0, 0)),
    ]
    args += [mod_l, g.reshape(1, D_MODEL), w]
    has_rope = rope_tabs is not None
    if has_rope:
        in_specs.append(resident((D_MODEL, n_tot), lambda i: (0, 0)))
        args.append(_swap_rotary_halves(w))
        for t in rope_tabs:
            in_specs.append(pl.BlockSpec((TM, LANES), lambda i: (_tile_pos_block(i), 0)))
            args.append(t)
        in_specs.append(pl.BlockSpec(gains.shape, lambda i: (0, 0)))
        args.append(gains)
    out_shape = [jax.ShapeDtypeStruct((N_ROWS, s[0]), dt) for s, dt in zip(segs, out_dtypes)]
    out_specs = [pl.BlockSpec((TM, s[0]), lambda i: (i, 0)) for s in segs]
    if w_vt is not None:
        in_specs.append(resident(w_vt.shape, lambda i: (0, 0)))
        args.append(w_vt)
        out_shape.append(jax.ShapeDtypeStruct((w_vt.shape[0], N_ROWS), BF16))
        out_specs.append(pl.BlockSpec((w_vt.shape[0], TM), lambda i: (0, i)))
    return pl.pallas_call(
        functools.partial(_proj_kernel, segs=segs, has_rope=has_rope, has_vt=w_vt is not None, split_in=split_in),
        out_shape=out_shape,
        grid=(N_ROWS // TM,),
        in_specs=in_specs,
        out_specs=out_specs,
        compiler_params=_cparams(("arbitrary",)),
        name="ada_proj",
    )(*args)


def _head_query(q_ref, rows, head, kv_half):
    pair, half = divmod(head, 2)
    q = q_ref[rows, pair * LANES:(pair + 1) * LANES]
    if half != kv_half:
        q = pltpu.roll(q, HEAD_DIM, 1)
    lane = lax.broadcasted_iota(jnp.int32, (1, LANES), 1)
    keep = (lane < HEAD_DIM) if kv_half == 0 else (lane >= HEAD_DIM)
    return jnp.where(keep, q, jnp.zeros_like(q))


def _attention_units(units, width=1):
    groups = [units[i:i + width] for i in range(0, len(units), width)]
    st1, st2 = [], []
    for idx in range(len(groups) + 2):
        cur = groups[idx] if idx < len(groups) else []
        qs = [u[0]() for u in cur]
        cur_s = [[] for _ in cur]
        cur_m8 = [None for _ in cur]
        new_e = [[] for _ in st1]
        new_d8 = [jnp.zeros((8, TQ), F32) for _ in st1]
        acc = [jnp.zeros((HEAD_DIM, TQ), F32) for _ in st2]
        n_steps = max([len(u[1]) for u in cur] + [len(p[0][1]) for p in st1] + [len(p[0][1]) for p in st2] + [0])
        for c in range(n_steps):
            for i, u in enumerate(cur):
                if c < len(u[1]):
                    k_fn, _, bias_fn = u[1][c]
                    s = lax.dot_general(k_fn(), qs[i], _NT, preferred_element_type=F32)
                    if bias_fn is not None:
                        s = s + bias_fn()
                    cur_s[i].append(s)
                    mc = jnp.max(s.reshape(s.shape[0] // 8, 8, TQ), axis=0)
                    cur_m8[i] = mc if cur_m8[i] is None else jnp.maximum(cur_m8[i], mc)
            for i, (u, p_s, p_m) in enumerate(st1):
                if c < len(u[1]):
                    e = jnp.exp2(p_s[c] - p_m)
                    new_d8[i] = new_d8[i] + jnp.sum(e.reshape(e.shape[0] // 8, 8, TQ), axis=0)
                    new_e[i].append(e.astype(BF16))
            for i, (u, p_e, p_d8, p_m) in enumerate(st2):
                if c < len(u[1]):
                    acc[i] = acc[i] + jnp.dot(u[1][c][1](), p_e[c], preferred_element_type=F32)
        for i, (u, p_e, p_d8, p_m) in enumerate(st2):
            den = jnp.sum(p_d8, axis=0, keepdims=True)
            if u[2] is not None:
                den = den + jnp.exp2(u[2] - p_m)
            u[3](acc[i] / den)
        st2 = [(u, new_e[i], new_d8[i], p_m) for i, (u, p_s, p_m) in enumerate(st1)]
        st1 = []
        for i, u in enumerate(cur):
            m = jnp.max(cur_m8[i], axis=0, keepdims=True)
            if u[2] is not None:
                m = jnp.maximum(m, u[2])
            st1.append((u, cur_s[i], m))


def _store_head(o_ref, head, cols):
    def store(o):
        o_ref[head * HEAD_DIM:(head + 1) * HEAD_DIM, cols] = o.astype(o_ref.dtype)
    return store


def _chunks(k_ref, vt_ref, vrows, k0, n_keys, chunk, bias_fn=None):
    out = []
    for r0 in range(0, n_keys, chunk):
        n = min(chunk, n_keys - r0)
        out.append((
            functools.partial(lambda r0, n: k_ref[pl.ds(k0 + r0, n), :], r0, n),
            functools.partial(lambda r0, n: vt_ref[vrows, pl.ds(k0 + r0, n)], r0, n),
            None if bias_fn is None else functools.partial(bias_fn, r0, n)))
    return out


def _head_units(q_ref, rows, o_ref, cols, n_heads, group, kc_ref, vtc_ref, lat=None, bias_fn=None, sink_fn=None):
    units = []
    for head in range(n_heads):
        kv_half = (head // group) % 2
        vrows = slice(kv_half * HEAD_DIM, (kv_half + 1) * HEAD_DIM)
        chunks = _chunks(kc_ref, vtc_ref, vrows, 0, CTX_LEN, CTX_LEN)
        if lat is not None:
            k_ref, vt_ref, k0, n_keys, chunk = lat
            chunks += _chunks(k_ref, vt_ref, vrows, k0, n_keys, chunk,
                              None if bias_fn is None else functools.partial(bias_fn, head))
        units.append((functools.partial(_head_query, q_ref, rows, head, kv_half), chunks,
                      None if sink_fn is None else sink_fn(head), _store_head(o_ref, head, cols)))
    return units


def _na_row_index():
    rows = SEQ // GRID_W
    idx = []
    for q0, k0 in ((0, 0), (NA_Q_ROWS, 0), (rows - NA_Q_ROWS, rows - NA_K_ROWS)):
        qrow = q0 + np.arange(NA_Q_ROWS)
        krow = k0 + np.arange(NA_K_ROWS)
        r0 = np.clip(qrow - NA_WIN_H // 2, 0, rows - NA_WIN_H)
        row_in = (krow[None, :] >= r0[:, None]) & (krow[None, :] < r0[:, None] + NA_WIN_H)
        idx.append(np.where(row_in, krow[None, :] - qrow[:, None] + NA_WIN_H - 1, NA_MASKED))
    return np.stack(idx)


def _na_strip_plan():
    lo = NA_WIN_H - 1 - NA_WIN_H // 2
    interior = [NA_MASKED] * NA_Q_ROWS + list(range(lo + NA_WIN_H - 1, lo - 1, -1)) + [NA_MASKED] * (NA_Q_ROWS - 1)
    edge = list(range(NA_MASKED - 1, -1, -1))
    strips = [interior + [NA_MASKED], [NA_MASKED] + interior, edge + [NA_MASKED], [NA_MASKED] + edge]
    idx = _na_row_index()
    plan = []
    for case in range(idx.shape[0]):
        rows = []
        for kr in range(NA_K_ROWS):
            want = [int(v) for v in idx[case, :, kr]]
            if all(v == NA_MASKED for v in want):
                rows.append(None)
                continue
            hits = [(s, p) for s, seq in enumerate(strips) for p in range(0, len(seq) - NA_Q_ROWS + 1, 2)
                    if seq[p:p + NA_Q_ROWS] == want]
            rows.append(hits[0])
        plan.append(rows)
    return strips, plan


def _na_fill_bias(strip_ref, bias_s):
    _, plan = _na_strip_plan()
    for head in range(2):
        for case in range(len(plan)):
            for kr, hit in enumerate(plan[case]):
                rows = slice(kr * GRID_W, (kr + 1) * GRID_W)
                if hit is None:
                    bias_s[head, case, rows, :] = jnp.full((GRID_W, TQ), MASK_VALUE, F32)
                else:
                    strip, first = hit
                    bias_s[head, case, rows, :] = strip_ref[head, strip, :, first * GRID_W:first * GRID_W + TQ]


def _na_kernel(qc_ref, ql_ref, kc_ref, vtc_ref, kl_ref, vtl_ref, strip_ref, oc_ref, ol_ref, bias_ref):
    @pl.when(pl.program_id(1) == 0)
    def _():
        _na_fill_bias(strip_ref, bias_ref)

    all_rows = slice(None)
    units = _head_units(qc_ref, all_rows, oc_ref, all_rows, 2, 1, kc_ref, vtc_ref)
    n_tiles = SEQ // TQ
    for j in range(n_tiles):
        case = 0 if j == 0 else (2 if j == n_tiles - 1 else 1)
        k0 = min(max(NA_Q_ROWS * j - NA_WIN_H // 2, 0), SEQ // GRID_W - NA_K_ROWS) * GRID_W
        rows = slice(j * TQ, (j + 1) * TQ)
        units += _head_units(ql_ref, rows, ol_ref, rows, 2, 1, kc_ref, vtc_ref, (kl_ref, vtl_ref, k0, NA_KEYS, NA_KEYS),
                             functools.partial(lambda case, head, r0, n: bias_ref[head, case, r0:r0 + n, :], case))
    _attention_units(units)


def _na_attention(q, k, vt, strips):
    lat0 = N_CTX_ROWS // SEQ
    return pl.pallas_call(
        _na_kernel,
        out_shape=[jax.ShapeDtypeStruct((D_MODEL, N_CTX_ROWS), BF16),
                   jax.ShapeDtypeStruct((D_MODEL, N_LAT_ROWS), BF16)],
        grid=(NA_HEADS // 2, BATCH),
        in_specs=[
            pl.BlockSpec((CTX_LEN, LANES), lambda p, b: (b, p)),
            pl.BlockSpec((SEQ, LANES), lambda p, b: (lat0 + b, p)),
            pl.BlockSpec((CTX_LEN, LANES), lambda p, b: (b, p)),
            pl.BlockSpec((LANES, CTX_LEN), lambda p, b: (p, b)),
            pl.BlockSpec((SEQ, LANES), lambda p, b: (lat0 + b, p)),
            pl.BlockSpec((LANES, SEQ), lambda p, b: (p, lat0 + b)),
            pl.BlockSpec((None,) + strips.shape[1:], lambda p, b: (p, 0, 0, 0, 0)),
        ],
        out_specs=[pl.BlockSpec((LANES, CTX_LEN), lambda p, b: (p, b)),
                   pl.BlockSpec((LANES, SEQ), lambda p, b: (p, b))],
        scratch_shapes=[pltpu.VMEM((2, 3, NA_KEYS, TQ), F32)],
        compiler_params=_cparams(("arbitrary", "arbitrary")),
        name="na_attention",
    )(q, q, k, vt, k, vt, strips)


def _q_tile_index(b, j):
    n_qt = SEQ // TQ
    return jnp.where(j == 0, b, N_CTX_ROWS // TQ + b * n_qt + j - 1)


def _lat_out_index(b, j):
    return b * (SEQ // TQ) + jnp.maximum(j - 1, 0)


def _gqa_kernel(q_ref, kc_ref, vtc_ref, kl_ref, vtl_ref, oc_ref, ol_ref):
    j = pl.program_id(2)
    group = GQA_HEADS // GQA_KV_HEADS
    all_rows = slice(None)

    @pl.when(j == 0)
    def _():
        _attention_units(_head_units(q_ref, all_rows, oc_ref, all_rows, 2 * group, group, kc_ref, vtc_ref),
                         SHORT_UNIT_WIDTH)

    @pl.when(j > 0)
    def _():
        _attention_units(_head_units(q_ref, all_rows, ol_ref, all_rows, 2 * group, group, kc_ref, vtc_ref,
                                     (kl_ref, vtl_ref, 0, SEQ, GQA_KEY_CHUNK)))


def _gqa_attention(q, k, vt):
    lat0 = N_CTX_ROWS // SEQ
    n_qt = SEQ // TQ
    qw = 2 * (GQA_HEADS // GQA_KV_HEADS) * HEAD_DIM
    return pl.pallas_call(
        _gqa_kernel,
        out_shape=[jax.ShapeDtypeStruct((D_MODEL, N_CTX_ROWS), BF16),
                   jax.ShapeDtypeStruct((D_MODEL, N_LAT_ROWS), BF16)],
        grid=(BATCH, GQA_KV_HEADS // 2, n_qt + 1),
        in_specs=[
            pl.BlockSpec((TQ, qw), lambda b, p, j: (_q_tile_index(b, j), p)),
            pl.BlockSpec((CTX_LEN, LANES), lambda b, p, j: (b, p)),
            pl.BlockSpec((LANES, CTX_LEN), lambda b, p, j: (p, b)),
            pl.BlockSpec((SEQ, LANES), lambda b, p, j: (lat0 + b, p)),
            pl.BlockSpec((LANES, SEQ), lambda b, p, j: (p, lat0 + b)),
        ],
        out_specs=[pl.BlockSpec((qw, CTX_LEN), lambda b, p, j: (p, b)),
                   pl.BlockSpec((qw, TQ), lambda b, p, j: (p, _lat_out_index(b, j)))],
        compiler_params=_cparams(("arbitrary", "arbitrary", "arbitrary")),
        name="gqa_attention",
    )(q, k, vt, k, vt)


def _swa_kernel(sink_ref, q_ref, kc_ref, vtc_ref, kl_ref, vtl_ref, oc_ref, ol_ref):
    j = pl.program_id(1)
    group = SWA_HEADS // SWA_KV_HEADS
    all_rows = slice(None)
    sink_fn = lambda head: sink_ref[head] * LOG2E

    @pl.when(j == 0)
    def _():
        _attention_units(_head_units(q_ref, all_rows, oc_ref, all_rows, SWA_HEADS, group, kc_ref, vtc_ref,
                                     sink_fn=sink_fn), SHORT_UNIT_WIDTH)

    @pl.when(j > 0)
    def _():
        band = TQ + 2 * SWA_WINDOW
        q0 = (j - 1) * TQ
        k0 = pl.multiple_of(jnp.clip(q0 - SWA_WINDOW, 0, SEQ - band), LANES)
        kpos = k0 + lax.broadcasted_iota(jnp.int32, (band, 1), 0)
        qpos = q0 + lax.broadcasted_iota(jnp.int32, (1, TQ), 1)
        mask = jnp.where(jnp.abs(qpos - kpos) <= SWA_WINDOW, 0.0, MASK_VALUE).astype(F32)
        _attention_units(_head_units(q_ref, all_rows, ol_ref, all_rows, SWA_HEADS, group, kc_ref, vtc_ref,
                                     (kl_ref, vtl_ref, k0, band, band),
                                     bias_fn=lambda head, r0, n: mask[r0:r0 + n, :], sink_fn=sink_fn),
                         SHORT_UNIT_WIDTH)


def _swa_attention(q, k, vt, sinks):
    lat0 = N_CTX_ROWS // SEQ
    n_qt = SEQ // TQ
    return pl.pallas_call(
        _swa_kernel,
        out_shape=[jax.ShapeDtypeStruct((D_MODEL, N_CTX_ROWS), BF16),
                   jax.ShapeDtypeStruct((D_MODEL, N_LAT_ROWS), BF16)],
        grid=(BATCH, n_qt + 1),
        in_specs=[
            pl.BlockSpec(memory_space=pltpu.SMEM),
            pl.BlockSpec((TQ, D_MODEL), lambda b, j: (_q_tile_index(b, j), 0)),
            pl.BlockSpec((CTX_LEN, LANES), lambda b, j: (b, 0)),
            pl.BlockSpec((LANES, CTX_LEN), lambda b, j: (0, b)),
            pl.BlockSpec((SEQ, LANES), lambda b, j: (lat0 + b, 0)),
            pl.BlockSpec((LANES, SEQ), lambda b, j: (0, lat0 + b)),
        ],
        out_specs=[pl.BlockSpec((D_MODEL, CTX_LEN), lambda b, j: (0, b)),
                   pl.BlockSpec((D_MODEL, TQ), lambda b, j: (0, _lat_out_index(b, j)))],
        compiler_params=_cparams(("arbitrary", "arbitrary")),
        name="swa_attention",
    )(sinks, q, k, vt, k, vt)


def _scan_rows(a, b, h, reverse):
    sub = 8
    groups = a.shape[0] // sub
    a3 = a.reshape(groups, sub, D_MODEL)
    b3 = b.reshape(groups, sub, D_MODEL)
    row = lax.broadcasted_iota(jnp.int32, (1, sub, 1), 1)
    step = 1
    while step < sub:
        keep = (row < sub - step) if reverse else (row >= step)
        shift = sub - step if reverse else step
        a_prev = jnp.where(keep, pltpu.roll(a3, shift, 1), 1.0)
        b_prev = jnp.where(keep, pltpu.roll(b3, shift, 1), 0.0)
        b3 = b3 + a3 * b_prev
        a3 = a3 * a_prev
        step *= 2
    out = [None] * groups
    for g in (range(groups - 1, -1, -1) if reverse else range(groups)):
        hg = a3[g] * h + b3[g]
        out[g] = hg
        h = hg[0:1, :] if reverse else hg[sub - 1:sub, :]
    return jnp.concatenate(out, axis=0), h


def _lru_kernel(xc_ref, xl_ref, cw_ref, cb_ref, wbd_ref, ba_ref, bx_ref, lam_ref, o_ref, conv_s):
    chunk = CTX_LEN
    n_lat = SEQ // chunk
    sp = jax.nn.softplus(-lam_ref[...])

    def conv(before, cur, after):
        ext = jnp.concatenate([before, cur, after], axis=0)
        acc = ext * cw_ref[2:3, :]
        acc = acc + pltpu.roll(ext, 2, 0) * cw_ref[0:1, :]
        acc = acc + pltpu.roll(ext, 1, 0) * cw_ref[1:2, :]
        acc = acc + pltpu.roll(ext, chunk + 15, 0) * cw_ref[3:4, :]
        return acc[8:8 + chunk, :] + cb_ref[...]

    def gates(xc, d):
        xb = xc.astype(BF16)
        pre = []
        for kind in range(2):
            cols = [jnp.dot(xb[:, 256 * c:256 * (c + 1)], wbd_ref[d, kind, c], preferred_element_type=F32)
                    for c in range(D_MODEL // 256)]
            pre.append(jnp.concatenate(cols, axis=-1))
        r = 0.5 * jnp.tanh(0.5 * (pre[0] + ba_ref[d:d + 1, :])) + 0.5
        ig = 0.5 * jnp.tanh(0.5 * (pre[1] + bx_ref[d:d + 1, :])) + 0.5
        log_a = (-LRU_C * r) * sp[d:d + 1, :]
        a = jnp.exp(log_a)
        return a, jnp.sqrt(-jnp.tanh(log_a) * (a * a + 1.0)) * (ig * xc)

    def lat_chunk(c, h, d, reverse):
        start = pl.multiple_of(c * chunk, chunk)
        rows = pl.ds(start, chunk)
        kept = pl.ds(CTX_LEN + start, chunk)
        if d == 0:
            before = jnp.where(c > 0, xl_ref[pl.ds(pl.multiple_of(jnp.maximum(start - 8, 0), 8), 8), :], 0.0)
            after = jnp.where(c < n_lat - 1,
                              xl_ref[pl.ds(pl.multiple_of(jnp.minimum(start + chunk, SEQ - 8), 8), 8), :], 0.0)
            xc = conv(before, xl_ref[rows, :], after)
            conv_s[kept, :] = xc
        else:
            xc = conv_s[kept, :]
        states, h = _scan_rows(*gates(xc, d), h, reverse)
        if d == 0:
            o_ref[rows, :] = states
        else:
            o_ref[rows, :] = o_ref[rows, :] + states
        return h

    zeros = jnp.zeros((8, D_MODEL), F32)
    conv_s[0:CTX_LEN, :] = conv(zeros, xc_ref[...], zeros)
    for d in range(2):
        reverse = d == 1
        _, h = _scan_rows(*gates(conv_s[0:CTX_LEN, :], d), jnp.zeros((1, D_MODEL), F32), reverse)
        lax.fori_loop(0, n_lat, lambda i, h: lat_chunk((n_lat - 1 - i) if reverse else i, h, d, reverse), h)


def _lru_scan(xr, conv_w, conv_b, wbd, b_a, b_x, lam):
    lat0 = N_CTX_ROWS // SEQ
    const2 = lambda b: (0, 0)
    return pl.pallas_call(
        _lru_kernel,
        out_shape=jax.ShapeDtypeStruct((N_LAT_ROWS, D_MODEL), F32),
        grid=(BATCH,),
        in_specs=[
            pl.BlockSpec((CTX_LEN, D_MODEL), lambda b: (b, 0)),
            pl.BlockSpec((SEQ, D_MODEL), lambda b: (lat0 + b, 0)),
            pl.BlockSpec((CONV_WIDTH, D_MODEL), const2),
            pl.BlockSpec((1, D_MODEL), const2),
            pl.BlockSpec(wbd.shape, lambda b: (0, 0, 0, 0, 0)),
            pl.BlockSpec((2, D_MODEL), const2),
            pl.BlockSpec((2, D_MODEL), const2),
            pl.BlockSpec((2, D_MODEL), const2),
        ],
        out_specs=pl.BlockSpec((SEQ, D_MODEL), lambda b: (b, 0)),
        scratch_shapes=[pltpu.VMEM((CTX_LEN + SEQ, D_MODEL), F32)],
        compiler_params=_cparams(("arbitrary",)),
        name="rglru_scan",
    )(xr, xr, conv_w, conv_b.reshape(1, D_MODEL), wbd, b_a, b_x, lam)


def _ffn_kernel(*refs, lru, final, split_in):
    if split_in:
        h = _tile_rows(refs[0], refs[1], pl.program_id(0))
        refs = refs[1:]
    else:
        h = refs[0][...]
    mod_ref, g_ref, wo_ref, win_ref, wout_ref = refs[1:6]
    k = 6
    if lru:
        hs_ref, gl_ref = refs[k:k + 2]
        k += 2
        o_mix = (hs_ref[...] * gl_ref[...]).astype(BF16)
        y = jnp.dot(o_mix, wo_ref[...], preferred_element_type=F32)
    else:
        oc_ref, ol_ref = refs[k:k + 2]
        k += 2
        o_t = jnp.where(pl.program_id(0) < N_CTX_TILES, oc_ref[...], ol_ref[...])
        y = lax.dot_general(o_t, wo_ref[...], (((0,), (0,)), ((), ())), preferred_element_type=F32)
    if final:
        gf_ref = refs[k]
        k += 1
    out_ref = refs[k]
    h1 = h + mod_ref[2:3, :] * y
    xn = _ada_norm(h1, g_ref[...], mod_ref[3:4, :], mod_ref[4:5, :]).astype(BF16)
    acc = jnp.zeros(h1.shape, F32)
    for c0 in range(0, FFN_HIDDEN, FFN_CHUNK):
        a = jnp.dot(xn, win_ref[:, c0:c0 + FFN_CHUNK], preferred_element_type=F32)
        g = jnp.dot(xn, win_ref[:, FFN_HIDDEN + c0:FFN_HIDDEN + c0 + FFN_CHUNK], preferred_element_type=F32)
        u = ((a * jax.nn.sigmoid(a)) * g).astype(BF16)
        acc = acc + jnp.dot(u, wout_ref[c0:c0 + FFN_CHUNK, :], preferred_element_type=F32)
    h2 = h1 + mod_ref[5:6, :] * acc
    if final:
        ms = jnp.mean(h2 * h2, axis=-1, keepdims=True)
        h2 = (h2 * lax.rsqrt(ms + EPS)) * gf_ref[...]
    out_ref[...] = h2


def _mix_ffn(hx, mod_l, g_ffn, w_o, w_in, w_out, layer, mix_inputs, lru=False, final_g=None):
    final = final_g is not None
    t0 = N_CTX_TILES if final else 0
    n_tiles = N_ROWS // TM - t0
    resident = functools.partial(pl.BlockSpec, pipeline_mode=pl.Buffered(1))
    split_in = isinstance(hx, tuple)
    assert not (split_in and final)
    if split_in:
        in_specs, args = _tile_row_specs(*hx)
    else:
        in_specs, args = [pl.BlockSpec((TM, D_MODEL), lambda i: (i + t0, 0))], [hx]
    in_specs += [
        pl.BlockSpec((None, N_MOD, D_MODEL), lambda i: (_tile_mod_row(i + t0), 0, 0)),
        pl.BlockSpec((1, D_MODEL), lambda i: (0, 0)),
        resident(w_o.shape, lambda i: (0, 0)),
        resident((None,) + w_in.shape[1:], lambda i: (layer, 0, 0)),
        resident((None,) + w_out.shape[1:], lambda i: (layer, 0, 0)),
    ]
    args += [mod_l, g_ffn.reshape(1, D_MODEL), w_o, w_in, w_out]
    if lru:
        for m, off in mix_inputs:
            in_specs.append(pl.BlockSpec((TM, D_MODEL), lambda i, off=off: (i + off, 0)))
            args.append(m)
    else:
        o_ctx, o_lat = mix_inputs
        in_specs.append(pl.BlockSpec((D_MODEL, TM), lambda i: (0, jnp.minimum(i, N_CTX_TILES - 1))))
        in_specs.append(pl.BlockSpec((D_MODEL, TM), lambda i: (0, jnp.maximum(i - N_CTX_TILES, 0))))
        args += [o_ctx, o_lat]
    if final:
        in_specs.append(pl.BlockSpec((1, D_MODEL), lambda i: (0, 0)))
        args.append(final_g.reshape(1, D_MODEL))
    return pl.pallas_call(
        functools.partial(_ffn_kernel, lru=lru, final=final, split_in=split_in),
        out_shape=jax.ShapeDtypeStruct((n_tiles * TM, D_MODEL), F32),
        grid=(n_tiles,),
        in_specs=in_specs,
        out_specs=pl.BlockSpec((TM, D_MODEL), lambda i: (i, 0)),
        compiler_params=_cparams(("arbitrary",)),
        name="mix_ffn",
    )(*args)


def _rope_tables():
    t = jnp.arange(SEQ, dtype=jnp.int32)
    row = (t // GRID_W).astype(F32)
    col = (t % GRID_W).astype(F32)
    half = HEAD_DIM // 2
    inv_freq = 1.0 / (ROPE_THETA ** (jnp.arange(0, half, 2, dtype=F32) / half))
    d = jnp.arange(LANES) % HEAD_DIM
    freq = inv_freq[(d % half) % (half // 2)]
    pos = jnp.where((d < half)[None, :], row[:, None], col[:, None])
    ang = pos * freq[None, :]
    first = ((d % half) < half // 2)[None, :]
    cos, sin = jnp.cos(ang), jnp.sin(ang)
    ident = jnp.ones((TM, LANES), F32)
    zero = jnp.zeros((TM, LANES), F32)
    return jnp.concatenate([ident, cos]), jnp.concatenate([zero, jnp.where(first, -sin, sin)])


def _na_bias_strips(rpb):
    n_drow = NA_MASKED
    n_dcol = 2 * NA_WIN_W - 1
    span = 2 * GRID_W
    left = GRID_W - NA_WIN_W
    w = jnp.pad(rpb.astype(F32), ((0, 0), (0, 0), (left, span - left - n_dcol)))
    flat = jnp.tile(w, (1, 1, GRID_W))[..., GRID_W - 1:GRID_W - 1 + GRID_W * (span - 1)]
    toep = flat.reshape(NA_HEADS, n_drow, GRID_W, span - 1)[..., :GRID_W]
    qcol = np.arange(GRID_W)
    col_start = np.clip(qcol - NA_WIN_W // 2, 0, GRID_W - NA_WIN_W)
    col_in = (qcol[None, :] >= col_start[:, None]) & (qcol[None, :] < col_start[:, None] + NA_WIN_W)
    blocks = jnp.where(col_in[None, None], toep * LOG2E, MASK_VALUE)
    blocks = jnp.concatenate([blocks, jnp.full((NA_HEADS, 1, GRID_W, GRID_W), MASK_VALUE, F32)], axis=1)
    blocks_t = blocks.transpose(0, 1, 3, 2)
    seqs, _ = _na_strip_plan()
    order = np.asarray(seqs, np.int32)
    strips = jnp.take(blocks_t, jnp.asarray(order.reshape(-1)), axis=1)
    strips = strips.reshape(NA_HEADS, order.shape[0], order.shape[1], GRID_W, GRID_W).transpose(0, 1, 3, 2, 4)
    return strips.reshape(NA_HEADS // 2, 2, order.shape[0], GRID_W, order.shape[1] * GRID_W)


def _block_diag_256(w):
    per = 256 // LRU_BLOCK_DIM
    w4 = w.reshape(LRU_BLOCKS // per, per, LRU_BLOCK_DIM, LRU_BLOCK_DIM)
    eye = jnp.eye(per, dtype=w.dtype)
    return jnp.einsum("cide,ij->cidje", w4, eye).reshape(LRU_BLOCKS // per, 256, 256)


def kernel(x, c, ctx, c_ctx, ada_w, ada_b, norm_mix, norm_ffn, norm_final, ffn_w_in, ffn_w_out, na_w_qkv, na_rpb, na_w_o, gqa_w_qkv, gqa_q_gain, gqa_k_gain, gqa_w_o, swa_w_qkv, swa_sinks, swa_w_o, lru_w_in, lru_conv_w, lru_conv_b, lru_w_a, lru_b_a, lru_w_x, lru_b_x, lru_lam, lru_w_out):
    assert x.shape == (BATCH, SEQ, D_MODEL) and ctx.shape == (BATCH, CTX_LEN, D_MODEL)
    hx = (ctx.reshape(N_CTX_ROWS, D_MODEL), x.reshape(N_LAT_ROWS, D_MODEL))
    c_all = jnp.zeros((MOD_ROWS, D_MODEL), F32).at[:BATCH].set(c).at[CTX_MOD_ROW].set(c_ctx)
    mod = _modulation(c_all, ada_w, ada_b).reshape(DEPTH, MOD_ROWS, N_MOD, D_MODEL)
    rope_tabs = _rope_tables()
    q_scale = HEAD_DIM ** -0.5 * LOG2E
    lat_tile0 = N_CTX_TILES
    w_in, w_out = ffn_w_in.astype(BF16), ffn_w_out.astype(BF16)

    def qkv_split(w, n_q, n_kv):
        wq_k = w[:, :(n_q + n_kv) * HEAD_DIM].astype(BF16)
        return wq_k, w[:, (n_q + n_kv) * HEAD_DIM:].T.astype(BF16)

    w_qk, w_vt = qkv_split(na_w_qkv[0], NA_HEADS, NA_HEADS)
    q, k, vt = _project(hx, mod[0], norm_mix[0], w_qk,
                        ((D_MODEL, "plain", None, q_scale), (D_MODEL, "plain", None, 1.0)), (BF16, BF16), w_vt=w_vt)
    o = _na_attention(q, k, vt, _na_bias_strips(na_rpb[0]))
    hx = _mix_ffn(hx, mod[0], norm_ffn[0], na_w_o[0].astype(BF16), w_in, w_out, 0, o)

    kvw = GQA_KV_HEADS * HEAD_DIM
    gains = jnp.stack([jnp.tile(g_, 2) for g_ in (gqa_q_gain[0], gqa_k_gain[0],
                                                  _swap_rotary_halves(gqa_q_gain[0]), _swap_rotary_halves(gqa_k_gain[0]))])
    w_qk, w_vt = qkv_split(gqa_w_qkv[0], GQA_HEADS, GQA_KV_HEADS)
    q, k, vt = _project(hx, mod[1], norm_mix[1], w_qk,
                        ((D_MODEL, "rope", 0, q_scale), (kvw, "rope", 1, 1.0)), (BF16, BF16),
                        rope_tabs, gains, w_vt=w_vt)
    o = _gqa_attention(q, k, vt)
    hx = _mix_ffn(hx, mod[1], norm_ffn[1], gqa_w_o[0].astype(BF16), w_in, w_out, 1, o)

    kvw = SWA_KV_HEADS * HEAD_DIM
    w_qk, w_vt = qkv_split(swa_w_qkv[0], SWA_HEADS, SWA_KV_HEADS)
    q, k, vt = _project(hx, mod[2], norm_mix[2], w_qk,
                        ((D_MODEL, "rope", None, q_scale), (kvw, "rope", None, 1.0)), (BF16, BF16),
                        rope_tabs, jnp.ones((4, LANES), F32), w_vt=w_vt)
    o = _swa_attention(q, k, vt, swa_sinks[0])
    hx = _mix_ffn(hx, mod[2], norm_ffn[2], swa_w_o[0].astype(BF16), w_in, w_out, 2, o)

    xr, gl = _project(hx, mod[3], norm_mix[3], lru_w_in[0].astype(BF16),
                      ((D_MODEL, "plain", None, 1.0), (D_MODEL, "gelu", None, 1.0)), (F32, F32))
    wbd = jnp.stack([jnp.stack([_block_diag_256(lru_w_a[0, d]), _block_diag_256(lru_w_x[0, d])])
                     for d in range(2)]).astype(BF16)
    hs = _lru_scan(xr, lru_conv_w[0], lru_conv_b[0], wbd, lru_b_a[0], lru_b_x[0], lru_lam[0])
    out = _mix_ffn(hx, mod[3], norm_ffn[3], lru_w_out[0].astype(BF16), w_in, w_out, 3,
                   [(hs, 0), (gl, lat_tile0)], lru=True, final_g=norm_final)
    return out.reshape(BATCH, SEQ, D_MODEL)
```

```python
import functools

import jax
import jax.numpy as jnp
import numpy as np
from jax import lax
from jax.experimental import pallas as pl
from jax.experimental.pallas import tpu as pltpu

F32 = jnp.float32
BF16 = jnp.bfloat16

D_MODEL = 1024
BATCH = 8
SEQ = 2048
DEPTH = 4
GRID_W = 64
CTX_LEN = 256
HEAD_DIM = 64
ROPE_THETA = 10000.0
NA_HEADS = 16
NA_WIN_H = 8
NA_WIN_W = 16
GQA_HEADS = 16
GQA_KV_HEADS = 4
SWA_HEADS = 16
SWA_KV_HEADS = 2
SWA_WINDOW = 128
LRU_BLOCKS = 16
LRU_BLOCK_DIM = D_MODEL // LRU_BLOCKS
CONV_WIDTH = 4
LRU_C = 8.0
FFN_HIDDEN = 2816
N_MOD = 6
EPS = 1e-6
MASK_VALUE = -1e30
LOG2E = 1.4426950408889634

N_CTX_ROWS = BATCH * CTX_LEN
N_LAT_ROWS = BATCH * SEQ
N_ROWS = N_CTX_ROWS + N_LAT_ROWS
MOD_ROWS = 16
CTX_MOD_ROW = BATCH

LANES = 128
TM = 512
TQ = 256
FFN_CHUNK = 256
MOD_K_TILE = 256
GQA_KEY_CHUNK = 256
SHORT_UNIT_WIDTH = 2
VMEM_LIMIT = 56 * 1024 * 1024

N_CTX_TILES = N_CTX_ROWS // TM
TILES_PER_BATCH = SEQ // TM
NA_Q_ROWS = TQ // GRID_W
NA_K_ROWS = 12
NA_KEYS = NA_K_ROWS * GRID_W
NA_MASKED = 2 * NA_WIN_H - 1

_NT = (((1,), (1,)), ((), ()))


def _cparams(sem):
    return pltpu.CompilerParams(dimension_semantics=sem, vmem_limit_bytes=VMEM_LIMIT)


def _tile_mod_row(i):
    return jnp.where(i < N_CTX_TILES, CTX_MOD_ROW, (i - N_CTX_TILES) // TILES_PER_BATCH)


def _tile_pos_block(i):
    return jnp.where(i < N_CTX_TILES, 0, 1 + (i - N_CTX_TILES) % TILES_PER_BATCH)


def _ada_norm(x, g, shift, scale):
    ms = jnp.mean(x * x, axis=-1, keepdims=True)
    y = x * lax.rsqrt(ms + EPS)
    return (y * g) * (1.0 + scale) + shift


def _mod_kernel(c_ref, w_ref, b_ref, o_ref):
    kk = pl.program_id(1)
    c = c_ref[...]
    sc = (c * jax.nn.sigmoid(c)).astype(BF16)
    part = jnp.dot(sc, w_ref[0].astype(BF16), preferred_element_type=F32)

    @pl.when(kk == 0)
    def _():
        o_ref[0] = part + b_ref[0]

    @pl.when(kk > 0)
    def _():
        o_ref[0] = o_ref[0] + part


def _modulation(c_all, ada_w, ada_b):
    tk = MOD_K_TILE
    n = N_MOD * D_MODEL
    return pl.pallas_call(
        _mod_kernel,
        out_shape=jax.ShapeDtypeStruct((DEPTH, MOD_ROWS, n), F32),
        grid=(DEPTH, D_MODEL // tk),
        in_specs=[
            pl.BlockSpec((MOD_ROWS, tk), lambda l, k: (0, k)),
            pl.BlockSpec((1, tk, n), lambda l, k: (l, k, 0)),
            pl.BlockSpec((1, 1, n), lambda l, k: (l, 0, 0)),
        ],
        out_specs=pl.BlockSpec((1, MOD_ROWS, n), lambda l, k: (l, 0, 0)),
        compiler_params=_cparams(("arbitrary", "arbitrary")),
        name="modulation",
    )(c_all, ada_w, ada_b.reshape(DEPTH, 1, n))


def _tile_rows(ctx_ref, lat_ref, tile):
    return jnp.where(tile < N_CTX_TILES, ctx_ref[...], lat_ref[...])


def _tile_row_specs(ctx_rows, lat_rows, t0=0):
    return [pl.BlockSpec((TM, D_MODEL), lambda i: (jnp.minimum(i + t0, N_CTX_TILES - 1), 0)),
            pl.BlockSpec((TM, D_MODEL), lambda i: (jnp.maximum(i + t0 - N_CTX_TILES, 0), 0))], [ctx_rows, lat_rows]


def _proj_kernel(*refs, segs, has_rope, has_vt, split_in):
    if split_in:
        x = _tile_rows(refs[0], refs[1], pl.program_id(0))
        refs = refs[1:]
    else:
        x = refs[0][...]
    mod_ref, g_ref, w_ref = refs[1:4]
    k = 4
    if has_rope:
        wsw_ref, cos_ref, sin_ref, gain_ref = refs[k:k + 4]
        k += 4
    if has_vt:
        wvt_ref = refs[k]
        k += 1
    out_refs = refs[k:]
    xn = _ada_norm(x, g_ref[...], mod_ref[0:1, :], mod_ref[1:2, :]).astype(BF16)
    lane = lax.broadcasted_iota(jnp.int32, (1, LANES), 1)
    lo = lane < HEAD_DIM
    col = 0
    for seg, o_ref in zip(segs, out_refs):
        width, kind, norm_idx, scale = seg
        chunk = 256
        for c0 in range(0, width, chunk):
            cw = min(chunk, width - c0)
            y = jnp.dot(xn, w_ref[:, col + c0:col + c0 + cw], preferred_element_type=F32)
            if kind == "gelu":
                y = jax.nn.gelu(y, approximate=True)
            elif kind == "rope":
                ysw = jnp.dot(xn, wsw_ref[:, col + c0:col + c0 + cw], preferred_element_type=F32)
                cos_t, sin_t = cos_ref[...], sin_ref[...]
                if norm_idx is not None:
                    cos_t = cos_t * gain_ref[norm_idx:norm_idx + 1, :]
                    sin_t = sin_t * gain_ref[2 + norm_idx:3 + norm_idx, :]
                parts = []
                for b0 in range(0, cw, LANES):
                    blk = y[:, b0:b0 + LANES]
                    rot = blk * cos_t + ysw[:, b0:b0 + LANES] * sin_t
                    if norm_idx is not None:
                        ss = blk * blk
                        s_lo = jnp.sum(jnp.where(lo, ss, 0.0), axis=-1, keepdims=True)
                        s_hi = jnp.sum(jnp.where(lo, 0.0, ss), axis=-1, keepdims=True)
                        ms = jnp.where(lo, s_lo, s_hi) * (1.0 / HEAD_DIM)
                        rot = rot * lax.rsqrt(ms + EPS)
                    parts.append(rot)
                y = parts[0] if len(parts) == 1 else jnp.concatenate(parts, axis=-1)
            if scale != 1.0:
                y = y * scale
            o_ref[:, c0:c0 + cw] = y.astype(o_ref.dtype)
        col += width
    if has_vt:
        vt_ref = out_refs[len(segs)]
        for r0 in range(0, wvt_ref.shape[0], 256):
            rw = min(256, wvt_ref.shape[0] - r0)
            vt = lax.dot_general(wvt_ref[r0:r0 + rw, :], xn, _NT, preferred_element_type=F32)
            vt_ref[r0:r0 + rw, :] = vt.astype(vt_ref.dtype)


def _swap_rotary_halves(a):
    quarter = HEAD_DIM // 4
    return a.reshape(a.shape[:-1] + (-1, 2, quarter))[..., ::-1, :].reshape(a.shape)


def _project(hx, mod_l, g, w, segs, out_dtypes, rope_tabs=None, gains=None, w_vt=None):
    n_tot = sum(s[0] for s in segs)
    resident = functools.partial(pl.BlockSpec, pipeline_mode=pl.Buffered(1))
    split_in = isinstance(hx, tuple)
    if split_in:
        in_specs, args = _tile_row_specs(*hx)
    else:
        in_specs, args = [pl.BlockSpec((TM, D_MODEL), lambda i: (i, 0))], [hx]
    in_specs += [
        pl.BlockSpec((None, N_MOD, D_MODEL), lambda i: (_tile_mod_row(i), 0, 0)),
        pl.BlockSpec((1, D_MODEL), lambda i: (0, 0)),
        resident((D_MODEL, n_tot), lambda i: (0, 0)),
    ]
    args += [mod_l, g.reshape(1, D_MODEL), w]
    has_rope = rope_tabs is not None
    if has_rope:
        in_specs.append(resident((D_MODEL, n_tot), lambda i: (0, 0)))
        args.append(_swap_rotary_halves(w))
        for t in rope_tabs:
            in_specs.append(pl.BlockSpec((TM, LANES), lambda i: (_tile_pos_block(i), 0)))
            args.append(t)
        in_specs.append(pl.BlockSpec(gains.shape, lambda i: (0, 0)))
        args.append(gains)
    out_shape = [jax.ShapeDtypeStruct((N_ROWS, s[0]), dt) for s, dt in zip(segs, out_dtypes)]
    out_specs = [pl.BlockSpec((TM, s[0]), lambda i: (i, 0)) for s in segs]
    if w_vt is not None:
        in_specs.append(resident(w_vt.shape, lambda i: (0, 0)))
        args.append(w_vt)
        out_shape.append(jax.ShapeDtypeStruct((w_vt.shape[0], N_ROWS), BF16))
        out_specs.append(pl.BlockSpec((w_vt.shape[0], TM), lambda i: (0, i)))
    return pl.pallas_call(
        functools.partial(_proj_kernel, segs=segs, has_rope=has_rope, has_vt=w_vt is not None, split_in=split_in),
        out_shape=out_shape,
        grid=(N_ROWS // TM,),
        in_specs=in_specs,
        out_specs=out_specs,
        compiler_params=_cparams(("arbitrary",)),
        name="ada_proj",
    )(*args)


def _head_query(q_ref, rows, head, kv_half):
    pair, half = divmod(head, 2)
    q = q_ref[rows, pair * LANES:(pair + 1) * LANES]
    if half != kv_half:
        q = pltpu.roll(q, HEAD_DIM, 1)
    lane = lax.broadcasted_iota(jnp.int32, (1, LANES), 1)
    keep = (lane < HEAD_DIM) if kv_half == 0 else (lane >= HEAD_DIM)
    return jnp.where(keep, q, jnp.zeros_like(q))


def _attention_units(units, width=1):
    groups = [units[i:i + width] for i in range(0, len(units), width)]
    st1, st2 = [], []
    for idx in range(len(groups) + 2):
        cur = groups[idx] if idx < len(groups) else []
        qs = [u[0]() for u in cur]
        cur_s = [[] for _ in cur]
        cur_m8 = [None for _ in cur]
        new_e = [[] for _ in st1]
        new_d8 = [jnp.zeros((8, TQ), F32) for _ in st1]
        acc = [jnp.zeros((HEAD_DIM, TQ), F32) for _ in st2]
        n_steps = max([len(u[1]) for u in cur] + [len(p[0][1]) for p in st1] + [len(p[0][1]) for p in st2] + [0])
        for c in range(n_steps):
            for i, u in enumerate(cur):
                if c < len(u[1]):
                    k_fn, _, bias_fn = u[1][c]
                    s = lax.dot_general(k_fn(), qs[i], _NT, preferred_element_type=F32)
                    if bias_fn is not None:
                        s = s + bias_fn()
                    cur_s[i].append(s)
                    mc = jnp.max(s.reshape(s.shape[0] // 8, 8, TQ), axis=0)
                    cur_m8[i] = mc if cur_m8[i] is None else jnp.maximum(cur_m8[i], mc)
            for i, (u, p_s, p_m) in enumerate(st1):
                if c < len(u[1]):
                    e = jnp.exp2(p_s[c] - p_m)
                    new_d8[i] = new_d8[i] + jnp.sum(e.reshape(e.shape[0] // 8, 8, TQ), axis=0)
                    new_e[i].append(e.astype(BF16))
            for i, (u, p_e, p_d8, p_m) in enumerate(st2):
                if c < len(u[1]):
                    acc[i] = acc[i] + jnp.dot(u[1][c][1](), p_e[c], preferred_element_type=F32)
        for i, (u, p_e, p_d8, p_m) in enumerate(st2):
            den = jnp.sum(p_d8, axis=0, keepdims=True)
            if u[2] is not None:
                den = den + jnp.exp2(u[2] - p_m)
            u[3](acc[i] / den)
        st2 = [(u, new_e[i], new_d8[i], p_m) for i, (u, p_s, p_m) in enumerate(st1)]
        st1 = []
        for i, u in enumerate(cur):
            m = jnp.max(cur_m8[i], axis=0, keepdims=True)
            if u[2] is not None:
                m = jnp.maximum(m, u[2])
            st1.append((u, cur_s[i], m))


def _store_head(o_ref, head, cols):
    def store(o):
        o_ref[head * HEAD_DIM:(head + 1) * HEAD_DIM, cols] = o.astype(o_ref.dtype)
    return store


def _chunks(k_ref, vt_ref, vrows, k0, n_keys, chunk, bias_fn=None):
    out = []
    for r0 in range(0, n_keys, chunk):
        n = min(chunk, n_keys - r0)
        out.append((
            functools.partial(lambda r0, n: k_ref[pl.ds(k0 + r0, n), :], r0, n),
            functools.partial(lambda r0, n: vt_ref[vrows, pl.ds(k0 + r0, n)], r0, n),
            None if bias_fn is None else functools.partial(bias_fn, r0, n)))
    return out


def _head_units(q_ref, rows, o_ref, cols, n_heads, group, kc_ref, vtc_ref, lat=None, bias_fn=None, sink_fn=None):
    units = []
    for head in range(n_heads):
        kv_half = (head // group) % 2
        vrows = slice(kv_half * HEAD_DIM, (kv_half + 1) * HEAD_DIM)
        chunks = _chunks(kc_ref, vtc_ref, vrows, 0, CTX_LEN, CTX_LEN)
        if lat is not None:
            k_ref, vt_ref, k0, n_keys, chunk = lat
            chunks += _chunks(k_ref, vt_ref, vrows, k0, n_keys, chunk,
                              None if bias_fn is None else functools.partial(bias_fn, head))
        units.append((functools.partial(_head_query, q_ref, rows, head, kv_half), chunks,
                      None if sink_fn is None else sink_fn(head), _store_head(o_ref, head, cols)))
    return units


def _na_row_index():
    rows = SEQ // GRID_W
    idx = []
    for q0, k0 in ((0, 0), (NA_Q_ROWS, 0), (rows - NA_Q_ROWS, rows - NA_K_ROWS)):
        qrow = q0 + np.arange(NA_Q_ROWS)
        krow = k0 + np.arange(NA_K_ROWS)
        r0 = np.clip(qrow - NA_WIN_H // 2, 0, rows - NA_WIN_H)
        row_in = (krow[None, :] >= r0[:, None]) & (krow[None, :] < r0[:, None] + NA_WIN_H)
        idx.append(np.where(row_in, krow[None, :] - qrow[:, None] + NA_WIN_H - 1, NA_MASKED))
    return np.stack(idx)


def _na_strip_plan():
    lo = NA_WIN_H - 1 - NA_WIN_H // 2
    interior = [NA_MASKED] * NA_Q_ROWS + list(range(lo + NA_WIN_H - 1, lo - 1, -1)) + [NA_MASKED] * (NA_Q_ROWS - 1)
    edge = list(range(NA_MASKED - 1, -1, -1))
    strips = [interior + [NA_MASKED], [NA_MASKED] + interior, edge + [NA_MASKED], [NA_MASKED] + edge]
    idx = _na_row_index()
    plan = []
    for case in range(idx.shape[0]):
        rows = []
        for kr in range(NA_K_ROWS):
            want = [int(v) for v in idx[case, :, kr]]
            if all(v == NA_MASKED for v in want):
                rows.append(None)
                continue
            hits = [(s, p) for s, seq in enumerate(strips) for p in range(0, len(seq) - NA_Q_ROWS + 1, 2)
                    if seq[p:p + NA_Q_ROWS] == want]
            rows.append(hits[0])
        plan.append(rows)
    return strips, plan


def _na_fill_bias(strip_ref, bias_s):
    _, plan = _na_strip_plan()
    for head in range(2):
        for case in range(len(plan)):
            for kr, hit in enumerate(plan[case]):
                rows = slice(kr * GRID_W, (kr + 1) * GRID_W)
                if hit is None:
                    bias_s[head, case, rows, :] = jnp.full((GRID_W, TQ), MASK_VALUE, F32)
                else:
                    strip, first = hit
                    bias_s[head, case, rows, :] = strip_ref[head, strip, :, first * GRID_W:first * GRID_W + TQ]


def _na_kernel(qc_ref, ql_ref, kc_ref, vtc_ref, kl_ref, vtl_ref, strip_ref, oc_ref, ol_ref, bias_ref):
    @pl.when(pl.program_id(1) == 0)
    def _():
        _na_fill_bias(strip_ref, bias_ref)

    all_rows = slice(None)
    units = _head_units(qc_ref, all_rows, oc_ref, all_rows, 2, 1, kc_ref, vtc_ref)
    n_tiles = SEQ // TQ
    for j in range(n_tiles):
        case = 0 if j == 0 else (2 if j == n_tiles - 1 else 1)
        k0 = min(max(NA_Q_ROWS * j - NA_WIN_H // 2, 0), SEQ // GRID_W - NA_K_ROWS) * GRID_W
        rows = slice(j * TQ, (j + 1) * TQ)
        units += _head_units(ql_ref, rows, ol_ref, rows, 2, 1, kc_ref, vtc_ref, (kl_ref, vtl_ref, k0, NA_KEYS, NA_KEYS),
                             functools.partial(lambda case, head, r0, n: bias_ref[head, case, r0:r0 + n, :], case))
    _attention_units(units)


def _na_attention(q, k, vt, strips):
    lat0 = N_CTX_ROWS // SEQ
    return pl.pallas_call(
        _na_kernel,
        out_shape=[jax.ShapeDtypeStruct((D_MODEL, N_CTX_ROWS), BF16),
                   jax.ShapeDtypeStruct((D_MODEL, N_LAT_ROWS), BF16)],
        grid=(NA_HEADS // 2, BATCH),
        in_specs=[
            pl.BlockSpec((CTX_LEN, LANES), lambda p, b: (b, p)),
            pl.BlockSpec((SEQ, LANES), lambda p, b: (lat0 + b, p)),
            pl.BlockSpec((CTX_LEN, LANES), lambda p, b: (b, p)),
            pl.BlockSpec((LANES, CTX_LEN), lambda p, b: (p, b)),
            pl.BlockSpec((SEQ, LANES), lambda p, b: (lat0 + b, p)),
            pl.BlockSpec((LANES, SEQ), lambda p, b: (p, lat0 + b)),
            pl.BlockSpec((None,) + strips.shape[1:], lambda p, b: (p, 0, 0, 0, 0)),
        ],
        out_specs=[pl.BlockSpec((LANES, CTX_LEN), lambda p, b: (p, b)),
                   pl.BlockSpec((LANES, SEQ), lambda p, b: (p, b))],
        scratch_shapes=[pltpu.VMEM((2, 3, NA_KEYS, TQ), F32)],
        compiler_params=_cparams(("arbitrary", "arbitrary")),
        name="na_attention",
    )(q, q, k, vt, k, vt, strips)


def _q_tile_index(b, j):
    n_qt = SEQ // TQ
    return jnp.where(j == 0, b, N_CTX_ROWS // TQ + b * n_qt + j - 1)


def _lat_out_index(b, j):
    return b * (SEQ // TQ) + jnp.maximum(j - 1, 0)


def _gqa_kernel(q_ref, kc_ref, vtc_ref, kl_ref, vtl_ref, oc_ref, ol_ref):
    j = pl.program_id(2)
    group = GQA_HEADS // GQA_KV_HEADS
    all_rows = slice(None)

    @pl.when(j == 0)
    def _():
        _attention_units(_head_units(q_ref, all_rows, oc_ref, all_rows, 2 * group, group, kc_ref, vtc_ref),
                         SHORT_UNIT_WIDTH)

    @pl.when(j > 0)
    def _():
        _attention_units(_head_units(q_ref, all_rows, ol_ref, all_rows, 2 * group, group, kc_ref, vtc_ref,
                                     (kl_ref, vtl_ref, 0, SEQ, GQA_KEY_CHUNK)))


def _gqa_attention(q, k, vt):
    lat0 = N_CTX_ROWS // SEQ
    n_qt = SEQ // TQ
    qw = 2 * (GQA_HEADS // GQA_KV_HEADS) * HEAD_DIM
    return pl.pallas_call(
        _gqa_kernel,
        out_shape=[jax.ShapeDtypeStruct((D_MODEL, N_CTX_ROWS), BF16),
                   jax.ShapeDtypeStruct((D_MODEL, N_LAT_ROWS), BF16)],
        grid=(BATCH, GQA_KV_HEADS // 2, n_qt + 1),
        in_specs=[
            pl.BlockSpec((TQ, qw), lambda b, p, j: (_q_tile_index(b, j), p)),
            pl.BlockSpec((CTX_LEN, LANES), lambda b, p, j: (b, p)),
            pl.BlockSpec((LANES, CTX_LEN), lambda b, p, j: (p, b)),
            pl.BlockSpec((SEQ, LANES), lambda b, p, j: (lat0 + b, p)),
            pl.BlockSpec((LANES, SEQ), lambda b, p, j: (p, lat0 + b)),
        ],
        out_specs=[pl.BlockSpec((qw, CTX_LEN), lambda b, p, j: (p, b)),
                   pl.BlockSpec((qw, TQ), lambda b, p, j: (p, _lat_out_index(b, j)))],
        compiler_params=_cparams(("arbitrary", "arbitrary", "arbitrary")),
        name="gqa_attention",
    )(q, k, vt, k, vt)


def _swa_kernel(sink_ref, q_ref, kc_ref, vtc_ref, kl_ref, vtl_ref, oc_ref, ol_ref):
    j = pl.program_id(1)
    group = SWA_HEADS // SWA_KV_HEADS
    all_rows = slice(None)
    sink_fn = lambda head: sink_ref[head] * LOG2E

    @pl.when(j == 0)
    def _():
        _attention_units(_head_units(q_ref, all_rows, oc_ref, all_rows, SWA_HEADS, group, kc_ref, vtc_ref,
                                     sink_fn=sink_fn), SHORT_UNIT_WIDTH)

    @pl.when(j > 0)
    def _():
        band = TQ + 2 * SWA_WINDOW
        q0 = (j - 1) * TQ
        k0 = pl.multiple_of(jnp.clip(q0 - SWA_WINDOW, 0, SEQ - band), LANES)
        kpos = k0 + lax.broadcasted_iota(jnp.int32, (band, 1), 0)
        qpos = q0 + lax.broadcasted_iota(jnp.int32, (1, TQ), 1)
        mask = jnp.where(jnp.abs(qpos - kpos) <= SWA_WINDOW, 0.0, MASK_VALUE).astype(F32)
        _attention_units(_head_units(q_ref, all_rows, ol_ref, all_rows, SWA_HEADS, group, kc_ref, vtc_ref,
                                     (kl_ref, vtl_ref, k0, band, band),
                                     bias_fn=lambda head, r0, n: mask[r0:r0 + n, :], sink_fn=sink_fn),
                         SHORT_UNIT_WIDTH)


def _swa_attention(q, k, vt, sinks):
    lat0 = N_CTX_ROWS // SEQ
    n_qt = SEQ // TQ
    return pl.pallas_call(
        _swa_kernel,
        out_shape=[jax.ShapeDtypeStruct((D_MODEL, N_CTX_ROWS), BF16),
                   jax.ShapeDtypeStruct((D_MODEL, N_LAT_ROWS), BF16)],
        grid=(BATCH, n_qt + 1),
        in_specs=[
            pl.BlockSpec(memory_space=pltpu.SMEM),
            pl.BlockSpec((TQ, D_MODEL), lambda b, j: (_q_tile_index(b, j), 0)),
            pl.BlockSpec((CTX_LEN, LANES), lambda b, j: (b, 0)),
            pl.BlockSpec((LANES, CTX_LEN), lambda b, j: (0, b)),
            pl.BlockSpec((SEQ, LANES), lambda b, j: (lat0 + b, 0)),
            pl.BlockSpec((LANES, SEQ), lambda b, j: (0, lat0 + b)),
        ],
        out_specs=[pl.BlockSpec((D_MODEL, CTX_LEN), lambda b, j: (0, b)),
                   pl.BlockSpec((D_MODEL, TQ), lambda b, j: (0, _lat_out_index(b, j)))],
        compiler_params=_cparams(("arbitrary", "arbitrary")),
        name="swa_attention",
    )(sinks, q, k, vt, k, vt)


def _scan_rows(a, b, h, reverse):
    sub = 8
    groups = a.shape[0] // sub
    a3 = a.reshape(groups, sub, D_MODEL)
    b3 = b.reshape(groups, sub, D_MODEL)
    row = lax.broadcasted_iota(jnp.int32, (1, sub, 1), 1)
    step = 1
    while step < sub:
        keep = (row < sub - step) if reverse else (row >= step)
        shift = sub - step if reverse else step
        a_prev = jnp.where(keep, pltpu.roll(a3, shift, 1), 1.0)
        b_prev = jnp.where(keep, pltpu.roll(b3, shift, 1), 0.0)
        b3 = b3 + a3 * b_prev
        a3 = a3 * a_prev
        step *= 2
    out = [None] * groups
    for g in (range(groups - 1, -1, -1) if reverse else range(groups)):
        hg = a3[g] * h + b3[g]
        out[g] = hg
        h = hg[0:1, :] if reverse else hg[sub - 1:sub, :]
    return jnp.concatenate(out, axis=0), h


def _lru_kernel(xc_ref, xl_ref, cw_ref, cb_ref, wbd_ref, ba_ref, bx_ref, lam_ref, o_ref, conv_s):
    chunk = CTX_LEN
    n_lat = SEQ // chunk
    sp = jax.nn.softplus(-lam_ref[...])

    def conv(before, cur, after):
        ext = jnp.concatenate([before, cur, after], axis=0)
        acc = ext * cw_ref[2:3, :]
        acc = acc + pltpu.roll(ext, 2, 0) * cw_ref[0:1, :]
        acc = acc + pltpu.roll(ext, 1, 0) * cw_ref[1:2, :]
        acc = acc + pltpu.roll(ext, chunk + 15, 0) * cw_ref[3:4, :]
        return acc[8:8 + chunk, :] + cb_ref[...]

    def gates(xc, d):
        xb = xc.astype(BF16)
        pre = []
        for kind in range(2):
            cols = [jnp.dot(xb[:, 256 * c:256 * (c + 1)], wbd_ref[d, kind, c], preferred_element_type=F32)
                    for c in range(D_MODEL // 256)]
            pre.append(jnp.concatenate(cols, axis=-1))
        r = 0.5 * jnp.tanh(0.5 * (pre[0] + ba_ref[d:d + 1, :])) + 0.5
        ig = 0.5 * jnp.tanh(0.5 * (pre[1] + bx_ref[d:d + 1, :])) + 0.5
        log_a = (-LRU_C * r) * sp[d:d + 1, :]
        a = jnp.exp(log_a)
        return a, jnp.sqrt(-jnp.tanh(log_a) * (a * a + 1.0)) * (ig * xc)

    def lat_chunk(c, h, d, reverse):
        start = pl.multiple_of(c * chunk, chunk)
        rows = pl.ds(start, chunk)
        kept = pl.ds(CTX_LEN + start, chunk)
        if d == 0:
            before = jnp.where(c > 0, xl_ref[pl.ds(pl.multiple_of(jnp.maximum(start - 8, 0), 8), 8), :], 0.0)
            after = jnp.where(c < n_lat - 1,
                              xl_ref[pl.ds(pl.multiple_of(jnp.minimum(start + chunk, SEQ - 8), 8), 8), :], 0.0)
            xc = conv(before, xl_ref[rows, :], after)
            conv_s[kept, :] = xc
        else:
            xc = conv_s[kept, :]
        states, h = _scan_rows(*gates(xc, d), h, reverse)
        if d == 0:
            o_ref[rows, :] = states
        else:
            o_ref[rows, :] = o_ref[rows, :] + states
        return h

    zeros = jnp.zeros((8, D_MODEL), F32)
    conv_s[0:CTX_LEN, :] = conv(zeros, xc_ref[...], zeros)
    for d in range(2):
        reverse = d == 1
        _, h = _scan_rows(*gates(conv_s[0:CTX_LEN, :], d), jnp.zeros((1, D_MODEL), F32), reverse)
        lax.fori_loop(0, n_lat, lambda i, h: lat_chunk((n_lat - 1 - i) if reverse else i, h, d, reverse), h)


def _lru_scan(xr, conv_w, conv_b, wbd, b_a, b_x, lam):
    lat0 = N_CTX_ROWS // SEQ
    const2 = lambda b: (0, 0)
    return pl.pallas_call(
        _lru_kernel,
        out_shape=jax.ShapeDtypeStruct((N_LAT_ROWS, D_MODEL), F32),
        grid=(BATCH,),
        in_specs=[
            pl.BlockSpec((CTX_LEN, D_MODEL), lambda b: (b, 0)),
            pl.BlockSpec((SEQ, D_MODEL), lambda b: (lat0 + b, 0)),
            pl.BlockSpec((CONV_WIDTH, D_MODEL), const2),
            pl.BlockSpec((1, D_MODEL), const2),
            pl.BlockSpec(wbd.shape, lambda b: (0, 0, 0, 0, 0)),
            pl.BlockSpec((2, D_MODEL), const2),
            pl.BlockSpec((2, D_MODEL), const2),
            pl.BlockSpec((2, D_MODEL), const2),
        ],
        out_specs=pl.BlockSpec((SEQ, D_MODEL), lambda b: (b, 0)),
        scratch_shapes=[pltpu.VMEM((CTX_LEN + SEQ, D_MODEL), F32)],
        compiler_params=_cparams(("arbitrary",)),
        name="rglru_scan",
    )(xr, xr, conv_w, conv_b.reshape(1, D_MODEL), wbd, b_a, b_x, lam)


def _ffn_kernel(*refs, lru, final, split_in):
    if split_in:
        h = _tile_rows(refs[0], refs[1], pl.program_id(0))
        refs = refs[1:]
    else:
        h = refs[0][...]
    mod_ref, g_ref, wo_ref, win_ref, wout_ref = refs[1:6]
    k = 6
    if lru:
        hs_ref, gl_ref = refs[k:k + 2]
        k += 2
        o_mix = (hs_ref[...] * gl_ref[...]).astype(BF16)
        y = jnp.dot(o_mix, wo_ref[...], preferred_element_type=F32)
    else:
        oc_ref, ol_ref = refs[k:k + 2]
        k += 2
        o_t = jnp.where(pl.program_id(0) < N_CTX_TILES, oc_ref[...], ol_ref[...])
        y = lax.dot_general(o_t, wo_ref[...], (((0,), (0,)), ((), ())), preferred_element_type=F32)
    if final:
        gf_ref = refs[k]
        k += 1
    out_ref = refs[k]
    h1 = h + mod_ref[2:3, :] * y
    xn = _ada_norm(h1, g_ref[...], mod_ref[3:4, :], mod_ref[4:5, :]).astype(BF16)
    acc = jnp.zeros(h1.shape, F32)
    for c0 in range(0, FFN_HIDDEN, FFN_CHUNK):
        a = jnp.dot(xn, win_ref[:, c0:c0 + FFN_CHUNK], preferred_element_type=F32)
        g = jnp.dot(xn, win_ref[:, FFN_HIDDEN + c0:FFN_HIDDEN + c0 + FFN_CHUNK], preferred_element_type=F32)
        u = ((a * jax.nn.sigmoid(a)) * g).astype(BF16)
        acc = acc + jnp.dot(u, wout_ref[c0:c0 + FFN_CHUNK, :], preferred_element_type=F32)
    h2 = h1 + mod_ref[5:6, :] * acc
    if final:
        ms = jnp.mean(h2 * h2, axis=-1, keepdims=True)
        h2 = (h2 * lax.rsqrt(ms + EPS)) * gf_ref[...]
    out_ref[...] = h2


def _mix_ffn(hx, mod_l, g_ffn, w_o, w_in, w_out, layer, mix_inputs, lru=False, final_g=None):
    final = final_g is not None
    t0 = N_CTX_TILES if final else 0
    n_tiles = N_ROWS // TM - t0
    resident = functools.partial(pl.BlockSpec, pipeline_mode=pl.Buffered(1))
    split_in = isinstance(hx, tuple)
    assert not (split_in and final)
    if split_in:
        in_specs, args = _tile_row_specs(*hx)
    else:
        in_specs, args = [pl.BlockSpec((TM, D_MODEL), lambda i: (i + t0, 0))], [hx]
    in_specs += [
        pl.BlockSpec((None, N_MOD, D_MODEL), lambda i: (_tile_mod_row(i + t0), 0, 0)),
        pl.BlockSpec((1, D_MODEL), lambda i: (0, 0)),
        resident(w_o.shape, lambda i: (0, 0)),
        resident((None,) + w_in.shape[1:], lambda i: (layer, 0, 0)),
        resident((None,) + w_out.shape[1:], lambda i: (layer, 0, 0)),
    ]
    args += [mod_l, g_ffn.reshape(1, D_MODEL), w_o, w_in, w_out]
    if lru:
        for m, off in mix_inputs:
            in_specs.append(pl.BlockSpec((TM, D_MODEL), lambda i, off=off: (i + off, 0)))
            args.append(m)
    else:
        o_ctx, o_lat = mix_inputs
        in_specs.append(pl.BlockSpec((D_MODEL, TM), lambda i: (0, jnp.minimum(i, N_CTX_TILES - 1))))
        in_specs.append(pl.BlockSpec((D_MODEL, TM), lambda i: (0, jnp.maximum(i - N_CTX_TILES, 0))))
        args += [o_ctx, o_lat]
    if final:
        in_specs.append(pl.BlockSpec((1, D_MODEL), lambda i: (0, 0)))
        args.append(final_g.reshape(1, D_MODEL))
    return pl.pallas_call(
        functools.partial(_ffn_kernel, lru=lru, final=final, split_in=split_in),
        out_shape=jax.ShapeDtypeStruct((n_tiles * TM, D_MODEL), F32),
        grid=(n_tiles,),
        in_specs=in_specs,
        out_specs=pl.BlockSpec((TM, D_MODEL), lambda i: (i, 0)),
        compiler_params=_cparams(("arbitrary",)),
        name="mix_ffn",
    )(*args)


def _rope_tables():
    f32 = np.float32
    t = np.arange(SEQ)
    row = (t // GRID_W).astype(f32)
    col = (t % GRID_W).astype(f32)
    half = HEAD_DIM // 2
    inv_freq = (f32(1.0) / (f32(ROPE_THETA) ** (np.arange(0, half, 2, dtype=f32) / f32(half)))).astype(f32)
    d = np.arange(LANES) % HEAD_DIM
    freq = inv_freq[(d % half) % (half // 2)]
    pos = np.where((d < half)[None, :], row[:, None], col[:, None])
    ang = (pos * freq[None, :]).astype(f32)
    first = ((d % half) < half // 2)[None, :]
    cos, sin = np.cos(ang).astype(f32), np.sin(ang).astype(f32)
    ident = np.ones((TM, LANES), f32)
    zero = np.zeros((TM, LANES), f32)
    return jnp.asarray(np.concatenate([ident, cos])), jnp.asarray(np.concatenate([zero, np.where(first, -sin, sin)]))


def _na_bias_strips(rpb):
    n_drow = NA_MASKED
    n_dcol = 2 * NA_WIN_W - 1
    span = 2 * GRID_W
    left = GRID_W - NA_WIN_W
    w = jnp.pad(rpb.astype(F32), ((0, 0), (0, 0), (left, span - left - n_dcol)))
    flat = jnp.tile(w, (1, 1, GRID_W))[..., GRID_W - 1:GRID_W - 1 + GRID_W * (span - 1)]
    toep = flat.reshape(NA_HEADS, n_drow, GRID_W, span - 1)[..., :GRID_W]
    qcol = np.arange(GRID_W)
    col_start = np.clip(qcol - NA_WIN_W // 2, 0, GRID_W - NA_WIN_W)
    col_in = (qcol[None, :] >= col_start[:, None]) & (qcol[None, :] < col_start[:, None] + NA_WIN_W)
    blocks = jnp.where(col_in[None, None], toep * LOG2E, MASK_VALUE)
    blocks = jnp.concatenate([blocks, jnp.full((NA_HEADS, 1, GRID_W, GRID_W), MASK_VALUE, F32)], axis=1)
    blocks_t = blocks.transpose(0, 1, 3, 2)
    seqs, _ = _na_strip_plan()
    strips = jnp.stack([jnp.concatenate([blocks_t[:, i] for i in seq], axis=-1) for seq in seqs], axis=1)
    return strips.reshape(NA_HEADS // 2, 2, len(seqs), GRID_W, len(seqs[0]) * GRID_W)


def _block_diag_256(w):
    per = 256 // LRU_BLOCK_DIM
    w4 = w.reshape(LRU_BLOCKS // per, per, LRU_BLOCK_DIM, LRU_BLOCK_DIM)
    eye = jnp.eye(per, dtype=w.dtype)
    return jnp.einsum("cide,ij->cidje", w4, eye).reshape(LRU_BLOCKS // per, 256, 256)


def kernel(x, c, ctx, c_ctx, ada_w, ada_b, norm_mix, norm_ffn, norm_final, ffn_w_in, ffn_w_out, na_w_qkv, na_rpb, na_w_o, gqa_w_qkv, gqa_q_gain, gqa_k_gain, gqa_w_o, swa_w_qkv, swa_sinks, swa_w_o, lru_w_in, lru_conv_w, lru_conv_b, lru_w_a, lru_b_a, lru_w_x, lru_b_x, lru_lam, lru_w_out):
    assert x.shape == (BATCH, SEQ, D_MODEL) and ctx.shape == (BATCH, CTX_LEN, D_MODEL)
    hx = (ctx.reshape(N_CTX_ROWS, D_MODEL), x.reshape(N_LAT_ROWS, D_MODEL))
    c_all = jnp.concatenate([c, c_ctx[None, :], jnp.zeros((MOD_ROWS - BATCH - 1, D_MODEL), F32)], axis=0)
    mod = _modulation(c_all, ada_w, ada_b).reshape(DEPTH, MOD_ROWS, N_MOD, D_MODEL)
    rope_tabs = _rope_tables()
    q_scale = HEAD_DIM ** -0.5 * LOG2E
    lat_tile0 = N_CTX_TILES
    w_in, w_out = ffn_w_in.astype(BF16), ffn_w_out.astype(BF16)

    def qkv_split(w, n_q, n_kv):
        wq_k = w[:, :(n_q + n_kv) * HEAD_DIM].astype(BF16)
        return wq_k, w[:, (n_q + n_kv) * HEAD_DIM:].T.astype(BF16)

    w_qk, w_vt = qkv_split(na_w_qkv[0], NA_HEADS, NA_HEADS)
    q, k, vt = _project(hx, mod[0], norm_mix[0], w_qk,
                        ((D_MODEL, "plain", None, q_scale), (D_MODEL, "plain", None, 1.0)), (BF16, BF16), w_vt=w_vt)
    o = _na_attention(q, k, vt, _na_bias_strips(na_rpb[0]))
    hx = _mix_ffn(hx, mod[0], norm_ffn[0], na_w_o[0].astype(BF16), w_in, w_out, 0, o)

    kvw = GQA_KV_HEADS * HEAD_DIM
    gains = jnp.stack([jnp.tile(g_, 2) for g_ in (gqa_q_gain[0], gqa_k_gain[0],
                                                  _swap_rotary_halves(gqa_q_gain[0]), _swap_rotary_halves(gqa_k_gain[0]))])
    w_qk, w_vt = qkv_split(gqa_w_qkv[0], GQA_HEADS, GQA_KV_HEADS)
    q, k, vt = _project(hx, mod[1], norm_mix[1], w_qk,
                        ((D_MODEL, "rope", 0, q_scale), (kvw, "rope", 1, 1.0)), (BF16, BF16),
                        rope_tabs, gains, w_vt=w_vt)
    o = _gqa_attention(q, k, vt)
    hx = _mix_ffn(hx, mod[1], norm_ffn[1], gqa_w_o[0].astype(BF16), w_in, w_out, 1, o)

    kvw = SWA_KV_HEADS * HEAD_DIM
    w_qk, w_vt = qkv_split(swa_w_qkv[0], SWA_HEADS, SWA_KV_HEADS)
    q, k, vt = _project(hx, mod[2], norm_mix[2], w_qk,
                        ((D_MODEL, "rope", None, q_scale), (kvw, "rope", None, 1.0)), (BF16, BF16),
                        rope_tabs, jnp.ones((4, LANES), F32), w_vt=w_vt)
    o = _swa_attention(q, k, vt, swa_sinks[0])
    hx = _mix_ffn(hx, mod[2], norm_ffn[2], swa_w_o[0].astype(BF16), w_in, w_out, 2, o)

    xr, gl = _project(hx, mod[3], norm_mix[3], lru_w_in[0].astype(BF16),
                      ((D_MODEL, "plain", None, 1.0), (D_MODEL, "gelu", None, 1.0)), (F32, F32))
    wbd = jnp.stack([jnp.stack([_block_diag_256(lru_w_a[0, d]), _block_diag_256(lru_w_x[0, d])])
                     for d in range(2)]).astype(BF16)
    hs = _lru_scan(xr, lru_conv_w[0], lru_conv_b[0], wbd, lru_b_a[0], lru_b_x[0], lru_lam[0])
    out = _mix_ffn(hx, mod[3], norm_ffn[3], lru_w_out[0].astype(BF16), w_in, w_out, 3,
                   [(hs, 0), (gl, lat_tile0)], lru=True, final_g=norm_final)
    return out.reshape(BATCH, SEQ, D_MODEL)
```

```python
import functools

import jax
import jax.numpy as jnp
import numpy as np
from jax import lax
from jax.experimental import pallas as pl
from jax.experimental.pallas import tpu as pltpu

F32 = jnp.float32
BF16 = jnp.bfloat16

D_MODEL = 1024
BATCH = 8
SEQ = 2048
DEPTH = 4
GRID_W = 64
CTX_LEN = 256
HEAD_DIM = 64
ROPE_THETA = 10000.0
NA_HEADS = 16
NA_WIN_H = 8
NA_WIN_W = 16
GQA_HEADS = 16
GQA_KV_HEADS = 4
SWA_HEADS = 16
SWA_KV_HEADS = 2
SWA_WINDOW = 128
LRU_BLOCKS = 16
LRU_BLOCK_DIM = D_MODEL // LRU_BLOCKS
CONV_WIDTH = 4
LRU_C = 8.0
FFN_HIDDEN = 2816
N_MOD = 6
EPS = 1e-6
MASK_VALUE = -1e30
LOG2E = 1.4426950408889634

N_CTX_ROWS = BATCH * CTX_LEN
N_LAT_ROWS = BATCH * SEQ
N_ROWS = N_CTX_ROWS + N_LAT_ROWS
MOD_ROWS = 16
CTX_MOD_ROW = BATCH

LANES = 128
TM = 512
TQ = 256
FFN_CHUNK = 256
MOD_K_TILE = 256
FFN_CAST_ROWS = 128
GQA_KEY_CHUNK = 256
SHORT_UNIT_WIDTH = 2
VMEM_LIMIT = 56 * 1024 * 1024

N_CTX_TILES = N_CTX_ROWS // TM
TILES_PER_BATCH = SEQ // TM
NA_Q_ROWS = TQ // GRID_W
NA_K_ROWS = 12
NA_KEYS = NA_K_ROWS * GRID_W
NA_MASKED = 2 * NA_WIN_H - 1

_NT = (((1,), (1,)), ((), ()))


def _cparams(sem):
    return pltpu.CompilerParams(dimension_semantics=sem, vmem_limit_bytes=VMEM_LIMIT)


def _tile_mod_row(i):
    return jnp.where(i < N_CTX_TILES, CTX_MOD_ROW, (i - N_CTX_TILES) // TILES_PER_BATCH)


def _tile_pos_block(i):
    return jnp.where(i < N_CTX_TILES, 0, 1 + (i - N_CTX_TILES) % TILES_PER_BATCH)


def _ada_norm(x, g, shift, scale):
    ms = jnp.mean(x * x, axis=-1, keepdims=True)
    y = x * lax.rsqrt(ms + EPS)
    return (y * g) * (1.0 + scale) + shift


def _mod_kernel(c_ref, w_ref, b_ref, o_ref):
    kk = pl.program_id(1)
    c = c_ref[...]
    sc = (c * jax.nn.sigmoid(c)).astype(BF16)
    part = jnp.dot(sc, w_ref[0].astype(BF16), preferred_element_type=F32)

    @pl.when(kk == 0)
    def _():
        o_ref[0] = part + b_ref[0]

    @pl.when(kk > 0)
    def _():
        o_ref[0] = o_ref[0] + part


def _modulation(c_all, ada_w, ada_b):
    tk = MOD_K_TILE
    n = N_MOD * D_MODEL
    return pl.pallas_call(
        _mod_kernel,
        out_shape=jax.ShapeDtypeStruct((DEPTH, MOD_ROWS, n), F32),
        grid=(DEPTH, D_MODEL // tk),
        in_specs=[
            pl.BlockSpec((MOD_ROWS, tk), lambda l, k: (0, k)),
            pl.BlockSpec((1, tk, n), lambda l, k: (l, k, 0)),
            pl.BlockSpec((1, 1, n), lambda l, k: (l, 0, 0)),
        ],
        out_specs=pl.BlockSpec((1, MOD_ROWS, n), lambda l, k: (l, 0, 0)),
        compiler_params=_cparams(("arbitrary", "arbitrary")),
        name="modulation",
    )(c_all, ada_w, ada_b.reshape(DEPTH, 1, n))


def _tile_rows(ctx_ref, lat_ref, tile):
    return jnp.where(tile < N_CTX_TILES, ctx_ref[...], lat_ref[...])


def _tile_row_specs(ctx_rows, lat_rows, t0=0):
    return [pl.BlockSpec((TM, D_MODEL), lambda i: (jnp.minimum(i + t0, N_CTX_TILES - 1), 0)),
            pl.BlockSpec((TM, D_MODEL), lambda i: (jnp.maximum(i + t0 - N_CTX_TILES, 0), 0))], [ctx_rows, lat_rows]


def _proj_kernel(*refs, segs, has_rope, has_vt, split_in, cast_ffn):
    if split_in:
        x = _tile_rows(refs[0], refs[1], pl.program_id(0))
        refs = refs[1:]
    else:
        x = refs[0][...]
    mod_ref, g_ref, w_ref = refs[1:4]
    k = 4
    if has_rope:
        wsw_ref, cos_ref, sin_ref, gain_ref = refs[k:k + 4]
        k += 4
    if has_vt:
        wvt_ref = refs[k]
        k += 1
    if cast_ffn:
        win_ref, wout_ref = refs[k:k + 2]
        k += 2
        refs[-2][...] = win_ref[...].astype(BF16)
        refs[-1][...] = wout_ref[...].astype(BF16)
    out_refs = refs[k:]
    xn = _ada_norm(x, g_ref[...], mod_ref[0:1, :], mod_ref[1:2, :]).astype(BF16)
    lane = lax.broadcasted_iota(jnp.int32, (1, LANES), 1)
    lo = lane < HEAD_DIM
    col = 0
    for seg, o_ref in zip(segs, out_refs):
        width, kind, norm_idx, scale = seg
        chunk = 256
        for c0 in range(0, width, chunk):
            cw = min(chunk, width - c0)
            y = jnp.dot(xn, w_ref[:, col + c0:col + c0 + cw], preferred_element_type=F32)
            if kind == "gelu":
                y = jax.nn.gelu(y, approximate=True)
            elif kind == "rope":
                ysw = jnp.dot(xn, wsw_ref[:, col + c0:col + c0 + cw], preferred_element_type=F32)
                cos_t, sin_t = cos_ref[...], sin_ref[...]
                if norm_idx is not None:
                    cos_t = cos_t * gain_ref[norm_idx:norm_idx + 1, :]
                    sin_t = sin_t * gain_ref[2 + norm_idx:3 + norm_idx, :]
                parts = []
                for b0 in range(0, cw, LANES):
                    blk = y[:, b0:b0 + LANES]
                    rot = blk * cos_t + ysw[:, b0:b0 + LANES] * sin_t
                    if norm_idx is not None:
                        ss = blk * blk
                        s_lo = jnp.sum(jnp.where(lo, ss, 0.0), axis=-1, keepdims=True)
                        s_hi = jnp.sum(jnp.where(lo, 0.0, ss), axis=-1, keepdims=True)
                        ms = jnp.where(lo, s_lo, s_hi) * (1.0 / HEAD_DIM)
                        rot = rot * lax.rsqrt(ms + EPS)
                    parts.append(rot)
                y = parts[0] if len(parts) == 1 else jnp.concatenate(parts, axis=-1)
            if scale != 1.0:
                y = y * scale
            o_ref[:, c0:c0 + cw] = y.astype(o_ref.dtype)
        col += width
    if has_vt:
        vt_ref = out_refs[len(segs)]
        for r0 in range(0, wvt_ref.shape[0], 256):
            rw = min(256, wvt_ref.shape[0] - r0)
            vt = lax.dot_general(wvt_ref[r0:r0 + rw, :], xn, _NT, preferred_element_type=F32)
            vt_ref[r0:r0 + rw, :] = vt.astype(vt_ref.dtype)


def _swap_rotary_halves(a):
    quarter = HEAD_DIM // 4
    return a.reshape(a.shape[:-1] + (-1, 2, quarter))[..., ::-1, :].reshape(a.shape)


def _project(hx, mod_l, g, w, segs, out_dtypes, rope_tabs=None, gains=None, w_vt=None, ffn_w=None):
    n_tot = sum(s[0] for s in segs)
    resident = functools.partial(pl.BlockSpec, pipeline_mode=pl.Buffered(1))
    split_in = isinstance(hx, tuple)
    if split_in:
        in_specs, args = _tile_row_specs(*hx)
    else:
        in_specs, args = [pl.BlockSpec((TM, D_MODEL), lambda i: (i, 0))], [hx]
    in_specs += [
        pl.BlockSpec((None, N_MOD, D_MODEL), lambda i: (_tile_mod_row(i), 0, 0)),
        pl.BlockSpec((1, D_MODEL), lambda i: (0, 0)),
        resident((D_MODEL, n_tot), lambda i: (0, 0)),
    ]
    args += [mod_l, g.reshape(1, D_MODEL), w]
    has_rope = rope_tabs is not None
    if has_rope:
        in_specs.append(resident((D_MODEL, n_tot), lambda i: (0, 0)))
        args.append(_swap_rotary_halves(w))
        for t in rope_tabs:
            in_specs.append(pl.BlockSpec((TM, LANES), lambda i: (_tile_pos_block(i), 0)))
            args.append(t)
        in_specs.append(pl.BlockSpec(gains.shape, lambda i: (0, 0)))
        args.append(gains)
    out_shape = [jax.ShapeDtypeStruct((N_ROWS, s[0]), dt) for s, dt in zip(segs, out_dtypes)]
    out_specs = [pl.BlockSpec((TM, s[0]), lambda i: (i, 0)) for s in segs]
    if w_vt is not None:
        in_specs.append(resident(w_vt.shape, lambda i: (0, 0)))
        args.append(w_vt)
        out_shape.append(jax.ShapeDtypeStruct((w_vt.shape[0], N_ROWS), BF16))
        out_specs.append(pl.BlockSpec((w_vt.shape[0], TM), lambda i: (0, i)))
    if ffn_w is not None:
        f_in, f_out, layer = ffn_w
        n_blk = FFN_HIDDEN // FFN_CAST_ROWS
        blk = lambda i: jnp.minimum(i, n_blk - 1)
        cols = f_in.shape[2] // n_blk
        in_specs += [pl.BlockSpec((None, D_MODEL, cols), lambda i: (layer, 0, blk(i))),
                     pl.BlockSpec((None, FFN_CAST_ROWS, D_MODEL), lambda i: (layer, blk(i), 0))]
        args += [f_in, f_out]
        out_shape += [jax.ShapeDtypeStruct(f_in.shape[1:], BF16), jax.ShapeDtypeStruct(f_out.shape[1:], BF16)]
        out_specs += [pl.BlockSpec((D_MODEL, cols), lambda i: (0, blk(i))),
                      pl.BlockSpec((FFN_CAST_ROWS, D_MODEL), lambda i: (blk(i), 0))]
    return pl.pallas_call(
        functools.partial(_proj_kernel, segs=segs, has_rope=has_rope, has_vt=w_vt is not None, split_in=split_in,
                          cast_ffn=ffn_w is not None),
        out_shape=out_shape,
        grid=(N_ROWS // TM,),
        in_specs=in_specs,
        out_specs=out_specs,
        compiler_params=_cparams(("arbitrary",)),
        name="ada_proj",
    )(*args)


def _head_query(q_ref, rows, head, kv_half):
    pair, half = divmod(head, 2)
    q = q_ref[rows, pair * LANES:(pair + 1) * LANES]
    if half != kv_half:
        q = pltpu.roll(q, HEAD_DIM, 1)
    lane = lax.broadcasted_iota(jnp.int32, (1, LANES), 1)
    keep = (lane < HEAD_DIM) if kv_half == 0 else (lane >= HEAD_DIM)
    return jnp.where(keep, q, jnp.zeros_like(q))


def _attention_units(units, width=1):
    groups = [units[i:i + width] for i in range(0, len(units), width)]
    st1, st2 = [], []
    for idx in range(len(groups) + 2):
        cur = groups[idx] if idx < len(groups) else []
        qs = [u[0]() for u in cur]
        cur_s = [[] for _ in cur]
        cur_m8 = [None for _ in cur]
        new_e = [[] for _ in st1]
        new_d8 = [jnp.zeros((8, TQ), F32) for _ in st1]
        acc = [jnp.zeros((HEAD_DIM, TQ), F32) for _ in st2]
        n_steps = max([len(u[1]) for u in cur] + [len(p[0][1]) for p in st1] + [len(p[0][1]) for p in st2] + [0])
        for c in range(n_steps):
            for i, u in enumerate(cur):
                if c < len(u[1]):
                    k_fn, _, bias_fn = u[1][c]
                    s = lax.dot_general(k_fn(), qs[i], _NT, preferred_element_type=F32)
                    if bias_fn is not None:
                        s = s + bias_fn()
                    cur_s[i].append(s)
                    mc = jnp.max(s.reshape(s.shape[0] // 8, 8, TQ), axis=0)
                    cur_m8[i] = mc if cur_m8[i] is None else jnp.maximum(cur_m8[i], mc)
            for i, (u, p_s, p_m) in enumerate(st1):
                if c < len(u[1]):
                    e = jnp.exp2(p_s[c] - p_m)
                    new_d8[i] = new_d8[i] + jnp.sum(e.reshape(e.shape[0] // 8, 8, TQ), axis=0)
                    new_e[i].append(e.astype(BF16))
            for i, (u, p_e, p_d8, p_m) in enumerate(st2):
                if c < len(u[1]):
                    acc[i] = acc[i] + jnp.dot(u[1][c][1](), p_e[c], preferred_element_type=F32)
        for i, (u, p_e, p_d8, p_m) in enumerate(st2):
            den = jnp.sum(p_d8, axis=0, keepdims=True)
            if u[2] is not None:
                den = den + jnp.exp2(u[2] - p_m)
            u[3](acc[i] / den)
        st2 = [(u, new_e[i], new_d8[i], p_m) for i, (u, p_s, p_m) in enumerate(st1)]
        st1 = []
        for i, u in enumerate(cur):
            m = jnp.max(cur_m8[i], axis=0, keepdims=True)
            if u[2] is not None:
                m = jnp.maximum(m, u[2])
            st1.append((u, cur_s[i], m))


def _store_head(o_ref, head, cols):
    def store(o):
        o_ref[head * HEAD_DIM:(head + 1) * HEAD_DIM, cols] = o.astype(o_ref.dtype)
    return store


def _chunks(k_ref, vt_ref, vrows, k0, n_keys, chunk, bias_fn=None):
    out = []
    for r0 in range(0, n_keys, chunk):
        n = min(chunk, n_keys - r0)
        out.append((
            functools.partial(lambda r0, n: k_ref[pl.ds(k0 + r0, n), :], r0, n),
            functools.partial(lambda r0, n: vt_ref[vrows, pl.ds(k0 + r0, n)], r0, n),
            None if bias_fn is None else functools.partial(bias_fn, r0, n)))
    return out


def _head_units(q_ref, rows, o_ref, cols, n_heads, group, kc_ref, vtc_ref, lat=None, bias_fn=None, sink_fn=None):
    units = []
    for head in range(n_heads):
        kv_half = (head // group) % 2
        vrows = slice(kv_half * HEAD_DIM, (kv_half + 1) * HEAD_DIM)
        chunks = _chunks(kc_ref, vtc_ref, vrows, 0, CTX_LEN, CTX_LEN)
        if lat is not None:
            k_ref, vt_ref, k0, n_keys, chunk = lat
            chunks += _chunks(k_ref, vt_ref, vrows, k0, n_keys, chunk,
                              None if bias_fn is None else functools.partial(bias_fn, head))
        units.append((functools.partial(_head_query, q_ref, rows, head, kv_half), chunks,
                      None if sink_fn is None else sink_fn(head), _store_head(o_ref, head, cols)))
    return units


def _na_row_index():
    rows = SEQ // GRID_W
    idx = []
    for q0, k0 in ((0, 0), (NA_Q_ROWS, 0), (rows - NA_Q_ROWS, rows - NA_K_ROWS)):
        qrow = q0 + np.arange(NA_Q_ROWS)
        krow = k0 + np.arange(NA_K_ROWS)
        r0 = np.clip(qrow - NA_WIN_H // 2, 0, rows - NA_WIN_H)
        row_in = (krow[None, :] >= r0[:, None]) & (krow[None, :] < r0[:, None] + NA_WIN_H)
        idx.append(np.where(row_in, krow[None, :] - qrow[:, None] + NA_WIN_H - 1, NA_MASKED))
    return np.stack(idx)


def _na_strip_plan():
    lo = NA_WIN_H - 1 - NA_WIN_H // 2
    interior = [NA_MASKED] * NA_Q_ROWS + list(range(lo + NA_WIN_H - 1, lo - 1, -1)) + [NA_MASKED] * (NA_Q_ROWS - 1)
    edge = list(range(NA_MASKED - 1, -1, -1))
    strips = [interior + [NA_MASKED], [NA_MASKED] + interior, edge + [NA_MASKED], [NA_MASKED] + edge]
    idx = _na_row_index()
    plan = []
    for case in range(idx.shape[0]):
        rows = []
        for kr in range(NA_K_ROWS):
            want = [int(v) for v in idx[case, :, kr]]
            if all(v == NA_MASKED for v in want):
                rows.append(None)
                continue
            hits = [(s, p) for s, seq in enumerate(strips) for p in range(0, len(seq) - NA_Q_ROWS + 1, 2)
                    if seq[p:p + NA_Q_ROWS] == want]
            rows.append(hits[0])
        plan.append(rows)
    return strips, plan


def _na_fill_bias(strip_ref, bias_s):
    _, plan = _na_strip_plan()
    for head in range(2):
        for case in range(len(plan)):
            for kr, hit in enumerate(plan[case]):
                rows = slice(kr * GRID_W, (kr + 1) * GRID_W)
                if hit is None:
                    bias_s[head, case, rows, :] = jnp.full((GRID_W, TQ), MASK_VALUE, F32)
                else:
                    strip, first = hit
                    bias_s[head, case, rows, :] = strip_ref[head, strip, :, first * GRID_W:first * GRID_W + TQ]


def _na_kernel(qc_ref, ql_ref, kc_ref, vtc_ref, kl_ref, vtl_ref, strip_ref, oc_ref, ol_ref, bias_ref):
    @pl.when(pl.program_id(1) == 0)
    def _():
        _na_fill_bias(strip_ref, bias_ref)

    all_rows = slice(None)
    units = _head_units(qc_ref, all_rows, oc_ref, all_rows, 2, 1, kc_ref, vtc_ref)
    n_tiles = SEQ // TQ
    for j in range(n_tiles):
        case = 0 if j == 0 else (2 if j == n_tiles - 1 else 1)
        k0 = min(max(NA_Q_ROWS * j - NA_WIN_H // 2, 0), SEQ // GRID_W - NA_K_ROWS) * GRID_W
        rows = slice(j * TQ, (j + 1) * TQ)
        units += _head_units(ql_ref, rows, ol_ref, rows, 2, 1, kc_ref, vtc_ref, (kl_ref, vtl_ref, k0, NA_KEYS, NA_KEYS),
                             functools.partial(lambda case, head, r0, n: bias_ref[head, case, r0:r0 + n, :], case))
    _attention_units(units)


def _na_attention(q, k, vt, strips):
    lat0 = N_CTX_ROWS // SEQ
    return pl.pallas_call(
        _na_kernel,
        out_shape=[jax.ShapeDtypeStruct((D_MODEL, N_CTX_ROWS), BF16),
                   jax.ShapeDtypeStruct((D_MODEL, N_LAT_ROWS), BF16)],
        grid=(NA_HEADS // 2, BATCH),
        in_specs=[
            pl.BlockSpec((CTX_LEN, LANES), lambda p, b: (b, p)),
            pl.BlockSpec((SEQ, LANES), lambda p, b: (lat0 + b, p)),
            pl.BlockSpec((CTX_LEN, LANES), lambda p, b: (b, p)),
            pl.BlockSpec((LANES, CTX_LEN), lambda p, b: (p, b)),
            pl.BlockSpec((SEQ, LANES), lambda p, b: (lat0 + b, p)),
            pl.BlockSpec((LANES, SEQ), lambda p, b: (p, lat0 + b)),
            pl.BlockSpec((None,) + strips.shape[1:], lambda p, b: (p, 0, 0, 0, 0)),
        ],
        out_specs=[pl.BlockSpec((LANES, CTX_LEN), lambda p, b: (p, b)),
                   pl.BlockSpec((LANES, SEQ), lambda p, b: (p, b))],
        scratch_shapes=[pltpu.VMEM((2, 3, NA_KEYS, TQ), F32)],
        compiler_params=_cparams(("arbitrary", "arbitrary")),
        name="na_attention",
    )(q, q, k, vt, k, vt, strips)


def _q_tile_index(b, j):
    n_qt = SEQ // TQ
    return jnp.where(j == 0, b, N_CTX_ROWS // TQ + b * n_qt + j - 1)


def _lat_out_index(b, j):
    return b * (SEQ // TQ) + jnp.maximum(j - 1, 0)


def _gqa_kernel(q_ref, kc_ref, vtc_ref, kl_ref, vtl_ref, oc_ref, ol_ref):
    j = pl.program_id(2)
    group = GQA_HEADS // GQA_KV_HEADS
    all_rows = slice(None)

    @pl.when(j == 0)
    def _():
        _attention_units(_head_units(q_ref, all_rows, oc_ref, all_rows, 2 * group, group, kc_ref, vtc_ref),
                         SHORT_UNIT_WIDTH)

    @pl.when(j > 0)
    def _():
        _attention_units(_head_units(q_ref, all_rows, ol_ref, all_rows, 2 * group, group, kc_ref, vtc_ref,
                                     (kl_ref, vtl_ref, 0, SEQ, GQA_KEY_CHUNK)))


def _gqa_attention(q, k, vt):
    lat0 = N_CTX_ROWS // SEQ
    n_qt = SEQ // TQ
    qw = 2 * (GQA_HEADS // GQA_KV_HEADS) * HEAD_DIM
    return pl.pallas_call(
        _gqa_kernel,
        out_shape=[jax.ShapeDtypeStruct((D_MODEL, N_CTX_ROWS), BF16),
                   jax.ShapeDtypeStruct((D_MODEL, N_LAT_ROWS), BF16)],
        grid=(BATCH, GQA_KV_HEADS // 2, n_qt + 1),
        in_specs=[
            pl.BlockSpec((TQ, qw), lambda b, p, j: (_q_tile_index(b, j), p)),
            pl.BlockSpec((CTX_LEN, LANES), lambda b, p, j: (b, p)),
            pl.BlockSpec((LANES, CTX_LEN), lambda b, p, j: (p, b)),
            pl.BlockSpec((SEQ, LANES), lambda b, p, j: (lat0 + b, p)),
            pl.BlockSpec((LANES, SEQ), lambda b, p, j: (p, lat0 + b)),
        ],
        out_specs=[pl.BlockSpec((qw, CTX_LEN), lambda b, p, j: (p, b)),
                   pl.BlockSpec((qw, TQ), lambda b, p, j: (p, _lat_out_index(b, j)))],
        compiler_params=_cparams(("arbitrary", "arbitrary", "arbitrary")),
        name="gqa_attention",
    )(q, k, vt, k, vt)


def _swa_kernel(sink_ref, q_ref, kc_ref, vtc_ref, kl_ref, vtl_ref, oc_ref, ol_ref):
    j = pl.program_id(1)
    group = SWA_HEADS // SWA_KV_HEADS
    all_rows = slice(None)
    sink_fn = lambda head: sink_ref[head] * LOG2E

    @pl.when(j == 0)
    def _():
        _attention_units(_head_units(q_ref, all_rows, oc_ref, all_rows, SWA_HEADS, group, kc_ref, vtc_ref,
                                     sink_fn=sink_fn), SHORT_UNIT_WIDTH)

    @pl.when(j > 0)
    def _():
        band = TQ + 2 * SWA_WINDOW
        q0 = (j - 1) * TQ
        k0 = pl.multiple_of(jnp.clip(q0 - SWA_WINDOW, 0, SEQ - band), LANES)
        kpos = k0 + lax.broadcasted_iota(jnp.int32, (band, 1), 0)
        qpos = q0 + lax.broadcasted_iota(jnp.int32, (1, TQ), 1)
        mask = jnp.where(jnp.abs(qpos - kpos) <= SWA_WINDOW, 0.0, MASK_VALUE).astype(F32)
        _attention_units(_head_units(q_ref, all_rows, ol_ref, all_rows, SWA_HEADS, group, kc_ref, vtc_ref,
                                     (kl_ref, vtl_ref, k0, band, band),
                                     bias_fn=lambda head, r0, n: mask[r0:r0 + n, :], sink_fn=sink_fn),
                         SHORT_UNIT_WIDTH)


def _swa_attention(q, k, vt, sinks):
    lat0 = N_CTX_ROWS // SEQ
    n_qt = SEQ // TQ
    return pl.pallas_call(
        _swa_kernel,
        out_shape=[jax.ShapeDtypeStruct((D_MODEL, N_CTX_ROWS), BF16),
                   jax.ShapeDtypeStruct((D_MODEL, N_LAT_ROWS), BF16)],
        grid=(BATCH, n_qt + 1),
        in_specs=[
            pl.BlockSpec(memory_space=pltpu.SMEM),
            pl.BlockSpec((TQ, D_MODEL), lambda b, j: (_q_tile_index(b, j), 0)),
            pl.BlockSpec((CTX_LEN, LANES), lambda b, j: (b, 0)),
            pl.BlockSpec((LANES, CTX_LEN), lambda b, j: (0, b)),
            pl.BlockSpec((SEQ, LANES), lambda b, j: (lat0 + b, 0)),
            pl.BlockSpec((LANES, SEQ), lambda b, j: (0, lat0 + b)),
        ],
        out_specs=[pl.BlockSpec((D_MODEL, CTX_LEN), lambda b, j: (0, b)),
                   pl.BlockSpec((D_MODEL, TQ), lambda b, j: (0, _lat_out_index(b, j)))],
        compiler_params=_cparams(("arbitrary", "arbitrary")),
        name="swa_attention",
    )(sinks, q, k, vt, k, vt)


def _scan_rows(a, b, h, reverse):
    sub = 8
    groups = a.shape[0] // sub
    a3 = a.reshape(groups, sub, D_MODEL)
    b3 = b.reshape(groups, sub, D_MODEL)
    row = lax.broadcasted_iota(jnp.int32, (1, sub, 1), 1)
    step = 1
    while step < sub:
        keep = (row < sub - step) if reverse else (row >= step)
        shift = sub - step if reverse else step
        a_prev = jnp.where(keep, pltpu.roll(a3, shift, 1), 1.0)
        b_prev = jnp.where(keep, pltpu.roll(b3, shift, 1), 0.0)
        b3 = b3 + a3 * b_prev
        a3 = a3 * a_prev
        step *= 2
    out = [None] * groups
    for g in (range(groups - 1, -1, -1) if reverse else range(groups)):
        hg = a3[g] * h + b3[g]
        out[g] = hg
        h = hg[0:1, :] if reverse else hg[sub - 1:sub, :]
    return jnp.concatenate(out, axis=0), h


def _lru_kernel(xc_ref, xl_ref, cw_ref, cb_ref, wbd_ref, ba_ref, bx_ref, lam_ref, o_ref, conv_s):
    chunk = CTX_LEN
    n_lat = SEQ // chunk
    sp = jax.nn.softplus(-lam_ref[...])

    def conv(before, cur, after):
        ext = jnp.concatenate([before, cur, after], axis=0)
        acc = ext * cw_ref[2:3, :]
        acc = acc + pltpu.roll(ext, 2, 0) * cw_ref[0:1, :]
        acc = acc + pltpu.roll(ext, 1, 0) * cw_ref[1:2, :]
        acc = acc + pltpu.roll(ext, chunk + 15, 0) * cw_ref[3:4, :]
        return acc[8:8 + chunk, :] + cb_ref[...]

    def gates(xc, d):
        xb = xc.astype(BF16)
        pre = []
        for kind in range(2):
            cols = [jnp.dot(xb[:, 256 * c:256 * (c + 1)], wbd_ref[d, kind, c], preferred_element_type=F32)
                    for c in range(D_MODEL // 256)]
            pre.append(jnp.concatenate(cols, axis=-1))
        t_r = jnp.tanh(0.5 * (pre[0] + ba_ref[d:d + 1, :]))
        ig = 0.5 * jnp.tanh(0.5 * (pre[1] + bx_ref[d:d + 1, :])) + 0.5
        half_decay = (-0.5 * LRU_C) * sp[d:d + 1, :]
        log_a = half_decay * t_r + half_decay
        a = jnp.exp(log_a)
        return a, jnp.sqrt(-jnp.tanh(log_a) * (a * a + 1.0)) * (ig * xc)

    def lat_chunk(c, h, d, reverse):
        start = pl.multiple_of(c * chunk, chunk)
        rows = pl.ds(start, chunk)
        kept = pl.ds(CTX_LEN + start, chunk)
        if d == 0:
            before = jnp.where(c > 0, xl_ref[pl.ds(pl.multiple_of(jnp.maximum(start - 8, 0), 8), 8), :], 0.0)
            after = jnp.where(c < n_lat - 1,
                              xl_ref[pl.ds(pl.multiple_of(jnp.minimum(start + chunk, SEQ - 8), 8), 8), :], 0.0)
            xc = conv(before, xl_ref[rows, :], after)
            conv_s[kept, :] = xc
        else:
            xc = conv_s[kept, :]
        states, h = _scan_rows(*gates(xc, d), h, reverse)
        if d == 0:
            o_ref[rows, :] = states
        else:
            o_ref[rows, :] = o_ref[rows, :] + states
        return h

    zeros = jnp.zeros((8, D_MODEL), F32)
    conv_s[0:CTX_LEN, :] = conv(zeros, xc_ref[...], zeros)
    for d in range(2):
        reverse = d == 1
        _, h = _scan_rows(*gates(conv_s[0:CTX_LEN, :], d), jnp.zeros((1, D_MODEL), F32), reverse)
        lax.fori_loop(0, n_lat, lambda i, h: lat_chunk((n_lat - 1 - i) if reverse else i, h, d, reverse), h)


def _lru_scan(xr, conv_w, conv_b, wbd, b_a, b_x, lam):
    lat0 = N_CTX_ROWS // SEQ
    const2 = lambda b: (0, 0)
    return pl.pallas_call(
        _lru_kernel,
        out_shape=jax.ShapeDtypeStruct((N_LAT_ROWS, D_MODEL), F32),
        grid=(BATCH,),
        in_specs=[
            pl.BlockSpec((CTX_LEN, D_MODEL), lambda b: (b, 0)),
            pl.BlockSpec((SEQ, D_MODEL), lambda b: (lat0 + b, 0)),
            pl.BlockSpec((CONV_WIDTH, D_MODEL), const2),
            pl.BlockSpec((1, D_MODEL), const2),
            pl.BlockSpec(wbd.shape, lambda b: (0, 0, 0, 0, 0)),
            pl.BlockSpec((2, D_MODEL), const2),
            pl.BlockSpec((2, D_MODEL), const2),
            pl.BlockSpec((2, D_MODEL), const2),
        ],
        out_specs=pl.BlockSpec((SEQ, D_MODEL), lambda b: (b, 0)),
        scratch_shapes=[pltpu.VMEM((CTX_LEN + SEQ, D_MODEL), F32)],
        compiler_params=_cparams(("arbitrary",)),
        name="rglru_scan",
    )(xr, xr, conv_w, conv_b.reshape(1, D_MODEL), wbd, b_a, b_x, lam)


def _ffn_kernel(*refs, lru, final, split_in):
    if split_in:
        h = _tile_rows(refs[0], refs[1], pl.program_id(0))
        refs = refs[1:]
    else:
        h = refs[0][...]
    mod_ref, g_ref, wo_ref, win_ref, wout_ref = refs[1:6]
    k = 6
    if lru:
        hs_ref, gl_ref = refs[k:k + 2]
        k += 2
        o_mix = (hs_ref[...] * gl_ref[...]).astype(BF16)
        y = jnp.dot(o_mix, wo_ref[...], preferred_element_type=F32)
    else:
        oc_ref, ol_ref = refs[k:k + 2]
        k += 2
        o_t = jnp.where(pl.program_id(0) < N_CTX_TILES, oc_ref[...], ol_ref[...])
        y = lax.dot_general(o_t, wo_ref[...], (((0,), (0,)), ((), ())), preferred_element_type=F32)
    if final:
        gf_ref = refs[k]
        k += 1
    out_ref = refs[k]
    h1 = h + mod_ref[2:3, :] * y
    xn = _ada_norm(h1, g_ref[...], mod_ref[3:4, :], mod_ref[4:5, :]).astype(BF16)
    acc = jnp.zeros(h1.shape, F32)
    for c0 in range(0, FFN_HIDDEN, FFN_CHUNK):
        a = jnp.dot(xn, win_ref[:, c0:c0 + FFN_CHUNK], preferred_element_type=F32)
        g = jnp.dot(xn, win_ref[:, FFN_HIDDEN + c0:FFN_HIDDEN + c0 + FFN_CHUNK], preferred_element_type=F32)
        u = ((a * jax.nn.sigmoid(a)) * g).astype(BF16)
        acc = acc + jnp.dot(u, wout_ref[c0:c0 + FFN_CHUNK, :], preferred_element_type=F32)
    h2 = h1 + mod_ref[5:6, :] * acc
    if final:
        ms = jnp.mean(h2 * h2, axis=-1, keepdims=True)
        h2 = (h2 * lax.rsqrt(ms + EPS)) * gf_ref[...]
    out_ref[...] = h2


def _mix_ffn(hx, mod_l, g_ffn, w_o, w_in, w_out, mix_inputs, lru=False, final_g=None):
    final = final_g is not None
    t0 = N_CTX_TILES if final else 0
    n_tiles = N_ROWS // TM - t0
    resident = functools.partial(pl.BlockSpec, pipeline_mode=pl.Buffered(1))
    split_in = isinstance(hx, tuple)
    assert not (split_in and final)
    if split_in:
        in_specs, args = _tile_row_specs(*hx)
    else:
        in_specs, args = [pl.BlockSpec((TM, D_MODEL), lambda i: (i + t0, 0))], [hx]
    in_specs += [
        pl.BlockSpec((None, N_MOD, D_MODEL), lambda i: (_tile_mod_row(i + t0), 0, 0)),
        pl.BlockSpec((1, D_MODEL), lambda i: (0, 0)),
        resident(w_o.shape, lambda i: (0, 0)),
        resident(w_in.shape, lambda i: (0, 0)),
        resident(w_out.shape, lambda i: (0, 0)),
    ]
    args += [mod_l, g_ffn.reshape(1, D_MODEL), w_o, w_in, w_out]
    if lru:
        for m, off in mix_inputs:
            in_specs.append(pl.BlockSpec((TM, D_MODEL), lambda i, off=off: (i + off, 0)))
            args.append(m)
    else:
        o_ctx, o_lat = mix_inputs
        in_specs.append(pl.BlockSpec((D_MODEL, TM), lambda i: (0, jnp.minimum(i, N_CTX_TILES - 1))))
        in_specs.append(pl.BlockSpec((D_MODEL, TM), lambda i: (0, jnp.maximum(i - N_CTX_TILES, 0))))
        args += [o_ctx, o_lat]
    if final:
        in_specs.append(pl.BlockSpec((1, D_MODEL), lambda i: (0, 0)))
        args.append(final_g.reshape(1, D_MODEL))
    return pl.pallas_call(
        functools.partial(_ffn_kernel, lru=lru, final=final, split_in=split_in),
        out_shape=jax.ShapeDtypeStruct((n_tiles * TM, D_MODEL), F32),
        grid=(n_tiles,),
        in_specs=in_specs,
        out_specs=pl.BlockSpec((TM, D_MODEL), lambda i: (i, 0)),
        compiler_params=_cparams(("arbitrary",)),
        name="mix_ffn",
    )(*args)


def _rope_tables():
    f32 = np.float32
    t = np.arange(SEQ)
    row = (t // GRID_W).astype(f32)
    col = (t % GRID_W).astype(f32)
    half = HEAD_DIM // 2
    inv_freq = (f32(1.0) / (f32(ROPE_THETA) ** (np.arange(0, half, 2, dtype=f32) / f32(half)))).astype(f32)
    d = np.arange(LANES) % HEAD_DIM
    freq = inv_freq[(d % half) % (half // 2)]
    pos = np.where((d < half)[None, :], row[:, None], col[:, None])
    ang = (pos * freq[None, :]).astype(f32)
    first = ((d % half) < half // 2)[None, :]
    cos, sin = np.cos(ang).astype(f32), np.sin(ang).astype(f32)
    ident = np.ones((TM, LANES), f32)
    zero = np.zeros((TM, LANES), f32)
    return jnp.asarray(np.concatenate([ident, cos])), jnp.asarray(np.concatenate([zero, np.where(first, -sin, sin)]))


def _na_bias_strips(rpb):
    n_drow = NA_MASKED
    n_dcol = 2 * NA_WIN_W - 1
    span = 2 * GRID_W
    left = GRID_W - NA_WIN_W
    w = jnp.pad(rpb.astype(F32), ((0, 0), (0, 0), (left, span - left - n_dcol)))
    flat = jnp.tile(w, (1, 1, GRID_W))[..., GRID_W - 1:GRID_W - 1 + GRID_W * (span - 1)]
    toep = flat.reshape(NA_HEADS, n_drow, GRID_W, span - 1)[..., :GRID_W]
    qcol = np.arange(GRID_W)
    col_start = np.clip(qcol - NA_WIN_W // 2, 0, GRID_W - NA_WIN_W)
    col_in = (qcol[None, :] >= col_start[:, None]) & (qcol[None, :] < col_start[:, None] + NA_WIN_W)
    blocks = jnp.where(col_in[None, None], toep * LOG2E, MASK_VALUE)
    blocks = jnp.concatenate([blocks, jnp.full((NA_HEADS, 1, GRID_W, GRID_W), MASK_VALUE, F32)], axis=1)
    blocks_t = blocks.transpose(0, 1, 3, 2)
    seqs, _ = _na_strip_plan()
    strips = jnp.stack([jnp.concatenate([blocks_t[:, i] for i in seq], axis=-1) for seq in seqs], axis=1)
    return strips.reshape(NA_HEADS // 2, 2, len(seqs), GRID_W, len(seqs[0]) * GRID_W)


def _block_diag_256(w):
    per = 256 // LRU_BLOCK_DIM
    w4 = w.reshape(LRU_BLOCKS // per, per, LRU_BLOCK_DIM, LRU_BLOCK_DIM)
    eye = jnp.eye(per, dtype=w.dtype)
    return jnp.einsum("cide,ij->cidje", w4, eye).reshape(LRU_BLOCKS // per, 256, 256)


def kernel(x, c, ctx, c_ctx, ada_w, ada_b, norm_mix, norm_ffn, norm_final, ffn_w_in, ffn_w_out, na_w_qkv, na_rpb, na_w_o, gqa_w_qkv, gqa_q_gain, gqa_k_gain, gqa_w_o, swa_w_qkv, swa_sinks, swa_w_o, lru_w_in, lru_conv_w, lru_conv_b, lru_w_a, lru_b_a, lru_w_x, lru_b_x, lru_lam, lru_w_out):
    assert x.shape == (BATCH, SEQ, D_MODEL) and ctx.shape == (BATCH, CTX_LEN, D_MODEL)
    hx = (ctx.reshape(N_CTX_ROWS, D_MODEL), x.reshape(N_LAT_ROWS, D_MODEL))
    c_all = jnp.concatenate([c, c_ctx[None, :], jnp.zeros((MOD_ROWS - BATCH - 1, D_MODEL), F32)], axis=0)
    mod = _modulation(c_all, ada_w, ada_b).reshape(DEPTH, MOD_ROWS, N_MOD, D_MODEL)
    rope_tabs = _rope_tables()
    q_scale = HEAD_DIM ** -0.5 * LOG2E
    lat_tile0 = N_CTX_TILES

    def qkv_split(w, n_q, n_kv):
        wq_k = w[:, :(n_q + n_kv) * HEAD_DIM].astype(BF16)
        return wq_k, w[:, (n_q + n_kv) * HEAD_DIM:].T.astype(BF16)

    w_qk, w_vt = qkv_split(na_w_qkv[0], NA_HEADS, NA_HEADS)
    q, k, vt, w_in, w_out = _project(hx, mod[0], norm_mix[0], w_qk,
                                     ((D_MODEL, "plain", None, q_scale), (D_MODEL, "plain", None, 1.0)), (BF16, BF16),
                                     w_vt=w_vt, ffn_w=(ffn_w_in, ffn_w_out, 0))
    o = _na_attention(q, k, vt, _na_bias_strips(na_rpb[0]))
    hx = _mix_ffn(hx, mod[0], norm_ffn[0], na_w_o[0].astype(BF16), w_in, w_out, o)

    kvw = GQA_KV_HEADS * HEAD_DIM
    gains = jnp.stack([jnp.tile(g_, 2) for g_ in (gqa_q_gain[0], gqa_k_gain[0],
                                                  _swap_rotary_halves(gqa_q_gain[0]), _swap_rotary_halves(gqa_k_gain[0]))])
    w_qk, w_vt = qkv_split(gqa_w_qkv[0], GQA_HEADS, GQA_KV_HEADS)
    q, k, vt, w_in, w_out = _project(hx, mod[1], norm_mix[1], w_qk,
                                     ((D_MODEL, "rope", 0, q_scale), (kvw, "rope", 1, 1.0)), (BF16, BF16),
                                     rope_tabs, gains, w_vt=w_vt, ffn_w=(ffn_w_in, ffn_w_out, 1))
    o = _gqa_attention(q, k, vt)
    hx = _mix_ffn(hx, mod[1], norm_ffn[1], gqa_w_o[0].astype(BF16), w_in, w_out, o)

    kvw = SWA_KV_HEADS * HEAD_DIM
    w_qk, w_vt = qkv_split(swa_w_qkv[0], SWA_HEADS, SWA_KV_HEADS)
    q, k, vt, w_in, w_out = _project(hx, mod[2], norm_mix[2], w_qk,
                                     ((D_MODEL, "rope", None, q_scale), (kvw, "rope", None, 1.0)), (BF16, BF16),
                                     rope_tabs, jnp.ones((4, LANES), F32), w_vt=w_vt, ffn_w=(ffn_w_in, ffn_w_out, 2))
    o = _swa_attention(q, k, vt, swa_sinks[0])
    hx = _mix_ffn(hx, mod[2], norm_ffn[2], swa_w_o[0].astype(BF16), w_in, w_out, o)

    xr, gl, w_in, w_out = _project(hx, mod[3], norm_mix[3], lru_w_in[0].astype(BF16),
                                   ((D_MODEL, "plain", None, 1.0), (D_MODEL, "gelu", None, 1.0)), (F32, F32),
                                   ffn_w=(ffn_w_in, ffn_w_out, 3))
    wbd = jnp.stack([jnp.stack([_block_diag_256(lru_w_a[0, d]), _block_diag_256(lru_w_x[0, d])])
                     for d in range(2)]).astype(BF16)
    hs = _lru_scan(xr, lru_conv_w[0], lru_conv_b[0], wbd, lru_b_a[0], lru_b_x[0], lru_lam[0])
    out = _mix_ffn(hx, mod[3], norm_ffn[3], lru_w_out[0].astype(BF16), w_in, w_out,
                   [(hs, 0), (gl, lat_tile0)], lru=True, final_g=norm_final)
    return out.reshape(BATCH, SEQ, D_MODEL)
```

```python
import functools

import jax
import jax.numpy as jnp
import numpy as np
from jax import lax
from jax.experimental import pallas as pl
from jax.experimental.pallas import tpu as pltpu

F32 = jnp.float32
BF16 = jnp.bfloat16

D_MODEL = 1024
BATCH = 8
SEQ = 2048
DEPTH = 4
GRID_W = 64
CTX_LEN = 256
HEAD_DIM = 64
ROPE_THETA = 10000.0
NA_HEADS = 16
NA_WIN_H = 8
NA_WIN_W = 16
GQA_HEADS = 16
GQA_KV_HEADS = 4
SWA_HEADS = 16
SWA_KV_HEADS = 2
SWA_WINDOW = 128
LRU_BLOCKS = 16
LRU_BLOCK_DIM = D_MODEL // LRU_BLOCKS
CONV_WIDTH = 4
LRU_C = 8.0
FFN_HIDDEN = 2816
N_MOD = 6
EPS = 1e-6
MASK_VALUE = -1e30
LOG2E = 1.4426950408889634

N_CTX_ROWS = BATCH * CTX_LEN
N_LAT_ROWS = BATCH * SEQ
N_ROWS = N_CTX_ROWS + N_LAT_ROWS
MOD_ROWS = 16
CTX_MOD_ROW = BATCH

LANES = 128
TM = 512
TQ = 256
FFN_CHUNK = 256
MOD_K_TILE = 256
FFN_CAST_ROWS = 128
GQA_KEY_CHUNK = 256
SHORT_UNIT_WIDTH = 2
VMEM_LIMIT = 56 * 1024 * 1024

N_CTX_TILES = N_CTX_ROWS // TM
TILES_PER_BATCH = SEQ // TM
NA_Q_ROWS = TQ // GRID_W
NA_K_ROWS = 12
NA_KEYS = NA_K_ROWS * GRID_W
NA_KEY_CHUNK = NA_KEYS // 2
NA_MASKED = 2 * NA_WIN_H - 1

_NT = (((1,), (1,)), ((), ()))


def _cparams(sem):
    return pltpu.CompilerParams(dimension_semantics=sem, vmem_limit_bytes=VMEM_LIMIT)


def _tile_mod_row(i):
    return jnp.where(i < N_CTX_TILES, CTX_MOD_ROW, (i - N_CTX_TILES) // TILES_PER_BATCH)


def _tile_pos_block(i):
    return jnp.where(i < N_CTX_TILES, 0, 1 + (i - N_CTX_TILES) % TILES_PER_BATCH)


def _ada_norm(x, g, shift, scale):
    ms = jnp.mean(x * x, axis=-1, keepdims=True)
    y = x * lax.rsqrt(ms + EPS)
    return (y * g) * (1.0 + scale) + shift


def _mod_kernel(c_ref, w_ref, b_ref, o_ref):
    kk = pl.program_id(1)
    c = c_ref[...]
    sc = (c * jax.nn.sigmoid(c)).astype(BF16)
    part = jnp.dot(sc, w_ref[0].astype(BF16), preferred_element_type=F32)

    @pl.when(kk == 0)
    def _():
        o_ref[0] = part + b_ref[0]

    @pl.when(kk > 0)
    def _():
        o_ref[0] = o_ref[0] + part


def _modulation(c_all, ada_w, ada_b):
    tk = MOD_K_TILE
    n = N_MOD * D_MODEL
    return pl.pallas_call(
        _mod_kernel,
        out_shape=jax.ShapeDtypeStruct((DEPTH, MOD_ROWS, n), F32),
        grid=(DEPTH, D_MODEL // tk),
        in_specs=[
            pl.BlockSpec((MOD_ROWS, tk), lambda l, k: (0, k)),
            pl.BlockSpec((1, tk, n), lambda l, k: (l, k, 0)),
            pl.BlockSpec((1, 1, n), lambda l, k: (l, 0, 0)),
        ],
        out_specs=pl.BlockSpec((1, MOD_ROWS, n), lambda l, k: (l, 0, 0)),
        compiler_params=_cparams(("arbitrary", "arbitrary")),
        name="modulation",
    )(c_all, ada_w, ada_b.reshape(DEPTH, 1, n))


def _tile_rows(ctx_ref, lat_ref, tile):
    return jnp.where(tile < N_CTX_TILES, ctx_ref[...], lat_ref[...])


def _tile_row_specs(ctx_rows, lat_rows, t0=0):
    return [pl.BlockSpec((TM, D_MODEL), lambda i: (jnp.minimum(i + t0, N_CTX_TILES - 1), 0)),
            pl.BlockSpec((TM, D_MODEL), lambda i: (jnp.maximum(i + t0 - N_CTX_TILES, 0), 0))], [ctx_rows, lat_rows]


def _proj_kernel(*refs, segs, has_rope, has_vt, split_in, cast_ffn):
    if split_in:
        x = _tile_rows(refs[0], refs[1], pl.program_id(0))
        refs = refs[1:]
    else:
        x = refs[0][...]
    mod_ref, g_ref, w_ref = refs[1:4]
    k = 4
    if has_rope:
        wsw_ref, cos_ref, sin_ref, gain_ref = refs[k:k + 4]
        k += 4
    if has_vt:
        wvt_ref = refs[k]
        k += 1
    if cast_ffn:
        win_ref, wout_ref = refs[k:k + 2]
        k += 2
        refs[-2][...] = win_ref[...].astype(BF16)
        refs[-1][...] = wout_ref[...].astype(BF16)
    out_refs = refs[k:]
    xn = _ada_norm(x, g_ref[...], mod_ref[0:1, :], mod_ref[1:2, :]).astype(BF16)
    lane = lax.broadcasted_iota(jnp.int32, (1, LANES), 1)
    lo = lane < HEAD_DIM
    col = 0
    for seg, o_ref in zip(segs, out_refs):
        width, kind, norm_idx, scale = seg
        chunk = 256
        for c0 in range(0, width, chunk):
            cw = min(chunk, width - c0)
            y = jnp.dot(xn, w_ref[:, col + c0:col + c0 + cw], preferred_element_type=F32)
            if kind == "gelu":
                y = jax.nn.gelu(y, approximate=True)
            elif kind == "rope":
                ysw = jnp.dot(xn, wsw_ref[:, col + c0:col + c0 + cw], preferred_element_type=F32)
                cos_t, sin_t = cos_ref[...], sin_ref[...]
                if norm_idx is not None:
                    cos_t = cos_t * gain_ref[norm_idx:norm_idx + 1, :]
                    sin_t = sin_t * gain_ref[2 + norm_idx:3 + norm_idx, :]
                parts = []
                for b0 in range(0, cw, LANES):
                    blk = y[:, b0:b0 + LANES]
                    rot = blk * cos_t + ysw[:, b0:b0 + LANES] * sin_t
                    if norm_idx is not None:
                        ss = blk * blk
                        s_lo = jnp.sum(jnp.where(lo, ss, 0.0), axis=-1, keepdims=True)
                        s_hi = jnp.sum(jnp.where(lo, 0.0, ss), axis=-1, keepdims=True)
                        ms = jnp.where(lo, s_lo, s_hi) * (1.0 / HEAD_DIM)
                        rot = rot * lax.rsqrt(ms + EPS)
                    parts.append(rot)
                y = parts[0] if len(parts) == 1 else jnp.concatenate(parts, axis=-1)
            if scale != 1.0:
                y = y * scale
            o_ref[:, c0:c0 + cw] = y.astype(o_ref.dtype)
        col += width
    if has_vt:
        vt_ref = out_refs[len(segs)]
        for r0 in range(0, wvt_ref.shape[0], 256):
            rw = min(256, wvt_ref.shape[0] - r0)
            vt = lax.dot_general(wvt_ref[r0:r0 + rw, :], xn, _NT, preferred_element_type=F32)
            vt_ref[r0:r0 + rw, :] = vt.astype(vt_ref.dtype)


def _swap_rotary_halves(a):
    quarter = HEAD_DIM // 4
    return a.reshape(a.shape[:-1] + (-1, 2, quarter))[..., ::-1, :].reshape(a.shape)


def _project(hx, mod_l, g, w, segs, out_dtypes, rope_tabs=None, gains=None, w_vt=None, ffn_w=None):
    n_tot = sum(s[0] for s in segs)
    resident = functools.partial(pl.BlockSpec, pipeline_mode=pl.Buffered(1))
    split_in = isinstance(hx, tuple)
    if split_in:
        in_specs, args = _tile_row_specs(*hx)
    else:
        in_specs, args = [pl.BlockSpec((TM, D_MODEL), lambda i: (i, 0))], [hx]
    in_specs += [
        pl.BlockSpec((None, N_MOD, D_MODEL), lambda i: (_tile_mod_row(i), 0, 0)),
        pl.BlockSpec((1, D_MODEL), lambda i: (0, 0)),
        resident((D_MODEL, n_tot), lambda i: (0, 0)),
    ]
    args += [mod_l, g.reshape(1, D_MODEL), w]
    has_rope = rope_tabs is not None
    if has_rope:
        in_specs.append(resident((D_MODEL, n_tot), lambda i: (0, 0)))
        args.append(_swap_rotary_halves(w))
        for t in rope_tabs:
            in_specs.append(pl.BlockSpec((TM, LANES), lambda i: (_tile_pos_block(i), 0)))
            args.append(t)
        in_specs.append(pl.BlockSpec(gains.shape, lambda i: (0, 0)))
        args.append(gains)
    out_shape = [jax.ShapeDtypeStruct((N_ROWS, s[0]), dt) for s, dt in zip(segs, out_dtypes)]
    out_specs = [pl.BlockSpec((TM, s[0]), lambda i: (i, 0)) for s in segs]
    if w_vt is not None:
        in_specs.append(resident(w_vt.shape, lambda i: (0, 0)))
        args.append(w_vt)
        out_shape.append(jax.ShapeDtypeStruct((w_vt.shape[0], N_ROWS), BF16))
        out_specs.append(pl.BlockSpec((w_vt.shape[0], TM), lambda i: (0, i)))
    if ffn_w is not None:
        f_in, f_out, layer = ffn_w
        n_blk = FFN_HIDDEN // FFN_CAST_ROWS
        blk = lambda i: jnp.minimum(i, n_blk - 1)
        cols = f_in.shape[2] // n_blk
        in_specs += [pl.BlockSpec((None, D_MODEL, cols), lambda i: (layer, 0, blk(i))),
                     pl.BlockSpec((None, FFN_CAST_ROWS, D_MODEL), lambda i: (layer, blk(i), 0))]
        args += [f_in, f_out]
        out_shape += [jax.ShapeDtypeStruct(f_in.shape[1:], BF16), jax.ShapeDtypeStruct(f_out.shape[1:], BF16)]
        out_specs += [pl.BlockSpec((D_MODEL, cols), lambda i: (0, blk(i))),
                      pl.BlockSpec((FFN_CAST_ROWS, D_MODEL), lambda i: (blk(i), 0))]
    return pl.pallas_call(
        functools.partial(_proj_kernel, segs=segs, has_rope=has_rope, has_vt=w_vt is not None, split_in=split_in,
                          cast_ffn=ffn_w is not None),
        out_shape=out_shape,
        grid=(N_ROWS // TM,),
        in_specs=in_specs,
        out_specs=out_specs,
        compiler_params=_cparams(("arbitrary",)),
        name="ada_proj",
    )(*args)


def _head_query(q_ref, rows, head, kv_half):
    pair, half = divmod(head, 2)
    q = q_ref[rows, pair * LANES:(pair + 1) * LANES]
    if half != kv_half:
        q = pltpu.roll(q, HEAD_DIM, 1)
    lane = lax.broadcasted_iota(jnp.int32, (1, LANES), 1)
    keep = (lane < HEAD_DIM) if kv_half == 0 else (lane >= HEAD_DIM)
    return jnp.where(keep, q, jnp.zeros_like(q))


def _attention_units(units, width=1):
    groups = [units[i:i + width] for i in range(0, len(units), width)]
    st1, st2 = [], []
    for idx in range(len(groups) + 2):
        cur = groups[idx] if idx < len(groups) else []
        qs = [u[0]() for u in cur]
        cur_s = [[] for _ in cur]
        cur_m8 = [None for _ in cur]
        new_e = [[] for _ in st1]
        new_d8 = [jnp.zeros((8, TQ), F32) for _ in st1]
        acc = [jnp.zeros((HEAD_DIM, TQ), F32) for _ in st2]
        n_steps = max([len(u[1]) for u in cur] + [len(p[0][1]) for p in st1] + [len(p[0][1]) for p in st2] + [0])
        for c in range(n_steps):
            for i, u in enumerate(cur):
                if c < len(u[1]):
                    k_fn, _, bias_fn = u[1][c]
                    s = lax.dot_general(k_fn(), qs[i], _NT, preferred_element_type=F32)
                    if bias_fn is not None:
                        s = s + bias_fn()
                    cur_s[i].append(s)
                    mc = jnp.max(s.reshape(s.shape[0] // 8, 8, TQ), axis=0)
                    cur_m8[i] = mc if cur_m8[i] is None else jnp.maximum(cur_m8[i], mc)
            for i, (u, p_s, p_m) in enumerate(st1):
                if c < len(u[1]):
                    e = jnp.exp2(p_s[c] - p_m)
                    new_d8[i] = new_d8[i] + jnp.sum(e.reshape(e.shape[0] // 8, 8, TQ), axis=0)
                    new_e[i].append(e.astype(BF16))
            for i, (u, p_e, p_d8, p_m) in enumerate(st2):
                if c < len(u[1]):
                    acc[i] = acc[i] + jnp.dot(u[1][c][1](), p_e[c], preferred_element_type=F32)
        for i, (u, p_e, p_d8, p_m) in enumerate(st2):
            den = jnp.sum(p_d8, axis=0, keepdims=True)
            if u[2] is not None:
                den = den + jnp.exp2(u[2] - p_m)
            u[3](acc[i] / den)
        st2 = [(u, new_e[i], new_d8[i], p_m) for i, (u, p_s, p_m) in enumerate(st1)]
        st1 = []
        for i, u in enumerate(cur):
            m = jnp.max(cur_m8[i], axis=0, keepdims=True)
            if u[2] is not None:
                m = jnp.maximum(m, u[2])
            st1.append((u, cur_s[i], m))


def _store_head(o_ref, head, cols):
    def store(o):
        o_ref[head * HEAD_DIM:(head + 1) * HEAD_DIM, cols] = o.astype(o_ref.dtype)
    return store


def _chunks(k_ref, vt_ref, vrows, k0, n_keys, chunk, bias_fn=None):
    out = []
    for r0 in range(0, n_keys, chunk):
        n = min(chunk, n_keys - r0)
        out.append((
            functools.partial(lambda r0, n: k_ref[pl.ds(k0 + r0, n), :], r0, n),
            functools.partial(lambda r0, n: vt_ref[vrows, pl.ds(k0 + r0, n)], r0, n),
            None if bias_fn is None else functools.partial(bias_fn, r0, n)))
    return out


def _head_units(q_ref, rows, o_ref, cols, n_heads, group, kc_ref, vtc_ref, lat=None, bias_fn=None, sink_fn=None):
    units = []
    for head in range(n_heads):
        kv_half = (head // group) % 2
        vrows = slice(kv_half * HEAD_DIM, (kv_half + 1) * HEAD_DIM)
        chunks = _chunks(kc_ref, vtc_ref, vrows, 0, CTX_LEN, CTX_LEN)
        if lat is not None:
            k_ref, vt_ref, k0, n_keys, chunk = lat
            chunks += _chunks(k_ref, vt_ref, vrows, k0, n_keys, chunk,
                              None if bias_fn is None else functools.partial(bias_fn, head))
        units.append((functools.partial(_head_query, q_ref, rows, head, kv_half), chunks,
                      None if sink_fn is None else sink_fn(head), _store_head(o_ref, head, cols)))
    return units


def _na_row_index():
    rows = SEQ // GRID_W
    idx = []
    for q0, k0 in ((0, 0), (NA_Q_ROWS, 0), (rows - NA_Q_ROWS, rows - NA_K_ROWS)):
        qrow = q0 + np.arange(NA_Q_ROWS)
        krow = k0 + np.arange(NA_K_ROWS)
        r0 = np.clip(qrow - NA_WIN_H // 2, 0, rows - NA_WIN_H)
        row_in = (krow[None, :] >= r0[:, None]) & (krow[None, :] < r0[:, None] + NA_WIN_H)
        idx.append(np.where(row_in, krow[None, :] - qrow[:, None] + NA_WIN_H - 1, NA_MASKED))
    return np.stack(idx)


def _na_strip_plan():
    lo = NA_WIN_H - 1 - NA_WIN_H // 2
    interior = [NA_MASKED] * NA_Q_ROWS + list(range(lo + NA_WIN_H - 1, lo - 1, -1)) + [NA_MASKED] * (NA_Q_ROWS - 1)
    edge = list(range(NA_MASKED - 1, -1, -1))
    strips = [interior + [NA_MASKED], [NA_MASKED] + interior, edge + [NA_MASKED], [NA_MASKED] + edge]
    idx = _na_row_index()
    plan = []
    for case in range(idx.shape[0]):
        rows = []
        for kr in range(NA_K_ROWS):
            want = [int(v) for v in idx[case, :, kr]]
            if all(v == NA_MASKED for v in want):
                rows.append(None)
                continue
            hits = [(s, p) for s, seq in enumerate(strips) for p in range(0, len(seq) - NA_Q_ROWS + 1, 2)
                    if seq[p:p + NA_Q_ROWS] == want]
            rows.append(hits[0])
        plan.append(rows)
    return strips, plan


def _na_fill_bias(strip_ref, bias_s):
    _, plan = _na_strip_plan()
    for head in range(2):
        for case in range(len(plan)):
            for kr, hit in enumerate(plan[case]):
                rows = slice(kr * GRID_W, (kr + 1) * GRID_W)
                if hit is None:
                    bias_s[head, case, rows, :] = jnp.full((GRID_W, TQ), MASK_VALUE, F32)
                else:
                    strip, first = hit
                    bias_s[head, case, rows, :] = strip_ref[head, strip, :, first * GRID_W:first * GRID_W + TQ]


def _na_kernel(qc_ref, ql_ref, kc_ref, vtc_ref, kl_ref, vtl_ref, strip_ref, oc_ref, ol_ref, bias_ref):
    @pl.when(pl.program_id(1) == 0)
    def _():
        _na_fill_bias(strip_ref, bias_ref)

    all_rows = slice(None)
    units = _head_units(qc_ref, all_rows, oc_ref, all_rows, 2, 1, kc_ref, vtc_ref)
    n_tiles = SEQ // TQ
    for j in range(n_tiles):
        case = 0 if j == 0 else (2 if j == n_tiles - 1 else 1)
        k0 = min(max(NA_Q_ROWS * j - NA_WIN_H // 2, 0), SEQ // GRID_W - NA_K_ROWS) * GRID_W
        rows = slice(j * TQ, (j + 1) * TQ)
        units += _head_units(ql_ref, rows, ol_ref, rows, 2, 1, kc_ref, vtc_ref, (kl_ref, vtl_ref, k0, NA_KEYS, NA_KEY_CHUNK),
                             functools.partial(lambda case, head, r0, n: bias_ref[head, case, r0:r0 + n, :], case))
    _attention_units(units)


def _na_attention(q, k, vt, strips):
    lat0 = N_CTX_ROWS // SEQ
    return pl.pallas_call(
        _na_kernel,
        out_shape=[jax.ShapeDtypeStruct((D_MODEL, N_CTX_ROWS), BF16),
                   jax.ShapeDtypeStruct((D_MODEL, N_LAT_ROWS), BF16)],
        grid=(NA_HEADS // 2, BATCH),
        in_specs=[
            pl.BlockSpec((CTX_LEN, LANES), lambda p, b: (b, p)),
            pl.BlockSpec((SEQ, LANES), lambda p, b: (lat0 + b, p)),
            pl.BlockSpec((CTX_LEN, LANES), lambda p, b: (b, p)),
            pl.BlockSpec((LANES, CTX_LEN), lambda p, b: (p, b)),
            pl.BlockSpec((SEQ, LANES), lambda p, b: (lat0 + b, p)),
            pl.BlockSpec((LANES, SEQ), lambda p, b: (p, lat0 + b)),
            pl.BlockSpec((None,) + strips.shape[1:], lambda p, b: (p, 0, 0, 0, 0)),
        ],
        out_specs=[pl.BlockSpec((LANES, CTX_LEN), lambda p, b: (p, b)),
                   pl.BlockSpec((LANES, SEQ), lambda p, b: (p, b))],
        scratch_shapes=[pltpu.VMEM((2, 3, NA_KEYS, TQ), F32)],
        compiler_params=_cparams(("arbitrary", "arbitrary")),
        name="na_attention",
    )(q, q, k, vt, k, vt, strips)


def _q_tile_index(b, j):
    n_qt = SEQ // TQ
    return jnp.where(j == 0, b, N_CTX_ROWS // TQ + b * n_qt + j - 1)


def _lat_out_index(b, j):
    return b * (SEQ // TQ) + jnp.maximum(j - 1, 0)


def _gqa_kernel(q_ref, kc_ref, vtc_ref, kl_ref, vtl_ref, oc_ref, ol_ref):
    j = pl.program_id(2)
    group = GQA_HEADS // GQA_KV_HEADS
    all_rows = slice(None)

    @pl.when(j == 0)
    def _():
        _attention_units(_head_units(q_ref, all_rows, oc_ref, all_rows, 2 * group, group, kc_ref, vtc_ref),
                         SHORT_UNIT_WIDTH)

    @pl.when(j > 0)
    def _():
        _attention_units(_head_units(q_ref, all_rows, ol_ref, all_rows, 2 * group, group, kc_ref, vtc_ref,
                                     (kl_ref, vtl_ref, 0, SEQ, GQA_KEY_CHUNK)))


def _gqa_attention(q, k, vt):
    lat0 = N_CTX_ROWS // SEQ
    n_qt = SEQ // TQ
    qw = 2 * (GQA_HEADS // GQA_KV_HEADS) * HEAD_DIM
    return pl.pallas_call(
        _gqa_kernel,
        out_shape=[jax.ShapeDtypeStruct((D_MODEL, N_CTX_ROWS), BF16),
                   jax.ShapeDtypeStruct((D_MODEL, N_LAT_ROWS), BF16)],
        grid=(BATCH, GQA_KV_HEADS // 2, n_qt + 1),
        in_specs=[
            pl.BlockSpec((TQ, qw), lambda b, p, j: (_q_tile_index(b, j), p)),
            pl.BlockSpec((CTX_LEN, LANES), lambda b, p, j: (b, p)),
            pl.BlockSpec((LANES, CTX_LEN), lambda b, p, j: (p, b)),
            pl.BlockSpec((SEQ, LANES), lambda b, p, j: (lat0 + b, p)),
            pl.BlockSpec((LANES, SEQ), lambda b, p, j: (p, lat0 + b)),
        ],
        out_specs=[pl.BlockSpec((qw, CTX_LEN), lambda b, p, j: (p, b)),
                   pl.BlockSpec((qw, TQ), lambda b, p, j: (p, _lat_out_index(b, j)))],
        compiler_params=_cparams(("arbitrary", "arbitrary", "arbitrary")),
        name="gqa_attention",
    )(q, k, vt, k, vt)


def _swa_kernel(sink_ref, q_ref, kc_ref, vtc_ref, kl_ref, vtl_ref, oc_ref, ol_ref):
    j = pl.program_id(1)
    group = SWA_HEADS // SWA_KV_HEADS
    all_rows = slice(None)
    sink_fn = lambda head: sink_ref[head] * LOG2E

    @pl.when(j == 0)
    def _():
        _attention_units(_head_units(q_ref, all_rows, oc_ref, all_rows, SWA_HEADS, group, kc_ref, vtc_ref,
                                     sink_fn=sink_fn), SHORT_UNIT_WIDTH)

    @pl.when(j > 0)
    def _():
        band = TQ + 2 * SWA_WINDOW
        q0 = (j - 1) * TQ
        k0 = pl.multiple_of(jnp.clip(q0 - SWA_WINDOW, 0, SEQ - band), LANES)
        kpos = k0 + lax.broadcasted_iota(jnp.int32, (band, 1), 0)
        qpos = q0 + lax.broadcasted_iota(jnp.int32, (1, TQ), 1)
        mask = jnp.where(jnp.abs(qpos - kpos) <= SWA_WINDOW, 0.0, MASK_VALUE).astype(F32)
        _attention_units(_head_units(q_ref, all_rows, ol_ref, all_rows, SWA_HEADS, group, kc_ref, vtc_ref,
                                     (kl_ref, vtl_ref, k0, band, band),
                                     bias_fn=lambda head, r0, n: mask[r0:r0 + n, :], sink_fn=sink_fn),
                         SHORT_UNIT_WIDTH)


def _swa_attention(q, k, vt, sinks):
    lat0 = N_CTX_ROWS // SEQ
    n_qt = SEQ // TQ
    return pl.pallas_call(
        _swa_kernel,
        out_shape=[jax.ShapeDtypeStruct((D_MODEL, N_CTX_ROWS), BF16),
                   jax.ShapeDtypeStruct((D_MODEL, N_LAT_ROWS), BF16)],
        grid=(BATCH, n_qt + 1),
        in_specs=[
            pl.BlockSpec(memory_space=pltpu.SMEM),
            pl.BlockSpec((TQ, D_MODEL), lambda b, j: (_q_tile_index(b, j), 0)),
            pl.BlockSpec((CTX_LEN, LANES), lambda b, j: (b, 0)),
            pl.BlockSpec((LANES, CTX_LEN), lambda b, j: (0, b)),
            pl.BlockSpec((SEQ, LANES), lambda b, j: (lat0 + b, 0)),
            pl.BlockSpec((LANES, SEQ), lambda b, j: (0, lat0 + b)),
        ],
        out_specs=[pl.BlockSpec((D_MODEL, CTX_LEN), lambda b, j: (0, b)),
                   pl.BlockSpec((D_MODEL, TQ), lambda b, j: (0, _lat_out_index(b, j)))],
        compiler_params=_cparams(("arbitrary", "arbitrary")),
        name="swa_attention",
    )(sinks, q, k, vt, k, vt)


def _scan_rows(a, b, h, reverse):
    sub = 8
    groups = a.shape[0] // sub
    a3 = a.reshape(groups, sub, D_MODEL)
    b3 = b.reshape(groups, sub, D_MODEL)
    row = lax.broadcasted_iota(jnp.int32, (1, sub, 1), 1)
    step = 1
    while step < sub:
        keep = (row < sub - step) if reverse else (row >= step)
        shift = sub - step if reverse else step
        a_prev = jnp.where(keep, pltpu.roll(a3, shift, 1), 1.0)
        b_prev = jnp.where(keep, pltpu.roll(b3, shift, 1), 0.0)
        b3 = b3 + a3 * b_prev
        a3 = a3 * a_prev
        step *= 2
    out = [None] * groups
    for g in (range(groups - 1, -1, -1) if reverse else range(groups)):
        hg = a3[g] * h + b3[g]
        out[g] = hg
        h = hg[0:1, :] if reverse else hg[sub - 1:sub, :]
    return jnp.concatenate(out, axis=0), h


def _lru_kernel(xc_ref, xl_ref, cw_ref, cb_ref, wbd_ref, ba_ref, bx_ref, lam_ref, o_ref, conv_s):
    chunk = CTX_LEN
    n_lat = SEQ // chunk
    sp = jax.nn.softplus(-lam_ref[...])

    def conv(before, cur, after):
        ext = jnp.concatenate([before, cur, after], axis=0)
        acc = ext * cw_ref[2:3, :]
        acc = acc + pltpu.roll(ext, 2, 0) * cw_ref[0:1, :]
        acc = acc + pltpu.roll(ext, 1, 0) * cw_ref[1:2, :]
        acc = acc + pltpu.roll(ext, chunk + 15, 0) * cw_ref[3:4, :]
        return acc[8:8 + chunk, :] + cb_ref[...]

    def gates(xc, d):
        xb = xc.astype(BF16)
        pre = []
        for kind in range(2):
            cols = [jnp.dot(xb[:, 256 * c:256 * (c + 1)], wbd_ref[d, kind, c], preferred_element_type=F32)
                    for c in range(D_MODEL // 256)]
            pre.append(jnp.concatenate(cols, axis=-1))
        t_r = jnp.tanh(0.5 * (pre[0] + ba_ref[d:d + 1, :]))
        ig = 0.5 * jnp.tanh(0.5 * (pre[1] + bx_ref[d:d + 1, :])) + 0.5
        half_decay = (-0.5 * LRU_C) * sp[d:d + 1, :]
        log_a = half_decay * t_r + half_decay
        a = jnp.exp(log_a)
        return a, jnp.sqrt(-jnp.tanh(log_a) * (a * a + 1.0)) * (ig * xc)

    def lat_chunk(c, h, d, reverse):
        start = pl.multiple_of(c * chunk, chunk)
        rows = pl.ds(start, chunk)
        kept = pl.ds(CTX_LEN + start, chunk)
        if d == 0:
            before = jnp.where(c > 0, xl_ref[pl.ds(pl.multiple_of(jnp.maximum(start - 8, 0), 8), 8), :], 0.0)
            after = jnp.where(c < n_lat - 1,
                              xl_ref[pl.ds(pl.multiple_of(jnp.minimum(start + chunk, SEQ - 8), 8), 8), :], 0.0)
            xc = conv(before, xl_ref[rows, :], after)
            conv_s[kept, :] = xc
        else:
            xc = conv_s[kept, :]
        states, h = _scan_rows(*gates(xc, d), h, reverse)
        if d == 0:
            o_ref[rows, :] = states
        else:
            o_ref[rows, :] = o_ref[rows, :] + states
        return h

    zeros = jnp.zeros((8, D_MODEL), F32)
    conv_s[0:CTX_LEN, :] = conv(zeros, xc_ref[...], zeros)
    for d in range(2):
        reverse = d == 1
        _, h = _scan_rows(*gates(conv_s[0:CTX_LEN, :], d), jnp.zeros((1, D_MODEL), F32), reverse)
        lax.fori_loop(0, n_lat, lambda i, h: lat_chunk((n_lat - 1 - i) if reverse else i, h, d, reverse), h)


def _lru_scan(xr, conv_w, conv_b, wbd, b_a, b_x, lam):
    lat0 = N_CTX_ROWS // SEQ
    const2 = lambda b: (0, 0)
    return pl.pallas_call(
        _lru_kernel,
        out_shape=jax.ShapeDtypeStruct((N_LAT_ROWS, D_MODEL), F32),
        grid=(BATCH,),
        in_specs=[
            pl.BlockSpec((CTX_LEN, D_MODEL), lambda b: (b, 0)),
            pl.BlockSpec((SEQ, D_MODEL), lambda b: (lat0 + b, 0)),
            pl.BlockSpec((CONV_WIDTH, D_MODEL), const2),
            pl.BlockSpec((1, D_MODEL), const2),
            pl.BlockSpec(wbd.shape, lambda b: (0, 0, 0, 0, 0)),
            pl.BlockSpec((2, D_MODEL), const2),
            pl.BlockSpec((2, D_MODEL), const2),
            pl.BlockSpec((2, D_MODEL), const2),
        ],
        out_specs=pl.BlockSpec((SEQ, D_MODEL), lambda b: (b, 0)),
        scratch_shapes=[pltpu.VMEM((CTX_LEN + SEQ, D_MODEL), F32)],
        compiler_params=_cparams(("arbitrary",)),
        name="rglru_scan",
    )(xr, xr, conv_w, conv_b.reshape(1, D_MODEL), wbd, b_a, b_x, lam)


def _ffn_kernel(*refs, lru, final, split_in):
    if split_in:
        h = _tile_rows(refs[0], refs[1], pl.program_id(0))
        refs = refs[1:]
    else:
        h = refs[0][...]
    mod_ref, g_ref, wo_ref, win_ref, wout_ref = refs[1:6]
    k = 6
    if lru:
        hs_ref, gl_ref = refs[k:k + 2]
        k += 2
        o_mix = (hs_ref[...] * gl_ref[...]).astype(BF16)
        y = jnp.dot(o_mix, wo_ref[...], preferred_element_type=F32)
    else:
        oc_ref, ol_ref = refs[k:k + 2]
        k += 2
        o_t = jnp.where(pl.program_id(0) < N_CTX_TILES, oc_ref[...], ol_ref[...])
        y = lax.dot_general(o_t, wo_ref[...], (((0,), (0,)), ((), ())), preferred_element_type=F32)
    if final:
        gf_ref = refs[k]
        k += 1
    out_ref = refs[k]
    h1 = h + mod_ref[2:3, :] * y
    xn = _ada_norm(h1, g_ref[...], mod_ref[3:4, :], mod_ref[4:5, :]).astype(BF16)
    acc = jnp.zeros(h1.shape, F32)
    for c0 in range(0, FFN_HIDDEN, FFN_CHUNK):
        a = jnp.dot(xn, win_ref[:, c0:c0 + FFN_CHUNK], preferred_element_type=F32)
        g = jnp.dot(xn, win_ref[:, FFN_HIDDEN + c0:FFN_HIDDEN + c0 + FFN_CHUNK], preferred_element_type=F32)
        u = ((a * jax.nn.sigmoid(a)) * g).astype(BF16)
        acc = acc + jnp.dot(u, wout_ref[c0:c0 + FFN_CHUNK, :], preferred_element_type=F32)
    h2 = h1 + mod_ref[5:6, :] * acc
    if final:
        ms = jnp.mean(h2 * h2, axis=-1, keepdims=True)
        h2 = (h2 * lax.rsqrt(ms + EPS)) * gf_ref[...]
    out_ref[...] = h2


def _mix_ffn(hx, mod_l, g_ffn, w_o, w_in, w_out, mix_inputs, lru=False, final_g=None):
    final = final_g is not None
    t0 = N_CTX_TILES if final else 0
    n_tiles = N_ROWS // TM - t0
    resident = functools.partial(pl.BlockSpec, pipeline_mode=pl.Buffered(1))
    split_in = isinstance(hx, tuple)
    assert not (split_in and final)
    if split_in:
        in_specs, args = _tile_row_specs(*hx)
    else:
        in_specs, args = [pl.BlockSpec((TM, D_MODEL), lambda i: (i + t0, 0))], [hx]
    in_specs += [
        pl.BlockSpec((None, N_MOD, D_MODEL), lambda i: (_tile_mod_row(i + t0), 0, 0)),
        pl.BlockSpec((1, D_MODEL), lambda i: (0, 0)),
        resident(w_o.shape, lambda i: (0, 0)),
        resident(w_in.shape, lambda i: (0, 0)),
        resident(w_out.shape, lambda i: (0, 0)),
    ]
    args += [mod_l, g_ffn.reshape(1, D_MODEL), w_o, w_in, w_out]
    if lru:
        for m, off in mix_inputs:
            in_specs.append(pl.BlockSpec((TM, D_MODEL), lambda i, off=off: (i + off, 0)))
            args.append(m)
    else:
        o_ctx, o_lat = mix_inputs
        in_specs.append(pl.BlockSpec((D_MODEL, TM), lambda i: (0, jnp.minimum(i, N_CTX_TILES - 1))))
        in_specs.append(pl.BlockSpec((D_MODEL, TM), lambda i: (0, jnp.maximum(i - N_CTX_TILES, 0))))
        args += [o_ctx, o_lat]
    if final:
        in_specs.append(pl.BlockSpec((1, D_MODEL), lambda i: (0, 0)))
        args.append(final_g.reshape(1, D_MODEL))
    return pl.pallas_call(
        functools.partial(_ffn_kernel, lru=lru, final=final, split_in=split_in),
        out_shape=jax.ShapeDtypeStruct((n_tiles * TM, D_MODEL), F32),
        grid=(n_tiles,),
        in_specs=in_specs,
        out_specs=pl.BlockSpec((TM, D_MODEL), lambda i: (i, 0)),
        compiler_params=_cparams(("arbitrary",)),
        name="mix_ffn",
    )(*args)


def _rope_tables():
    f32 = np.float32
    t = np.arange(SEQ)
    row = (t // GRID_W).astype(f32)
    col = (t % GRID_W).astype(f32)
    half = HEAD_DIM // 2
    inv_freq = (f32(1.0) / (f32(ROPE_THETA) ** (np.arange(0, half, 2, dtype=f32) / f32(half)))).astype(f32)
    d = np.arange(LANES) % HEAD_DIM
    freq = inv_freq[(d % half) % (half // 2)]
    pos = np.where((d < half)[None, :], row[:, None], col[:, None])
    ang = (pos * freq[None, :]).astype(f32)
    first = ((d % half) < half // 2)[None, :]
    cos, sin = np.cos(ang).astype(f32), np.sin(ang).astype(f32)
    ident = np.ones((TM, LANES), f32)
    zero = np.zeros((TM, LANES), f32)
    return jnp.asarray(np.concatenate([ident, cos])), jnp.asarray(np.concatenate([zero, np.where(first, -sin, sin)]))


def _na_bias_strips(rpb):
    n_drow = NA_MASKED
    n_dcol = 2 * NA_WIN_W - 1
    span = 2 * GRID_W
    left = GRID_W - NA_WIN_W
    w = jnp.pad(rpb.astype(F32), ((0, 0), (0, 0), (left, span - left - n_dcol)))
    flat = jnp.tile(w, (1, 1, GRID_W))[..., GRID_W - 1:GRID_W - 1 + GRID_W * (span - 1)]
    toep = flat.reshape(NA_HEADS, n_drow, GRID_W, span - 1)[..., :GRID_W]
    qcol = np.arange(GRID_W)
    col_start = np.clip(qcol - NA_WIN_W // 2, 0, GRID_W - NA_WIN_W)
    col_in = (qcol[None, :] >= col_start[:, None]) & (qcol[None, :] < col_start[:, None] + NA_WIN_W)
    blocks = jnp.where(col_in[None, None], toep * LOG2E, MASK_VALUE)
    blocks = jnp.concatenate([blocks, jnp.full((NA_HEADS, 1, GRID_W, GRID_W), MASK_VALUE, F32)], axis=1)
    blocks_t = blocks.transpose(0, 1, 3, 2)
    seqs, _ = _na_strip_plan()
    strips = jnp.stack([jnp.concatenate([blocks_t[:, i] for i in seq], axis=-1) for seq in seqs], axis=1)
    return strips.reshape(NA_HEADS // 2, 2, len(seqs), GRID_W, len(seqs[0]) * GRID_W)


def _block_diag_256(w):
    per = 256 // LRU_BLOCK_DIM
    w4 = w.reshape(LRU_BLOCKS // per, per, LRU_BLOCK_DIM, LRU_BLOCK_DIM)
    eye = jnp.eye(per, dtype=w.dtype)
    return jnp.einsum("cide,ij->cidje", w4, eye).reshape(LRU_BLOCKS // per, 256, 256)


def kernel(x, c, ctx, c_ctx, ada_w, ada_b, norm_mix, norm_ffn, norm_final, ffn_w_in, ffn_w_out, na_w_qkv, na_rpb, na_w_o, gqa_w_qkv, gqa_q_gain, gqa_k_gain, gqa_w_o, swa_w_qkv, swa_sinks, swa_w_o, lru_w_in, lru_conv_w, lru_conv_b, lru_w_a, lru_b_a, lru_w_x, lru_b_x, lru_lam, lru_w_out):
    assert x.shape == (BATCH, SEQ, D_MODEL) and ctx.shape == (BATCH, CTX_LEN, D_MODEL)
    hx = (ctx.reshape(N_CTX_ROWS, D_MODEL), x.reshape(N_LAT_ROWS, D_MODEL))
    c_all = jnp.concatenate([c, c_ctx[None, :], jnp.zeros((MOD_ROWS - BATCH - 1, D_MODEL), F32)], axis=0)
    mod = _modulation(c_all, ada_w, ada_b).reshape(DEPTH, MOD_ROWS, N_MOD, D_MODEL)
    rope_tabs = _rope_tables()
    q_scale = HEAD_DIM ** -0.5 * LOG2E
    lat_tile0 = N_CTX_TILES

    def qkv_split(w, n_q, n_kv):
        wq_k = w[:, :(n_q + n_kv) * HEAD_DIM].astype(BF16)
        return wq_k, w[:, (n_q + n_kv) * HEAD_DIM:].T.astype(BF16)

    w_qk, w_vt = qkv_split(na_w_qkv[0], NA_HEADS, NA_HEADS)
    q, k, vt, w_in, w_out = _project(hx, mod[0], norm_mix[0], w_qk,
                                     ((D_MODEL, "plain", None, q_scale), (D_MODEL, "plain", None, 1.0)), (BF16, BF16),
                                     w_vt=w_vt, ffn_w=(ffn_w_in, ffn_w_out, 0))
    o = _na_attention(q, k, vt, _na_bias_strips(na_rpb[0]))
    hx = _mix_ffn(hx, mod[0], norm_ffn[0], na_w_o[0].astype(BF16), w_in, w_out, o)

    kvw = GQA_KV_HEADS * HEAD_DIM
    gains = jnp.stack([jnp.tile(g_, 2) for g_ in (gqa_q_gain[0], gqa_k_gain[0],
                                                  _swap_rotary_halves(gqa_q_gain[0]), _swap_rotary_halves(gqa_k_gain[0]))])
    w_qk, w_vt = qkv_split(gqa_w_qkv[0], GQA_HEADS, GQA_KV_HEADS)
    q, k, vt, w_in, w_out = _project(hx, mod[1], norm_mix[1], w_qk,
                                     ((D_MODEL, "rope", 0, q_scale), (kvw, "rope", 1, 1.0)), (BF16, BF16),
                                     rope_tabs, gains, w_vt=w_vt, ffn_w=(ffn_w_in, ffn_w_out, 1))
    o = _gqa_attention(q, k, vt)
    hx = _mix_ffn(hx, mod[1], norm_ffn[1], gqa_w_o[0].astype(BF16), w_in, w_out, o)

    kvw = SWA_KV_HEADS * HEAD_DIM
    w_qk, w_vt = qkv_split(swa_w_qkv[0], SWA_HEADS, SWA_KV_HEADS)
    q, k, vt, w_in, w_out = _project(hx, mod[2], norm_mix[2], w_qk,
                                     ((D_MODEL, "rope", None, q_scale), (kvw, "rope", None, 1.0)), (BF16, BF16),
                                     rope_tabs, jnp.ones((4, LANES), F32), w_vt=w_vt, ffn_w=(ffn_w_in, ffn_w_out, 2))
    o = _swa_attention(q, k, vt, swa_sinks[0])
    hx = _mix_ffn(hx, mod[2], norm_ffn[2], swa_w_o[0].astype(BF16), w_in, w_out, o)

    xr, gl, w_in, w_out = _project(hx, mod[3], norm_mix[3], lru_w_in[0].astype(BF16),
                                   ((D_MODEL, "plain", None, 1.0), (D_MODEL, "gelu", None, 1.0)), (F32, BF16),
                                   ffn_w=(ffn_w_in, ffn_w_out, 3))
    wbd = jnp.stack([jnp.stack([_block_diag_256(lru_w_a[0, d]), _block_diag_256(lru_w_x[0, d])])
                     for d in range(2)]).astype(BF16)
    hs = _lru_scan(xr, lru_conv_w[0], lru_conv_b[0], wbd, lru_b_a[0], lru_b_x[0], lru_lam[0])
    out = _mix_ffn(hx, mod[3], norm_ffn[3], lru_w_out[0].astype(BF16), w_in, w_out,
                   [(hs, 0), (gl, lat_tile0)], lru=True, final_g=norm_final)
    return out.reshape(BATCH, SEQ, D_MODEL)
```

```python
import functools

import jax
import jax.numpy as jnp
import numpy as np
from jax import lax
from jax.experimental import pallas as pl
from jax.experimental.pallas import tpu as pltpu

F32 = jnp.float32
BF16 = jnp.bfloat16

D_MODEL = 1024
BATCH = 8
SEQ = 2048
DEPTH = 4
GRID_W = 64
CTX_LEN = 256
HEAD_DIM = 64
ROPE_THETA = 10000.0
NA_HEADS = 16
NA_WIN_H = 8
NA_WIN_W = 16
GQA_HEADS = 16
GQA_KV_HEADS = 4
SWA_HEADS = 16
SWA_KV_HEADS = 2
SWA_WINDOW = 128
LRU_BLOCKS = 16
LRU_BLOCK_DIM = D_MODEL // LRU_BLOCKS
CONV_WIDTH = 4
LRU_C = 8.0
FFN_HIDDEN = 2816
N_MOD = 6
EPS = 1e-6
MASK_VALUE = -1e30
LOG2E = 1.4426950408889634

N_CTX_ROWS = BATCH * CTX_LEN
N_LAT_ROWS = BATCH * SEQ
N_ROWS = N_CTX_ROWS + N_LAT_ROWS
MOD_ROWS = 16
CTX_MOD_ROW = BATCH

LANES = 128
TM = 512
TQ = 256
FFN_CHUNK = 256
MOD_K_TILE = 256
FFN_CAST_ROWS = 128
GQA_KEY_CHUNK = 256
SHORT_UNIT_WIDTH = 2
VMEM_LIMIT = 56 * 1024 * 1024

N_CTX_TILES = N_CTX_ROWS // TM
TILES_PER_BATCH = SEQ // TM
NA_Q_ROWS = TQ // GRID_W
NA_K_ROWS = 12
NA_KEYS = NA_K_ROWS * GRID_W
NA_KEY_CHUNK = NA_KEYS // 2
NA_MASKED = 2 * NA_WIN_H - 1

_NT = (((1,), (1,)), ((), ()))


def _cparams(sem):
    return pltpu.CompilerParams(dimension_semantics=sem, vmem_limit_bytes=VMEM_LIMIT)


def _tile_mod_row(i):
    return jnp.where(i < N_CTX_TILES, CTX_MOD_ROW, (i - N_CTX_TILES) // TILES_PER_BATCH)


def _tile_pos_block(i):
    return jnp.where(i < N_CTX_TILES, 0, 1 + (i - N_CTX_TILES) % TILES_PER_BATCH)


def _ada_norm(x, g, shift, scale):
    ms = jnp.mean(x * x, axis=-1, keepdims=True)
    y = x * lax.rsqrt(ms + EPS)
    return (y * g) * (1.0 + scale) + shift


def _mod_kernel(c_ref, w_ref, b_ref, o_ref):
    kk = pl.program_id(1)
    c = c_ref[...]
    sc = (c * jax.nn.sigmoid(c)).astype(BF16)
    part = jnp.dot(sc, w_ref[0].astype(BF16), preferred_element_type=F32)

    @pl.when(kk == 0)
    def _():
        o_ref[0] = part + b_ref[0]

    @pl.when(kk > 0)
    def _():
        o_ref[0] = o_ref[0] + part


def _modulation(c_all, ada_w, ada_b):
    tk = MOD_K_TILE
    n = N_MOD * D_MODEL
    return pl.pallas_call(
        _mod_kernel,
        out_shape=jax.ShapeDtypeStruct((DEPTH, MOD_ROWS, n), F32),
        grid=(DEPTH, D_MODEL // tk),
        in_specs=[
            pl.BlockSpec((MOD_ROWS, tk), lambda l, k: (0, k)),
            pl.BlockSpec((1, tk, n), lambda l, k: (l, k, 0)),
            pl.BlockSpec((1, 1, n), lambda l, k: (l, 0, 0)),
        ],
        out_specs=pl.BlockSpec((1, MOD_ROWS, n), lambda l, k: (l, 0, 0)),
        compiler_params=_cparams(("arbitrary", "arbitrary")),
        name="modulation",
    )(c_all, ada_w, ada_b.reshape(DEPTH, 1, n))


def _tile_rows(ctx_ref, lat_ref, tile):
    return jnp.where(tile < N_CTX_TILES, ctx_ref[...], lat_ref[...])


def _tile_row_specs(ctx_rows, lat_rows, t0=0):
    return [pl.BlockSpec((TM, D_MODEL), lambda i: (jnp.minimum(i + t0, N_CTX_TILES - 1), 0)),
            pl.BlockSpec((TM, D_MODEL), lambda i: (jnp.maximum(i + t0 - N_CTX_TILES, 0), 0))], [ctx_rows, lat_rows]


def _proj_kernel(*refs, segs, has_rope, has_vt, split_in, cast_ffn):
    if split_in:
        x = _tile_rows(refs[0], refs[1], pl.program_id(0))
        refs = refs[1:]
    else:
        x = refs[0][...]
    mod_ref, g_ref, w_ref = refs[1:4]
    k = 4
    if has_rope:
        wsw_ref, cos_ref, sin_ref, gain_ref = refs[k:k + 4]
        k += 4
    if has_vt:
        wvt_ref = refs[k]
        k += 1
    if cast_ffn:
        win_ref, wout_ref = refs[k:k + 2]
        k += 2
        refs[-2][...] = win_ref[...].astype(BF16)
        refs[-1][...] = wout_ref[...].astype(BF16)
    out_refs = refs[k:]
    xn = _ada_norm(x, g_ref[...], mod_ref[0:1, :], mod_ref[1:2, :]).astype(BF16)
    lane = lax.broadcasted_iota(jnp.int32, (1, LANES), 1)
    lo = lane < HEAD_DIM
    col = 0
    for seg, o_ref in zip(segs, out_refs):
        width, kind, norm_idx, scale = seg
        chunk = 256
        for c0 in range(0, width, chunk):
            cw = min(chunk, width - c0)
            y = jnp.dot(xn, w_ref[:, col + c0:col + c0 + cw], preferred_element_type=F32)
            if kind == "gelu":
                y = jax.nn.gelu(y, approximate=True)
            elif kind == "rope":
                ysw = jnp.dot(xn, wsw_ref[:, col + c0:col + c0 + cw], preferred_element_type=F32)
                cos_t, sin_t = cos_ref[...], sin_ref[...]
                if norm_idx is not None:
                    cos_t = cos_t * gain_ref[norm_idx:norm_idx + 1, :]
                    sin_t = sin_t * gain_ref[2 + norm_idx:3 + norm_idx, :]
                parts = []
                for b0 in range(0, cw, LANES):
                    blk = y[:, b0:b0 + LANES]
                    rot = blk * cos_t + ysw[:, b0:b0 + LANES] * sin_t
                    if norm_idx is not None:
                        ss = blk * blk
                        s_lo = jnp.sum(jnp.where(lo, ss, 0.0), axis=-1, keepdims=True)
                        s_hi = jnp.sum(jnp.where(lo, 0.0, ss), axis=-1, keepdims=True)
                        ms = jnp.where(lo, s_lo, s_hi) * (1.0 / HEAD_DIM)
                        rot = rot * lax.rsqrt(ms + EPS)
                    parts.append(rot)
                y = parts[0] if len(parts) == 1 else jnp.concatenate(parts, axis=-1)
            if scale != 1.0:
                y = y * scale
            o_ref[:, c0:c0 + cw] = y.astype(o_ref.dtype)
        col += width
    if has_vt:
        vt_ref = out_refs[len(segs)]
        for r0 in range(0, wvt_ref.shape[0], 256):
            rw = min(256, wvt_ref.shape[0] - r0)
            vt = lax.dot_general(wvt_ref[r0:r0 + rw, :], xn, _NT, preferred_element_type=F32)
            vt_ref[r0:r0 + rw, :] = vt.astype(vt_ref.dtype)


def _swap_rotary_halves(a):
    quarter = HEAD_DIM // 4
    return a.reshape(a.shape[:-1] + (-1, 2, quarter))[..., ::-1, :].reshape(a.shape)


def _project(hx, mod_l, g, w, segs, out_dtypes, rope_tabs=None, gains=None, w_vt=None, ffn_w=None):
    n_tot = sum(s[0] for s in segs)
    resident = functools.partial(pl.BlockSpec, pipeline_mode=pl.Buffered(1))
    split_in = isinstance(hx, tuple)
    if split_in:
        in_specs, args = _tile_row_specs(*hx)
    else:
        in_specs, args = [pl.BlockSpec((TM, D_MODEL), lambda i: (i, 0))], [hx]
    in_specs += [
        pl.BlockSpec((None, N_MOD, D_MODEL), lambda i: (_tile_mod_row(i), 0, 0)),
        pl.BlockSpec((1, D_MODEL), lambda i: (0, 0)),
        resident((D_MODEL, n_tot), lambda i: (0, 0)),
    ]
    args += [mod_l, g.reshape(1, D_MODEL), w]
    has_rope = rope_tabs is not None
    if has_rope:
        in_specs.append(resident((D_MODEL, n_tot), lambda i: (0, 0)))
        args.append(_swap_rotary_halves(w))
        for t in rope_tabs:
            in_specs.append(pl.BlockSpec((TM, LANES), lambda i: (_tile_pos_block(i), 0)))
            args.append(t)
        in_specs.append(pl.BlockSpec(gains.shape, lambda i: (0, 0)))
        args.append(gains)
    out_shape = [jax.ShapeDtypeStruct((N_ROWS, s[0]), dt) for s, dt in zip(segs, out_dtypes)]
    out_specs = [pl.BlockSpec((TM, s[0]), lambda i: (i, 0)) for s in segs]
    if w_vt is not None:
        in_specs.append(resident(w_vt.shape, lambda i: (0, 0)))
        args.append(w_vt)
        out_shape.append(jax.ShapeDtypeStruct((w_vt.shape[0], N_ROWS), BF16))
        out_specs.append(pl.BlockSpec((w_vt.shape[0], TM), lambda i: (0, i)))
    if ffn_w is not None:
        f_in, f_out, layer = ffn_w
        n_blk = FFN_HIDDEN // FFN_CAST_ROWS
        blk = lambda i: jnp.minimum(i, n_blk - 1)
        cols = f_in.shape[2] // n_blk
        in_specs += [pl.BlockSpec((None, D_MODEL, cols), lambda i: (layer, 0, blk(i))),
                     pl.BlockSpec((None, FFN_CAST_ROWS, D_MODEL), lambda i: (layer, blk(i), 0))]
        args += [f_in, f_out]
        out_shape += [jax.ShapeDtypeStruct(f_in.shape[1:], BF16), jax.ShapeDtypeStruct(f_out.shape[1:], BF16)]
        out_specs += [pl.BlockSpec((D_MODEL, cols), lambda i: (0, blk(i))),
                      pl.BlockSpec((FFN_CAST_ROWS, D_MODEL), lambda i: (blk(i), 0))]
    return pl.pallas_call(
        functools.partial(_proj_kernel, segs=segs, has_rope=has_rope, has_vt=w_vt is not None, split_in=split_in,
                          cast_ffn=ffn_w is not None),
        out_shape=out_shape,
        grid=(N_ROWS // TM,),
        in_specs=in_specs,
        out_specs=out_specs,
        compiler_params=_cparams(("arbitrary",)),
        name="ada_proj",
    )(*args)


def _head_query(q_ref, rows, head, kv_half):
    pair, half = divmod(head, 2)
    q = q_ref[rows, pair * LANES:(pair + 1) * LANES]
    if half != kv_half:
        q = pltpu.roll(q, HEAD_DIM, 1)
    lane = lax.broadcasted_iota(jnp.int32, (1, LANES), 1)
    keep = (lane < HEAD_DIM) if kv_half == 0 else (lane >= HEAD_DIM)
    return jnp.where(keep, q, jnp.zeros_like(q))


def _attention_units(units, width=1):
    groups = [units[i:i + width] for i in range(0, len(units), width)]
    st1, st2 = [], []
    for idx in range(len(groups) + 2):
        cur = groups[idx] if idx < len(groups) else []
        qs = [u[0]() for u in cur]
        cur_s = [[] for _ in cur]
        cur_m8 = [None for _ in cur]
        new_e = [[] for _ in st1]
        new_d8 = [jnp.zeros((8, TQ), F32) for _ in st1]
        acc = [jnp.zeros((HEAD_DIM, TQ), F32) for _ in st2]
        n_steps = max([len(u[1]) for u in cur] + [len(p[0][1]) for p in st1] + [len(p[0][1]) for p in st2] + [0])
        for c in range(n_steps):
            for i, u in enumerate(cur):
                if c < len(u[1]):
                    k_fn, _, bias_fn = u[1][c]
                    s = lax.dot_general(k_fn(), qs[i], _NT, preferred_element_type=F32)
                    if bias_fn is not None:
                        s = s + bias_fn()
                    cur_s[i].append(s)
                    mc = jnp.max(s.reshape(s.shape[0] // 8, 8, TQ), axis=0)
                    cur_m8[i] = mc if cur_m8[i] is None else jnp.maximum(cur_m8[i], mc)
            for i, (u, p_s, p_m) in enumerate(st1):
                if c < len(u[1]):
                    e = jnp.exp2(p_s[c] - p_m)
                    new_d8[i] = new_d8[i] + jnp.sum(e.reshape(e.shape[0] // 8, 8, TQ), axis=0)
                    new_e[i].append(e.astype(BF16))
            for i, (u, p_e, p_d8, p_m) in enumerate(st2):
                if c < len(u[1]):
                    acc[i] = acc[i] + jnp.dot(u[1][c][1](), p_e[c], preferred_element_type=F32)
        for i, (u, p_e, p_d8, p_m) in enumerate(st2):
            den = jnp.sum(p_d8, axis=0, keepdims=True)
            if u[2] is not None:
                den = den + jnp.exp2(u[2] - p_m)
            u[3](acc[i] / den)
        st2 = [(u, new_e[i], new_d8[i], p_m) for i, (u, p_s, p_m) in enumerate(st1)]
        st1 = []
        for i, u in enumerate(cur):
            m = jnp.max(cur_m8[i], axis=0, keepdims=True)
            if u[2] is not None:
                m = jnp.maximum(m, u[2])
            st1.append((u, cur_s[i], m))


def _store_head(o_ref, head, cols):
    def store(o):
        o_ref[head * HEAD_DIM:(head + 1) * HEAD_DIM, cols] = o.astype(o_ref.dtype)
    return store


def _chunks(k_ref, vt_ref, vrows, k0, n_keys, chunk, bias_fn=None):
    out = []
    for r0 in range(0, n_keys, chunk):
        n = min(chunk, n_keys - r0)
        out.append((
            functools.partial(lambda r0, n: k_ref[pl.ds(k0 + r0, n), :], r0, n),
            functools.partial(lambda r0, n: vt_ref[vrows, pl.ds(k0 + r0, n)], r0, n),
            None if bias_fn is None else functools.partial(bias_fn, r0, n)))
    return out


def _head_units(q_ref, rows, o_ref, cols, n_heads, group, kc_ref, vtc_ref, lat=None, bias_fn=None, sink_fn=None):
    units = []
    for head in range(n_heads):
        kv_half = (head // group) % 2
        vrows = slice(kv_half * HEAD_DIM, (kv_half + 1) * HEAD_DIM)
        chunks = _chunks(kc_ref, vtc_ref, vrows, 0, CTX_LEN, CTX_LEN)
        if lat is not None:
            k_ref, vt_ref, k0, n_keys, chunk = lat
            chunks += _chunks(k_ref, vt_ref, vrows, k0, n_keys, chunk,
                              None if bias_fn is None else functools.partial(bias_fn, head))
        units.append((functools.partial(_head_query, q_ref, rows, head, kv_half), chunks,
                      None if sink_fn is None else sink_fn(head), _store_head(o_ref, head, cols)))
    return units


def _na_row_index():
    rows = SEQ // GRID_W
    idx = []
    for q0, k0 in ((0, 0), (NA_Q_ROWS, 0), (rows - NA_Q_ROWS, rows - NA_K_ROWS)):
        qrow = q0 + np.arange(NA_Q_ROWS)
        krow = k0 + np.arange(NA_K_ROWS)
        r0 = np.clip(qrow - NA_WIN_H // 2, 0, rows - NA_WIN_H)
        row_in = (krow[None, :] >= r0[:, None]) & (krow[None, :] < r0[:, None] + NA_WIN_H)
        idx.append(np.where(row_in, krow[None, :] - qrow[:, None] + NA_WIN_H - 1, NA_MASKED))
    return np.stack(idx)


def _na_strip_plan():
    lo = NA_WIN_H - 1 - NA_WIN_H // 2
    interior = [NA_MASKED] * NA_Q_ROWS + list(range(lo + NA_WIN_H - 1, lo - 1, -1)) + [NA_MASKED] * (NA_Q_ROWS - 1)
    edge = list(range(NA_MASKED - 1, -1, -1))
    strips = [interior + [NA_MASKED], [NA_MASKED] + interior, edge + [NA_MASKED], [NA_MASKED] + edge]
    idx = _na_row_index()
    plan = []
    for case in range(idx.shape[0]):
        rows = []
        for kr in range(NA_K_ROWS):
            want = [int(v) for v in idx[case, :, kr]]
            if all(v == NA_MASKED for v in want):
                rows.append(None)
                continue
            hits = [(s, p) for s, seq in enumerate(strips) for p in range(0, len(seq) - NA_Q_ROWS + 1, 2)
                    if seq[p:p + NA_Q_ROWS] == want]
            rows.append(hits[0])
        plan.append(rows)
    return strips, plan


def _na_fill_bias(strip_ref, bias_s):
    _, plan = _na_strip_plan()
    for head in range(2):
        for case in range(len(plan)):
            for kr, hit in enumerate(plan[case]):
                rows = slice(kr * GRID_W, (kr + 1) * GRID_W)
                if hit is None:
                    bias_s[head, case, rows, :] = jnp.full((GRID_W, TQ), MASK_VALUE, F32)
                else:
                    strip, first = hit
                    bias_s[head, case, rows, :] = strip_ref[head, strip, :, first * GRID_W:first * GRID_W + TQ]


def _na_kernel(qc_ref, ql_ref, kc_ref, vtc_ref, kl_ref, vtl_ref, strip_ref, oc_ref, ol_ref, bias_ref):
    @pl.when(pl.program_id(1) == 0)
    def _():
        _na_fill_bias(strip_ref, bias_ref)

    all_rows = slice(None)
    units = _head_units(qc_ref, all_rows, oc_ref, all_rows, 2, 1, kc_ref, vtc_ref)
    n_tiles = SEQ // TQ
    for j in range(n_tiles):
        case = 0 if j == 0 else (2 if j == n_tiles - 1 else 1)
        k0 = min(max(NA_Q_ROWS * j - NA_WIN_H // 2, 0), SEQ // GRID_W - NA_K_ROWS) * GRID_W
        rows = slice(j * TQ, (j + 1) * TQ)
        units += _head_units(ql_ref, rows, ol_ref, rows, 2, 1, kc_ref, vtc_ref, (kl_ref, vtl_ref, k0, NA_KEYS, NA_KEY_CHUNK),
                             functools.partial(lambda case, head, r0, n: bias_ref[head, case, r0:r0 + n, :], case))
    _attention_units(units)


def _na_attention(q, k, vt, strips):
    lat0 = N_CTX_ROWS // SEQ
    return pl.pallas_call(
        _na_kernel,
        out_shape=[jax.ShapeDtypeStruct((D_MODEL, N_CTX_ROWS), BF16),
                   jax.ShapeDtypeStruct((D_MODEL, N_LAT_ROWS), BF16)],
        grid=(NA_HEADS // 2, BATCH),
        in_specs=[
            pl.BlockSpec((CTX_LEN, LANES), lambda p, b: (b, p)),
            pl.BlockSpec((SEQ, LANES), lambda p, b: (lat0 + b, p)),
            pl.BlockSpec((CTX_LEN, LANES), lambda p, b: (b, p)),
            pl.BlockSpec((LANES, CTX_LEN), lambda p, b: (p, b)),
            pl.BlockSpec((SEQ, LANES), lambda p, b: (lat0 + b, p)),
            pl.BlockSpec((LANES, SEQ), lambda p, b: (p, lat0 + b)),
            pl.BlockSpec((None,) + strips.shape[1:], lambda p, b: (p, 0, 0, 0, 0)),
        ],
        out_specs=[pl.BlockSpec((LANES, CTX_LEN), lambda p, b: (p, b)),
                   pl.BlockSpec((LANES, SEQ), lambda p, b: (p, b))],
        scratch_shapes=[pltpu.VMEM((2, 3, NA_KEYS, TQ), F32)],
        compiler_params=_cparams(("arbitrary", "arbitrary")),
        name="na_attention",
    )(q, q, k, vt, k, vt, strips)


def _q_tile_index(b, j):
    n_qt = SEQ // TQ
    return jnp.where(j == 0, b, N_CTX_ROWS // TQ + b * n_qt + j - 1)


def _lat_out_index(b, j):
    return b * (SEQ // TQ) + jnp.maximum(j - 1, 0)


def _gqa_kernel(qc_ref, ql_ref, kc_ref, vtc_ref, kl_ref, vtl_ref, oc_ref, ol_ref):
    group = GQA_HEADS // GQA_KV_HEADS
    all_rows = slice(None)
    _attention_units(_head_units(qc_ref, all_rows, oc_ref, all_rows, 2 * group, group, kc_ref, vtc_ref),
                     SHORT_UNIT_WIDTH)

    def tile(j, carry):
        rows = pl.ds(pl.multiple_of(j * TQ, TQ), TQ)
        _attention_units(_head_units(ql_ref, rows, ol_ref, rows, 2 * group, group, kc_ref, vtc_ref,
                                     (kl_ref, vtl_ref, 0, SEQ, GQA_KEY_CHUNK)))
        return carry

    lax.fori_loop(0, SEQ // TQ, tile, 0)


def _gqa_attention(q, k, vt):
    lat0 = N_CTX_ROWS // SEQ
    qw = 2 * (GQA_HEADS // GQA_KV_HEADS) * HEAD_DIM
    return pl.pallas_call(
        _gqa_kernel,
        out_shape=[jax.ShapeDtypeStruct((D_MODEL, N_CTX_ROWS), BF16),
                   jax.ShapeDtypeStruct((D_MODEL, N_LAT_ROWS), BF16)],
        grid=(BATCH, GQA_KV_HEADS // 2),
        in_specs=[
            pl.BlockSpec((CTX_LEN, qw), lambda b, p: (b, p)),
            pl.BlockSpec((SEQ, qw), lambda b, p: (lat0 + b, p)),
            pl.BlockSpec((CTX_LEN, LANES), lambda b, p: (b, p)),
            pl.BlockSpec((LANES, CTX_LEN), lambda b, p: (p, b)),
            pl.BlockSpec((SEQ, LANES), lambda b, p: (lat0 + b, p)),
            pl.BlockSpec((LANES, SEQ), lambda b, p: (p, lat0 + b)),
        ],
        out_specs=[pl.BlockSpec((qw, CTX_LEN), lambda b, p: (p, b)),
                   pl.BlockSpec((qw, SEQ), lambda b, p: (p, b))],
        compiler_params=_cparams(("arbitrary", "arbitrary")),
        name="gqa_attention",
    )(q, q, k, vt, k, vt)


def _swa_kernel(sink_ref, q_ref, kc_ref, vtc_ref, kl_ref, vtl_ref, oc_ref, ol_ref):
    j = pl.program_id(1)
    group = SWA_HEADS // SWA_KV_HEADS
    all_rows = slice(None)
    sink_fn = lambda head: sink_ref[head] * LOG2E

    @pl.when(j == 0)
    def _():
        _attention_units(_head_units(q_ref, all_rows, oc_ref, all_rows, SWA_HEADS, group, kc_ref, vtc_ref,
                                     sink_fn=sink_fn), SHORT_UNIT_WIDTH)

    @pl.when(j > 0)
    def _():
        band = TQ + 2 * SWA_WINDOW
        q0 = (j - 1) * TQ
        k0 = pl.multiple_of(jnp.clip(q0 - SWA_WINDOW, 0, SEQ - band), LANES)
        kpos = k0 + lax.broadcasted_iota(jnp.int32, (band, 1), 0)
        qpos = q0 + lax.broadcasted_iota(jnp.int32, (1, TQ), 1)
        mask = jnp.where(jnp.abs(qpos - kpos) <= SWA_WINDOW, 0.0, MASK_VALUE).astype(F32)
        _attention_units(_head_units(q_ref, all_rows, ol_ref, all_rows, SWA_HEADS, group, kc_ref, vtc_ref,
                                     (kl_ref, vtl_ref, k0, band, band),
                                     bias_fn=lambda head, r0, n: mask[r0:r0 + n, :], sink_fn=sink_fn),
                         SHORT_UNIT_WIDTH)


def _swa_attention(q, k, vt, sinks):
    lat0 = N_CTX_ROWS // SEQ
    n_qt = SEQ // TQ
    return pl.pallas_call(
        _swa_kernel,
        out_shape=[jax.ShapeDtypeStruct((D_MODEL, N_CTX_ROWS), BF16),
                   jax.ShapeDtypeStruct((D_MODEL, N_LAT_ROWS), BF16)],
        grid=(BATCH, n_qt + 1),
        in_specs=[
            pl.BlockSpec(memory_space=pltpu.SMEM),
            pl.BlockSpec((TQ, D_MODEL), lambda b, j: (_q_tile_index(b, j), 0)),
            pl.BlockSpec((CTX_LEN, LANES), lambda b, j: (b, 0)),
            pl.BlockSpec((LANES, CTX_LEN), lambda b, j: (0, b)),
            pl.BlockSpec((SEQ, LANES), lambda b, j: (lat0 + b, 0)),
            pl.BlockSpec((LANES, SEQ), lambda b, j: (0, lat0 + b)),
        ],
        out_specs=[pl.BlockSpec((D_MODEL, CTX_LEN), lambda b, j: (0, b)),
                   pl.BlockSpec((D_MODEL, TQ), lambda b, j: (0, _lat_out_index(b, j)))],
        compiler_params=_cparams(("arbitrary", "arbitrary")),
        name="swa_attention",
    )(sinks, q, k, vt, k, vt)


def _scan_rows(a, b, h, reverse):
    sub = 8
    groups = a.shape[0] // sub
    a3 = a.reshape(groups, sub, D_MODEL)
    b3 = b.reshape(groups, sub, D_MODEL)
    row = lax.broadcasted_iota(jnp.int32, (1, sub, 1), 1)
    step = 1
    while step < sub:
        keep = (row < sub - step) if reverse else (row >= step)
        shift = sub - step if reverse else step
        a_prev = jnp.where(keep, pltpu.roll(a3, shift, 1), 1.0)
        b_prev = jnp.where(keep, pltpu.roll(b3, shift, 1), 0.0)
        b3 = b3 + a3 * b_prev
        a3 = a3 * a_prev
        step *= 2
    out = [None] * groups
    for g in (range(groups - 1, -1, -1) if reverse else range(groups)):
        hg = a3[g] * h + b3[g]
        out[g] = hg
        h = hg[0:1, :] if reverse else hg[sub - 1:sub, :]
    return jnp.concatenate(out, axis=0), h


def _lru_kernel(xc_ref, xl_ref, cw_ref, cb_ref, wbd_ref, ba_ref, bx_ref, lam_ref, o_ref, conv_s):
    chunk = CTX_LEN
    n_lat = SEQ // chunk
    sp = jax.nn.softplus(-lam_ref[...])

    def conv(before, cur, after):
        ext = jnp.concatenate([before, cur, after], axis=0)
        acc = ext * cw_ref[2:3, :]
        acc = acc + pltpu.roll(ext, 2, 0) * cw_ref[0:1, :]
        acc = acc + pltpu.roll(ext, 1, 0) * cw_ref[1:2, :]
        acc = acc + pltpu.roll(ext, chunk + 15, 0) * cw_ref[3:4, :]
        return acc[8:8 + chunk, :] + cb_ref[...]

    def gates(xc, d):
        xb = xc.astype(BF16)
        pre = []
        for kind in range(2):
            cols = [jnp.dot(xb[:, 256 * c:256 * (c + 1)], wbd_ref[d, kind, c], preferred_element_type=F32)
                    for c in range(D_MODEL // 256)]
            pre.append(jnp.concatenate(cols, axis=-1))
        t_r = jnp.tanh(0.5 * (pre[0] + ba_ref[d:d + 1, :]))
        ig = 0.5 * jnp.tanh(0.5 * (pre[1] + bx_ref[d:d + 1, :])) + 0.5
        half_decay = (-0.5 * LRU_C) * sp[d:d + 1, :]
        log_a = half_decay * t_r + half_decay
        a = jnp.exp(log_a)
        return a, jnp.sqrt(-jnp.tanh(log_a) * (a * a + 1.0)) * (ig * xc)

    def lat_chunk(c, h, d, reverse):
        start = pl.multiple_of(c * chunk, chunk)
        rows = pl.ds(start, chunk)
        kept = pl.ds(CTX_LEN + start, chunk)
        if d == 0:
            before = jnp.where(c > 0, xl_ref[pl.ds(pl.multiple_of(jnp.maximum(start - 8, 0), 8), 8), :], 0.0)
            after = jnp.where(c < n_lat - 1,
                              xl_ref[pl.ds(pl.multiple_of(jnp.minimum(start + chunk, SEQ - 8), 8), 8), :], 0.0)
            xc = conv(before, xl_ref[rows, :], after)
            conv_s[kept, :] = xc
        else:
            xc = conv_s[kept, :]
        states, h = _scan_rows(*gates(xc, d), h, reverse)
        if d == 0:
            o_ref[rows, :] = states
        else:
            o_ref[rows, :] = o_ref[rows, :] + states
        return h

    zeros = jnp.zeros((8, D_MODEL), F32)
    conv_s[0:CTX_LEN, :] = conv(zeros, xc_ref[...], zeros)
    for d in range(2):
        reverse = d == 1
        _, h = _scan_rows(*gates(conv_s[0:CTX_LEN, :], d), jnp.zeros((1, D_MODEL), F32), reverse)
        lax.fori_loop(0, n_lat, lambda i, h: lat_chunk((n_lat - 1 - i) if reverse else i, h, d, reverse), h)


def _lru_scan(xr, conv_w, conv_b, wbd, b_a, b_x, lam):
    lat0 = N_CTX_ROWS // SEQ
    const2 = lambda b: (0, 0)
    return pl.pallas_call(
        _lru_kernel,
        out_shape=jax.ShapeDtypeStruct((N_LAT_ROWS, D_MODEL), F32),
        grid=(BATCH,),
        in_specs=[
            pl.BlockSpec((CTX_LEN, D_MODEL), lambda b: (b, 0)),
            pl.BlockSpec((SEQ, D_MODEL), lambda b: (lat0 + b, 0)),
            pl.BlockSpec((CONV_WIDTH, D_MODEL), const2),
            pl.BlockSpec((1, D_MODEL), const2),
            pl.BlockSpec(wbd.shape, lambda b: (0, 0, 0, 0, 0)),
            pl.BlockSpec((2, D_MODEL), const2),
            pl.BlockSpec((2, D_MODEL), const2),
            pl.BlockSpec((2, D_MODEL), const2),
        ],
        out_specs=pl.BlockSpec((SEQ, D_MODEL), lambda b: (b, 0)),
        scratch_shapes=[pltpu.VMEM((CTX_LEN + SEQ, D_MODEL), F32)],
        compiler_params=_cparams(("arbitrary",)),
        name="rglru_scan",
    )(xr, xr, conv_w, conv_b.reshape(1, D_MODEL), wbd, b_a, b_x, lam)


def _ffn_kernel(*refs, lru, final, split_in):
    if split_in:
        h = _tile_rows(refs[0], refs[1], pl.program_id(0))
        refs = refs[1:]
    else:
        h = refs[0][...]
    mod_ref, g_ref, wo_ref, win_ref, wout_ref = refs[1:6]
    k = 6
    if lru:
        hs_ref, gl_ref = refs[k:k + 2]
        k += 2
        o_mix = (hs_ref[...] * gl_ref[...]).astype(BF16)
        y = jnp.dot(o_mix, wo_ref[...], preferred_element_type=F32)
    else:
        oc_ref, ol_ref = refs[k:k + 2]
        k += 2
        o_t = jnp.where(pl.program_id(0) < N_CTX_TILES, oc_ref[...], ol_ref[...])
        y = lax.dot_general(o_t, wo_ref[...], (((0,), (0,)), ((), ())), preferred_element_type=F32)
    if final:
        gf_ref = refs[k]
        k += 1
    out_ref = refs[k]
    h1 = h + mod_ref[2:3, :] * y
    xn = _ada_norm(h1, g_ref[...], mod_ref[3:4, :], mod_ref[4:5, :]).astype(BF16)
    acc = jnp.zeros(h1.shape, F32)
    for c0 in range(0, FFN_HIDDEN, FFN_CHUNK):
        a = jnp.dot(xn, win_ref[:, c0:c0 + FFN_CHUNK], preferred_element_type=F32)
        g = jnp.dot(xn, win_ref[:, FFN_HIDDEN + c0:FFN_HIDDEN + c0 + FFN_CHUNK], preferred_element_type=F32)
        u = ((a * jax.nn.sigmoid(a)) * g).astype(BF16)
        acc = acc + jnp.dot(u, wout_ref[c0:c0 + FFN_CHUNK, :], preferred_element_type=F32)
    h2 = h1 + mod_ref[5:6, :] * acc
    if final:
        ms = jnp.mean(h2 * h2, axis=-1, keepdims=True)
        h2 = (h2 * lax.rsqrt(ms + EPS)) * gf_ref[...]
    out_ref[...] = h2


def _mix_ffn(hx, mod_l, g_ffn, w_o, w_in, w_out, mix_inputs, lru=False, final_g=None):
    final = final_g is not None
    t0 = N_CTX_TILES if final else 0
    n_tiles = N_ROWS // TM - t0
    resident = functools.partial(pl.BlockSpec, pipeline_mode=pl.Buffered(1))
    split_in = isinstance(hx, tuple)
    assert not (split_in and final)
    if split_in:
        in_specs, args = _tile_row_specs(*hx)
    else:
        in_specs, args = [pl.BlockSpec((TM, D_MODEL), lambda i: (i + t0, 0))], [hx]
    in_specs += [
        pl.BlockSpec((None, N_MOD, D_MODEL), lambda i: (_tile_mod_row(i + t0), 0, 0)),
        pl.BlockSpec((1, D_MODEL), lambda i: (0, 0)),
        resident(w_o.shape, lambda i: (0, 0)),
        resident(w_in.shape, lambda i: (0, 0)),
        resident(w_out.shape, lambda i: (0, 0)),
    ]
    args += [mod_l, g_ffn.reshape(1, D_MODEL), w_o, w_in, w_out]
    if lru:
        for m, off in mix_inputs:
            in_specs.append(pl.BlockSpec((TM, D_MODEL), lambda i, off=off: (i + off, 0)))
            args.append(m)
    else:
        o_ctx, o_lat = mix_inputs
        in_specs.append(pl.BlockSpec((D_MODEL, TM), lambda i: (0, jnp.minimum(i, N_CTX_TILES - 1))))
        in_specs.append(pl.BlockSpec((D_MODEL, TM), lambda i: (0, jnp.maximum(i - N_CTX_TILES, 0))))
        args += [o_ctx, o_lat]
    if final:
        in_specs.append(pl.BlockSpec((1, D_MODEL), lambda i: (0, 0)))
        args.append(final_g.reshape(1, D_MODEL))
    return pl.pallas_call(
        functools.partial(_ffn_kernel, lru=lru, final=final, split_in=split_in),
        out_shape=jax.ShapeDtypeStruct((n_tiles * TM, D_MODEL), F32),
        grid=(n_tiles,),
        in_specs=in_specs,
        out_specs=pl.BlockSpec((TM, D_MODEL), lambda i: (i, 0)),
        compiler_params=_cparams(("arbitrary",)),
        name="mix_ffn",
    )(*args)


def _rope_tables():
    f32 = np.float32
    t = np.arange(SEQ)
    row = (t // GRID_W).astype(f32)
    col = (t % GRID_W).astype(f32)
    half = HEAD_DIM // 2
    inv_freq = (f32(1.0) / (f32(ROPE_THETA) ** (np.arange(0, half, 2, dtype=f32) / f32(half)))).astype(f32)
    d = np.arange(LANES) % HEAD_DIM
    freq = inv_freq[(d % half) % (half // 2)]
    pos = np.where((d < half)[None, :], row[:, None], col[:, None])
    ang = (pos * freq[None, :]).astype(f32)
    first = ((d % half) < half // 2)[None, :]
    cos, sin = np.cos(ang).astype(f32), np.sin(ang).astype(f32)
    ident = np.ones((TM, LANES), f32)
    zero = np.zeros((TM, LANES), f32)
    return jnp.asarray(np.concatenate([ident, cos])), jnp.asarray(np.concatenate([zero, np.where(first, -sin, sin)]))


def _na_bias_strips(rpb):
    n_drow = NA_MASKED
    n_dcol = 2 * NA_WIN_W - 1
    span = 2 * GRID_W
    left = GRID_W - NA_WIN_W
    w = jnp.pad(rpb.astype(F32), ((0, 0), (0, 0), (left, span - left - n_dcol)))
    flat = jnp.tile(w, (1, 1, GRID_W))[..., GRID_W - 1:GRID_W - 1 + GRID_W * (span - 1)]
    toep = flat.reshape(NA_HEADS, n_drow, GRID_W, span - 1)[..., :GRID_W]
    qcol = np.arange(GRID_W)
    col_start = np.clip(qcol - NA_WIN_W // 2, 0, GRID_W - NA_WIN_W)
    col_in = (qcol[None, :] >= col_start[:, None]) & (qcol[None, :] < col_start[:, None] + NA_WIN_W)
    blocks = jnp.where(col_in[None, None], toep * LOG2E, MASK_VALUE)
    blocks = jnp.concatenate([blocks, jnp.full((NA_HEADS, 1, GRID_W, GRID_W), MASK_VALUE, F32)], axis=1)
    blocks_t = blocks.transpose(0, 1, 3, 2)
    seqs, _ = _na_strip_plan()
    strips = jnp.stack([jnp.concatenate([blocks_t[:, i] for i in seq], axis=-1) for seq in seqs], axis=1)
    return strips.reshape(NA_HEADS // 2, 2, len(seqs), GRID_W, len(seqs[0]) * GRID_W)


def _block_diag_256(w):
    per = 256 // LRU_BLOCK_DIM
    w4 = w.reshape(LRU_BLOCKS // per, per, LRU_BLOCK_DIM, LRU_BLOCK_DIM)
    eye = jnp.eye(per, dtype=w.dtype)
    return jnp.einsum("cide,ij->cidje", w4, eye).reshape(LRU_BLOCKS // per, 256, 256)


def kernel(x, c, ctx, c_ctx, ada_w, ada_b, norm_mix, norm_ffn, norm_final, ffn_w_in, ffn_w_out, na_w_qkv, na_rpb, na_w_o, gqa_w_qkv, gqa_q_gain, gqa_k_gain, gqa_w_o, swa_w_qkv, swa_sinks, swa_w_o, lru_w_in, lru_conv_w, lru_conv_b, lru_w_a, lru_b_a, lru_w_x, lru_b_x, lru_lam, lru_w_out):
    assert x.shape == (BATCH, SEQ, D_MODEL) and ctx.shape == (BATCH, CTX_LEN, D_MODEL)
    hx = (ctx.reshape(N_CTX_ROWS, D_MODEL), x.reshape(N_LAT_ROWS, D_MODEL))
    c_all = jnp.concatenate([c, c_ctx[None, :], jnp.zeros((MOD_ROWS - BATCH - 1, D_MODEL), F32)], axis=0)
    mod = _modulation(c_all, ada_w, ada_b).reshape(DEPTH, MOD_ROWS, N_MOD, D_MODEL)
    rope_tabs = _rope_tables()
    q_scale = HEAD_DIM ** -0.5 * LOG2E
    lat_tile0 = N_CTX_TILES

    def qkv_split(w, n_q, n_kv):
        wq_k = w[:, :(n_q + n_kv) * HEAD_DIM].astype(BF16)
        return wq_k, w[:, (n_q + n_kv) * HEAD_DIM:].T.astype(BF16)

    w_qk, w_vt = qkv_split(na_w_qkv[0], NA_HEADS, NA_HEADS)
    q, k, vt, w_in, w_out = _project(hx, mod[0], norm_mix[0], w_qk,
                                     ((D_MODEL, "plain", None, q_scale), (D_MODEL, "plain", None, 1.0)), (BF16, BF16),
                                     w_vt=w_vt, ffn_w=(ffn_w_in, ffn_w_out, 0))
    o = _na_attention(q, k, vt, _na_bias_strips(na_rpb[0]))
    hx = _mix_ffn(hx, mod[0], norm_ffn[0], na_w_o[0].astype(BF16), w_in, w_out, o)

    kvw = GQA_KV_HEADS * HEAD_DIM
    gains = jnp.stack([jnp.tile(g_, 2) for g_ in (gqa_q_gain[0], gqa_k_gain[0],
                                                  _swap_rotary_halves(gqa_q_gain[0]), _swap_rotary_halves(gqa_k_gain[0]))])
    w_qk, w_vt = qkv_split(gqa_w_qkv[0], GQA_HEADS, GQA_KV_HEADS)
    q, k, vt, w_in, w_out = _project(hx, mod[1], norm_mix[1], w_qk,
                                     ((D_MODEL, "rope", 0, q_scale), (kvw, "rope", 1, 1.0)), (BF16, BF16),
                                     rope_tabs, gains, w_vt=w_vt, ffn_w=(ffn_w_in, ffn_w_out, 1))
    o = _gqa_attention(q, k, vt)
    hx = _mix_ffn(hx, mod[1], norm_ffn[1], gqa_w_o[0].astype(BF16), w_in, w_out, o)

    kvw = SWA_KV_HEADS * HEAD_DIM
    w_qk, w_vt = qkv_split(swa_w_qkv[0], SWA_HEADS, SWA_KV_HEADS)
    q, k, vt, w_in, w_out = _project(hx, mod[2], norm_mix[2], w_qk,
                                     ((D_MODEL, "rope", None, q_scale), (kvw, "rope", None, 1.0)), (BF16, BF16),
                                     rope_tabs, jnp.ones((4, LANES), F32), w_vt=w_vt, ffn_w=(ffn_w_in, ffn_w_out, 2))
    o = _swa_attention(q, k, vt, swa_sinks[0])
    hx = _mix_ffn(hx, mod[2], norm_ffn[2], swa_w_o[0].astype(BF16), w_in, w_out, o)

    xr, gl, w_in, w_out = _project(hx, mod[3], norm_mix[3], lru_w_in[0].astype(BF16),
                                   ((D_MODEL, "plain", None, 1.0), (D_MODEL, "gelu", None, 1.0)), (F32, BF16),
                                   ffn_w=(ffn_w_in, ffn_w_out, 3))
    wbd = jnp.stack([jnp.stack([_block_diag_256(lru_w_a[0, d]), _block_diag_256(lru_w_x[0, d])])
                     for d in range(2)]).astype(BF16)
    hs = _lru_scan(xr, lru_conv_w[0], lru_conv_b[0], wbd, lru_b_a[0], lru_b_x[0], lru_lam[0])
    out = _mix_ffn(hx, mod[3], norm_ffn[3], lru_w_out[0].astype(BF16), w_in, w_out,
                   [(hs, 0), (gl, lat_tile0)], lru=True, final_g=norm_final)
    return out.reshape(BATCH, SEQ, D_MODEL)
```
